```python
import math
import jax, jax.numpy as jnp
from jax import lax
import numpy as np

D_MODEL = 1024
BATCH = 2
SEQ = 16384
DEPTH = 4

N_META = 16
N_A_LAYERS = DEPTH // 2
N_B_LAYERS = DEPTH - N_A_LAYERS
RWKV_HEAD = 64
RWKV_HEADS = D_MODEL // RWKV_HEAD
DECAY_LORA = 64
AAA_LORA = 64
MV_LORA = 32
GATE_LORA = 160
RWKV_GN_EPS = 64e-5
DIFF_HEADS = 8
DIFF_HD = D_MODEL // (2 * DIFF_HEADS)
Q_BLOCK = 128
N_EXPERTS = 64
TOP_K = 8
N_GROUPS = 8
TOPK_GROUPS = 4
EXPERT_FF = D_MODEL // 4
SHARED_FF = EXPERT_FF
ROUTED_SCALE = 2.5
MOE_BLOCK = 128
DN_ALPHA = (2 * DEPTH) ** 0.25
DN_BETA = (8 * DEPTH) ** -0.25
LN_EPS = 1e-5

kernel_name = "yoco_rwkv7_diffattn_moe_trunk"


def layer_norm(x, g, b):
    xf = x.astype(jnp.float32)
    mu = xf.mean(-1, keepdims=True)
    var = jnp.mean(jnp.square(xf - mu), -1, keepdims=True)
    return ((xf - mu) * lax.rsqrt(var + LN_EPS) * g + b).astype(x.dtype)


def wkv7_scan(r, w, k, v, a, b):
    B, T, H, N = r.shape
    tm = lambda z: jnp.moveaxis(z.astype(jnp.float32), 1, 0)

    def step(S, inp):
        r_t, w_t, k_t, v_t, a_t, b_t = inp
        Sa = jnp.einsum('bhvk,bhk->bhv', S, a_t)
        S = S * w_t[:, :, None, :] + Sa[..., None] * b_t[:, :, None, :] + v_t[..., None] * k_t[:, :, None, :]
        return S, jnp.einsum('bhvk,bhk->bhv', S, r_t)

    S0 = jnp.zeros((B, H, N, N), jnp.float32)
    _, y = lax.scan(step, S0, (tm(r), tm(w), tm(k), tm(v), tm(a), tm(b)))
    return jnp.moveaxis(y, 0, 1)


def rwkv7_time_mix(x, v_first, mu, w_rkv, w0, w_l1, w_l2, a0, a_l1, a_l2, g_l1, g_l2,
                   k_k, k_a, r_k, lnx_g, lnx_b, w_out, v_res):
    B, T, D = x.shape
    H, N = RWKV_HEADS, RWKV_HEAD
    x_prev = jnp.pad(x, ((0, 0), (1, 0), (0, 0)))[:, :-1]
    xm = x[None] + (x_prev - x)[None] * mu[:, None, None, :]
    r, k, v = jnp.einsum('sbtd,sde->sbte', xm[:3], w_rkv)
    xw, xa, xg = xm[3], xm[4], xm[5]
    w_log = -jax.nn.softplus(-(w0 + jnp.tanh(xw @ w_l1) @ w_l2)) - 0.5
    decay = jnp.exp(-jnp.exp(w_log.astype(jnp.float32)))
    if v_res is not None:
        v0, v_l1, v_l2 = v_res
        v = v + (v_first - v) * jax.nn.sigmoid(v0 + (xm[2] @ v_l1) @ v_l2)
    a = jax.nn.sigmoid(a0 + (xa @ a_l1) @ a_l2)
    g = jax.nn.sigmoid(xg @ g_l1) @ g_l2
    heads = lambda z: z.reshape(B, T, H, N)
    kk = heads(k * k_k).astype(jnp.float32)
    kk = kk / jnp.maximum(jnp.linalg.norm(kk, axis=-1, keepdims=True), 1e-12)
    k = k * (1 + (a - 1) * k_a)
    rh, kh, vh, ah = heads(r), heads(k), heads(v), heads(a)
    y = wkv7_scan(rh, heads(decay), kh, vh, -kk, kk * ah)
    m = y.mean(-1, keepdims=True)
    var = jnp.mean(jnp.square(y - m), -1, keepdims=True)
    yn = ((y - m) * lax.rsqrt(var + RWKV_GN_EPS)).reshape(B, T, D) * lnx_g + lnx_b
    bonus = (jnp.sum((rh * kh * r_k).astype(jnp.float32), -1, keepdims=True) * vh).reshape(B, T, D)
    out = ((yn + bonus) * g).astype(x.dtype) @ w_out
    return out, v


def shared_kv(h, kv_w):
    B, L, D = h.shape
    H, d = DIFF_HEADS, DIFF_HD
    kv = h @ kv_w
    k = jnp.transpose(kv[..., :2 * H * d].reshape(B, L, H, 2, d), (3, 0, 2, 1, 4))
    v = jnp.transpose(kv[..., 2 * H * d:].reshape(B, L, H, 2 * d), (0, 2, 1, 3))
    return k, v


def diff_attention(h, k_sh, v_sh, w_q, lam_q1, lam_k1, lam_q2, lam_k2, subln_g, w_out, layer_idx):
    B, L, D = h.shape
    H, d = DIFF_HEADS, DIFF_HD
    Lp = -(-L // Q_BLOCK) * Q_BLOCK
    q = (h @ w_q).reshape(B, L, H, 2, d)
    q = jnp.pad(q, ((0, 0), (0, Lp - L), (0, 0), (0, 0), (0, 0)))
    q = jnp.transpose(q, (3, 0, 2, 1, 4))
    lam_init = 0.8 - 0.6 * math.exp(-0.3 * layer_idx)
    f32 = jnp.float32
    lam = (jnp.exp(jnp.sum(lam_q1.astype(f32) * lam_k1.astype(f32)))
           - jnp.exp(jnp.sum(lam_q2.astype(f32) * lam_k2.astype(f32))) + lam_init)
    kpos = jnp.arange(L)
    scale = d ** -0.5

    def block(i):
        s = i * Q_BLOCK
        qb = lax.dynamic_slice_in_dim(q, s, Q_BLOCK, axis=3)
        sc = jnp.einsum('cbhqd,cbhkd->cbhqk', qb, k_sh).astype(f32) * scale
        qpos = s + jnp.arange(Q_BLOCK)
        sc = jnp.where(kpos[None, :] <= qpos[:, None], sc, -jnp.inf)
        p = jax.nn.softmax(sc, axis=-1)
        attn = p[0] - lam * p[1]
        return jnp.einsum('bhqk,bhkv->bhqv', attn.astype(v_sh.dtype), v_sh)

    o = lax.map(block, jnp.arange(Lp // Q_BLOCK))
    o = jnp.transpose(o, (1, 0, 3, 2, 4)).reshape(B, Lp, H, 2 * d)[:, :L].astype(f32)
    o = o * lax.rsqrt(jnp.mean(jnp.square(o), -1, keepdims=True) + 1e-5) * subln_g * (1.0 - lam_init)
    return o.reshape(B, L, H * 2 * d).astype(h.dtype) @ w_out


def route(x, router_w, router_b):
    N = x.shape[0]
    s = jax.nn.sigmoid((x @ router_w).astype(jnp.float32))
    sel = s + router_b.astype(jnp.float32)
    gscore = lax.top_k(sel.reshape(N, N_GROUPS, -1), 2)[0].sum(-1)
    _, gidx = lax.top_k(gscore, TOPK_GROUPS)
    gmask = jnp.any(gidx[:, :, None] == jnp.arange(N_GROUPS), axis=1)
    sel = jnp.where(jnp.repeat(gmask, N_EXPERTS // N_GROUPS, axis=1), sel, -jnp.inf)
    _, idx = lax.top_k(sel, TOP_K)
    wts = jnp.take_along_axis(s, idx, axis=1)
    wts = wts / jnp.sum(wts, -1, keepdims=True) * ROUTED_SCALE
    return idx, wts.astype(x.dtype)


def routed_experts(x, idx, gates, w_gate, w_up, w_down):
    N, D = x.shape
    E = w_gate.shape[0]
    NK = N * TOP_K
    n_blocks = -(-NK // MOE_BLOCK) + E
    P = n_blocks * MOE_BLOCK
    flat_e = idx.reshape(-1)
    order = jnp.argsort(flat_e)
    e_sorted = flat_e[order]
    tok_sorted = (order // TOP_K).astype(jnp.int32)
    gate_sorted = gates.reshape(-1)[order]
    counts = jnp.bincount(flat_e, length=E)
    padded = (counts + MOE_BLOCK - 1) // MOE_BLOCK * MOE_BLOCK
    pad_end = jnp.cumsum(padded)
    pad_start = pad_end - padded
    start = jnp.cumsum(counts) - counts
    dest = pad_start[e_sorted] + jnp.arange(NK) - start[e_sorted]
    slot_tok = jnp.full((P,), N, jnp.int32).at[dest].set(tok_sorted)
    slot_gate = jnp.zeros((P,), x.dtype).at[dest].set(gate_sorted)
    block_e = jnp.minimum(jnp.searchsorted(pad_end, jnp.arange(n_blocks) * MOE_BLOCK, side='right'), E - 1)
    x_pad = jnp.concatenate([x, jnp.zeros((1, D), x.dtype)], axis=0)

    def block(args):
        tok, gt, e = args
        xb = x_pad[tok]
        hb = jax.nn.silu(xb @ w_gate[e]) * (xb @ w_up[e])
        return (hb @ w_down[e]) * gt[:, None]

    yb = lax.map(block, (slot_tok.reshape(n_blocks, MOE_BLOCK), slot_gate.reshape(n_blocks, MOE_BLOCK), block_e))
    return jax.ops.segment_sum(yb.reshape(P, D), slot_tok, num_segments=N + 1)[:N]


def moe_ffn(h, router_w, router_b, w_gate, w_up, w_down, sh_gate, sh_up, sh_down):
    B, L, D = h.shape
    xf = h.reshape(-1, D)
    idx, gates = route(xf, router_w, router_b)
    routed = routed_experts(xf, idx, gates, w_gate, w_up, w_down)
    shared = (jax.nn.silu(xf @ sh_gate) * (xf @ sh_up)) @ sh_down
    return (routed + shared).reshape(B, L, D)


def setup_inputs(seed: int = 0) -> dict:
    key = jax.random.key(seed)
    ks = iter(jax.random.split(key, 64))
    f32 = jnp.float32
    D, A, Bn, H, N = D_MODEL, N_A_LAYERS, N_B_LAYERS, RWKV_HEADS, RWKV_HEAD
    d, E, F, Fs = DIFF_HD, N_EXPERTS, EXPERT_FF, SHARED_FF

    def nrm(shape, scale):
        return jax.random.normal(next(ks), shape, f32) * scale

    def unif(shape, lo, hi):
        return jax.random.uniform(next(ks), shape, f32, lo, hi)

    return {
        "x": nrm((BATCH, SEQ, D), 1.0),
        "meta_tokens": nrm((N_META, D), 1.0),
        "ln_mix_g": 1.0 + nrm((DEPTH, D), 0.02),
        "ln_mix_b": nrm((DEPTH, D), 0.02),
        "ln_ffn_g": 1.0 + nrm((DEPTH, D), 0.02),
        "ln_ffn_b": nrm((DEPTH, D), 0.02),
        "rw_mu": unif((A, 6, D), 0.0, 1.0),
        "rw_w_rkv": nrm((A, 3, D, D), D ** -0.5),
        "rw_w0": nrm((A, D), 0.5),
        "rw_w_l1": nrm((A, D, DECAY_LORA), D ** -0.5),
        "rw_w_l2": nrm((A, DECAY_LORA, D), 0.3 * DECAY_LORA ** -0.5),
        "rw_a0": nrm((A, D), 0.5),
        "rw_a_l1": nrm((A, D, AAA_LORA), D ** -0.5),
        "rw_a_l2": nrm((A, AAA_LORA, D), 0.3 * AAA_LORA ** -0.5),
        "rw_g_l1": nrm((A, D, GATE_LORA), D ** -0.5),
        "rw_g_l2": nrm((A, GATE_LORA, D), GATE_LORA ** -0.5),
        "rw_k_k": 0.85 + nrm((A, D), 0.05),
        "rw_k_a": 1.0 + nrm((A, D), 0.05),
        "rw_r_k": nrm((A, H, N), 0.1),
        "rw_lnx_g": 1.0 + nrm((A, D), 0.02),
        "rw_lnx_b": nrm((A, D), 0.02),
        "rw_w_out": nrm((A, D, D), D ** -0.5 * DN_BETA),
        "rw_v0": nrm((A - 1, D), 0.5),
        "rw_v_l1": nrm((A - 1, D, MV_LORA), D ** -0.5),
        "rw_v_l2": nrm((A - 1, MV_LORA, D), 0.3 * MV_LORA ** -0.5),
        "kv_w": nrm((D, 4 * DIFF_HEADS * d), D ** -0.5),
        "da_w_q": nrm((Bn, D, 2 * DIFF_HEADS * d), D ** -0.5),
        "da_lam_q1": nrm((Bn, d), 0.1),
        "da_lam_k1": nrm((Bn, d), 0.1),
        "da_lam_q2": nrm((Bn, d), 0.1),
        "da_lam_k2": nrm((Bn, d), 0.1),
        "da_subln_g": 1.0 + nrm((Bn, 2 * d), 0.02),
        "da_w_out": nrm((Bn, 2 * DIFF_HEADS * d, D), (2 * DIFF_HEADS * d) ** -0.5 * DN_BETA),
        "moe_router_w": nrm((DEPTH, D, E), D ** -0.5),
        "moe_router_b": nrm((DEPTH, E), 0.01),
        "moe_w_gate": nrm((DEPTH, E, D, F), D ** -0.5),
        "moe_w_up": nrm((DEPTH, E, D, F), D ** -0.5),
        "moe_w_down": nrm((DEPTH, E, F, D), F ** -0.5 * DN_BETA),
        "moe_sh_gate": nrm((DEPTH, D, Fs), D ** -0.5),
        "moe_sh_up": nrm((DEPTH, D, Fs), D ** -0.5),
        "moe_sh_down": nrm((DEPTH, Fs, D), Fs ** -0.5 * DN_BETA),
    }


def reference(x, meta_tokens, ln_mix_g, ln_mix_b, ln_ffn_g, ln_ffn_b,
              rw_mu, rw_w_rkv, rw_w0, rw_w_l1, rw_w_l2, rw_a0, rw_a_l1, rw_a_l2,
              rw_g_l1, rw_g_l2, rw_k_k, rw_k_a, rw_r_k, rw_lnx_g, rw_lnx_b, rw_w_out,
              rw_v0, rw_v_l1, rw_v_l2, kv_w,
              da_w_q, da_lam_q1, da_lam_k1, da_lam_q2, da_lam_k2, da_subln_g, da_w_out,
              moe_router_w, moe_router_b, moe_w_gate, moe_w_up, moe_w_down,
              moe_sh_gate, moe_sh_up, moe_sh_down):
    B, _, D = x.shape
    meta = jnp.broadcast_to(meta_tokens[None].astype(x.dtype), (B, N_META, D))
    h = jnp.concatenate([meta, x], axis=1)
    v_first = None
    k_sh = v_sh = None
    for l in range(DEPTH):
        if l < N_A_LAYERS:
            v_res = None if l == 0 else (rw_v0[l - 1], rw_v_l1[l - 1], rw_v_l2[l - 1])
            mix, v = rwkv7_time_mix(h, v_first, rw_mu[l], rw_w_rkv[l], rw_w0[l], rw_w_l1[l], rw_w_l2[l],
                                    rw_a0[l], rw_a_l1[l], rw_a_l2[l], rw_g_l1[l], rw_g_l2[l],
                                    rw_k_k[l], rw_k_a[l], rw_r_k[l], rw_lnx_g[l], rw_lnx_b[l],
                                    rw_w_out[l], v_res)
            if l == 0:
                v_first = v
        else:
            if l == N_A_LAYERS:
                k_sh, v_sh = shared_kv(h, kv_w)
            j = l - N_A_LAYERS
            mix = diff_attention(h, k_sh, v_sh, da_w_q[j], da_lam_q1[j], da_lam_k1[j], da_lam_q2[j],
                                 da_lam_k2[j], da_subln_g[j], da_w_out[j], l)
        h = layer_norm(DN_ALPHA * h + mix, ln_mix_g[l], ln_mix_b[l])
        ffn = moe_ffn(h, moe_router_w[l], moe_router_b[l], moe_w_gate[l], moe_w_up[l], moe_w_down[l],
                      moe_sh_gate[l], moe_sh_up[l], moe_sh_down[l])
        h = layer_norm(DN_ALPHA * h + ffn, ln_ffn_g[l], ln_ffn_b[l])
    return h[:, N_META:]
```

```python
import functools
import math

import numpy as np
import jax
import jax.numpy as jnp
from jax import lax
from jax.experimental import pallas as pl
from jax.experimental.pallas import tpu as pltpu

F32 = jnp.float32
BF16 = jnp.bfloat16
I32 = jnp.int32

D_MODEL = 1024
DEPTH = 4
N_META = 16
N_A_LAYERS = DEPTH // 2
RWKV_HEAD = 64
RWKV_HEADS = D_MODEL // RWKV_HEAD
RWKV_GN_EPS = 64e-5
DIFF_HEADS = 8
DIFF_HD = D_MODEL // (2 * DIFF_HEADS)
N_EXPERTS = 64
TOP_K = 8
N_GROUPS = 8
TOPK_GROUPS = 4
EXPERT_FF = D_MODEL // 4
ROUTED_SCALE = 2.5
DN_ALPHA = (2 * DEPTH) ** 0.25
LN_EPS = 1e-5

LANES = 128
SUBLANES = 8
VMEM_LIMIT_BYTES = 56 * 1024 * 1024

SEQ_ALIGN = 256
PROJ_TM = 256
ROW_TM = 512
SCAN_TC = 128
SCAN_SUB = SUBLANES
ATT_T = 256
ROUTE_TM = 256
MOE_BLK = 256
DISP_TM = 512
COMB_TM = 128

_SIGMA = (0, 4, 1, 5, 2, 6, 3, 7)


def _cparams(sem):
    return pltpu.CompilerParams(dimension_semantics=sem, vmem_limit_bytes=VMEM_LIMIT_BYTES)


def _dot(a, b):
    return jnp.dot(a, b, preferred_element_type=F32)


def _dot_hi(a, b):
    return jnp.dot(a, b, preferred_element_type=F32, precision=lax.Precision.HIGHEST)


def _full(shape):
    nd = len(shape)
    return pl.BlockSpec(shape, lambda *_: (0,) * nd)


def _layer_norm(y, g, b):
    mu = jnp.mean(y, axis=-1, keepdims=True)
    d = y - mu
    var = jnp.mean(d * d, axis=-1, keepdims=True)
    return d * lax.rsqrt(var + LN_EPS) * g + b


def _perm_cols():
    L = np.arange(D_MODEL)
    j, p, hh = L // LANES, (L % LANES) // 16, L % 16
    i = 8 * j + np.asarray(_SIGMA)[p]
    return (hh * RWKV_HEAD + i).astype(np.int32)


def _head_sum_mats():
    L = np.arange(D_MODEL)
    c = np.arange(LANES)
    gs = (L[:, None] % 16 == c[None, :] % 16).astype(np.float32)
    gb = ((c[:, None] < 16) & (c[:, None] == L[None, :] % 16)).astype(np.float32)
    return gs, gb


def _rwkv_proj_kernel(has_vres, *refs):
    if has_vres:
        (x_ref, xp_ref, mu_ref, wr_ref, wk_ref, wv_ref, w0_ref, wl1_ref, wl2_ref, a0_ref, al1_ref,
         al2_ref, gl1_ref, gl2_ref, kk_ref, ka_ref, gs_ref, gb_ref, vf_ref, v0_ref, vl1_ref, vl2_ref,
         r_o, w_o, k_o, v_o, a_o, b_o, g_o) = refs
    else:
        (x_ref, xp_ref, mu_ref, wr_ref, wk_ref, wv_ref, w0_ref, wl1_ref, wl2_ref, a0_ref, al1_ref,
         al2_ref, gl1_ref, gl2_ref, kk_ref, ka_ref, gs_ref, gb_ref,
         r_o, w_o, k_o, v_o, a_o, b_o, g_o) = refs
    i = pl.program_id(1)
    x = x_ref[0]
    prev = xp_ref[0][SUBLANES - 1:SUBLANES, :]
    prev = jnp.where(i == 0, 0.0, prev)
    row = lax.broadcasted_iota(I32, x.shape, 0)
    xprev = jnp.where(row == 0, prev, pltpu.roll(x, 1, 0))
    dx = xprev - x

    def mix(s):
        return (x + dx * mu_ref[s:s + 1, :]).astype(BF16)

    xv = mix(2)
    r = _dot(mix(0), wr_ref[...])
    k = _dot(mix(1), wk_ref[...])
    v = _dot(xv, wv_ref[...])
    zw = w0_ref[...] + _dot(jnp.tanh(_dot(mix(3), wl1_ref[...])).astype(BF16), wl2_ref[...])
    decay = jnp.exp(-math.exp(-0.5) * jax.nn.sigmoid(zw))
    a = jax.nn.sigmoid(a0_ref[...] + _dot(_dot(mix(4), al1_ref[...]).astype(BF16), al2_ref[...]))
    g = _dot(jax.nn.sigmoid(_dot(mix(5), gl1_ref[...])).astype(BF16), gl2_ref[...])
    if has_vres:
        gate_v = jax.nn.sigmoid(v0_ref[...] + _dot(_dot(xv, vl1_ref[...]).astype(BF16), vl2_ref[...]))
        v = v + (vf_ref[0] - v) * gate_v
    kk = k * kk_ref[...]
    ss = _dot_hi(kk * kk, gs_ref[...])
    inv = 1.0 / jnp.maximum(jnp.sqrt(ss), 1e-12)
    kk = kk * _dot_hi(inv, gb_ref[...])
    k = k * (1.0 + (a - 1.0) * ka_ref[...])
    r_o[0] = r
    w_o[0] = decay
    k_o[0] = k
    v_o[0] = v
    a_o[0] = -kk
    b_o[0] = kk * a
    g_o[0] = g


def _rwkv_proj(h, vfirst, p):
    B, Lp, D = h.shape
    TM = PROJ_TM
    has_vres = vfirst is not None
    tile = pl.BlockSpec((1, TM, D), lambda b, i: (b, i, 0))
    prev8 = pl.BlockSpec((1, SUBLANES, D), lambda b, i: (b, jnp.maximum(i * (TM // SUBLANES) - 1, 0), 0))
    names = ["mu", "wr", "wk", "wv", "w0", "wl1", "wl2", "a0", "al1", "al2", "gl1", "gl2", "kk", "ka", "gs", "gb"]
    args = [h, h] + [p[n] for n in names]
    specs = [tile, prev8] + [_full(p[n].shape) for n in names]
    if has_vres:
        args += [vfirst, p["v0"], p["vl1"], p["vl2"]]
        specs += [tile, _full(p["v0"].shape), _full(p["vl1"].shape), _full(p["vl2"].shape)]
    out = jax.ShapeDtypeStruct((B, Lp, D), F32)
    return pl.pallas_call(
        functools.partial(_rwkv_proj_kernel, has_vres),
        out_shape=[out] * 7,
        grid=(B, Lp // TM),
        in_specs=specs,
        out_specs=[tile] * 7,
        compiler_params=_cparams(("parallel", "arbitrary")),
        name="rwkv_proj",
    )(*args)


def _wkv_kernel(r_ref, w_ref, k_ref, v_ref, a_ref, b_ref, rk_ref, lg_ref, lb_ref, o_ref,
                s_ref, ma_ref, mwr_ref, mw_ref, mb_ref, mk_ref, zv_ref, br_ref, kr_ref, rkk_ref, yc_ref):
    NV = RWKV_HEAD
    NCOL = 2 * D_MODEL // LANES
    NRB = NV // SUBLANES

    @pl.when(pl.program_id(0) == 0)
    def _():
        s_ref[...] = jnp.zeros_like(s_ref)

    lane = lax.broadcasted_iota(I32, (SUBLANES, LANES), 1)
    even = ((lane // 16) % 2) == 0
    grp = lane // 32

    def cs(j):
        return slice(j * LANES, (j + 1) * LANES)

    def merge(ref, t0):
        x0 = ref[0, pl.ds(t0, SUBLANES), :]
        x1 = ref[1, pl.ds(t0, SUBLANES), :]
        cols = []
        for j in range(D_MODEL // LANES):
            a0 = x0[:, cs(j)]
            a1 = x1[:, cs(j)]
            cols.append(jnp.where(even, a0, pltpu.roll(a1, 16, 1)))
            cols.append(jnp.where(even, pltpu.roll(a0, LANES - 16, 1), a1))
        return cols

    def fold(x):
        return (x + pltpu.roll(x, 32, 1)) + (pltpu.roll(x, 64, 1) + pltpu.roll(x, 96, 1))

    def colsum(xs):
        acc = xs[0]
        for x in xs[1:]:
            acc = acc + x
        return acc

    def sub_chunk(c, carry):
        t0 = pl.multiple_of(c * SCAN_SUB, SCAN_SUB)
        R = merge(r_ref, t0)
        W = merge(w_ref, t0)
        K = merge(k_ref, t0)
        V = merge(v_ref, t0)
        A = merge(a_ref, t0)
        Bm = merge(b_ref, t0)
        per_step = pl.ds(0, SCAN_SUB, stride=SUBLANES)
        for j in range(NCOL):
            ma_ref[j, per_step, :] = A[j]
            mw_ref[j, per_step, :] = W[j]
            mwr_ref[j, per_step, :] = W[j] * R[j]
            mb_ref[j, per_step, :] = Bm[j]
            mk_ref[j, per_step, :] = K[j]
        br_ref[per_step, :] = fold(colsum([Bm[j] * R[j] for j in range(NCOL)]))
        kr_ref[per_step, :] = fold(colsum([K[j] * R[j] for j in range(NCOL)]))
        rkk_ref[per_step, :] = fold(colsum([K[j] * R[j] * rk_ref[:, cs(j)] for j in range(NCOL)]))
        for j in range(NCOL):
            for q in range(4):
                vi = 4 * j + q
                z = fold(jnp.where(grp == q, V[j], 0.0))
                zv_ref[pl.ds(vi, SUBLANES, stride=NV), :] = z

        def step(t, carry2):
            trow = pl.ds(pl.multiple_of(t * SUBLANES, SUBLANES), 1)
            acc_a = [None] * NRB
            acc_y = [None] * NRB
            for j in range(NCOL):
                a_row = ma_ref[j, trow, :]
                wr_row = mwr_ref[j, trow, :]
                for i in range(NRB):
                    s = s_ref[i * SUBLANES:(i + 1) * SUBLANES, cs(j)]
                    pa = s * a_row
                    py = s * wr_row
                    acc_a[i] = pa if j == 0 else acc_a[i] + pa
                    acc_y[i] = py if j == 0 else acc_y[i] + py
            sa = [fold(x) for x in acc_a]
            yp = [fold(x) for x in acc_y]
            base = pl.multiple_of(t * NV, NV)
            vcol = [zv_ref[pl.ds(base + i * SUBLANES, SUBLANES), :] for i in range(NRB)]
            for j in range(NCOL):
                w_row = mw_ref[j, trow, :]
                b_row = mb_ref[j, trow, :]
                k_row = mk_ref[j, trow, :]
                for i in range(NRB):
                    blk = (slice(i * SUBLANES, (i + 1) * SUBLANES), cs(j))
                    s_ref[blk] = s_ref[blk] * w_row + sa[i] * b_row + vcol[i] * k_row
            brt = br_ref[trow, :]
            krt = kr_ref[trow, :]
            rkt = rkk_ref[trow, :]
            y = [yp[i] + sa[i] * brt + vcol[i] * krt for i in range(NRB)]
            mean = jnp.sum(colsum(y), axis=0, keepdims=True) * (1.0 / NV)
            d = [yi - mean for yi in y]
            var = jnp.sum(colsum([di * di for di in d]), axis=0, keepdims=True) * (1.0 / NV)
            inv = lax.rsqrt(var + RWKV_GN_EPS)
            for i in range(NRB):
                rows = slice(i * SUBLANES, (i + 1) * SUBLANES)
                out = d[i] * inv * lg_ref[rows, :] + lb_ref[rows, :] + vcol[i] * rkt
                yc_ref[pl.ds(base + i * SUBLANES, SUBLANES), :] = out
            return carry2

        lax.fori_loop(0, SCAN_SUB, step, 0)

        cols = []
        for j in range(NCOL):
            acc = None
            for q in range(4):
                z = yc_ref[pl.ds(4 * j + q, SUBLANES, stride=NV), :]
                acc = z if q == 0 else jnp.where(grp == q, z, acc)
            cols.append(acc)
        for j in range(D_MODEL // LANES):
            c0, c1 = cols[2 * j], cols[2 * j + 1]
            o_ref[0, pl.ds(t0, SUBLANES), cs(j)] = jnp.where(even, c0, pltpu.roll(c1, 16, 1))
            o_ref[1, pl.ds(t0, SUBLANES), cs(j)] = jnp.where(even, pltpu.roll(c0, LANES - 16, 1), c1)
        return carry

    lax.fori_loop(0, SCAN_TC // SCAN_SUB, sub_chunk, 0)


def _wkv_scan(r, w, k, v, a, b, rk_m, lg_t, lb_t):
    B, Lp, D = r.shape
    assert B == 2, "the scan packs exactly two batch rows into the lane dimension"
    blk = pl.BlockSpec((B, SCAN_TC, D), lambda i: (0, i, 0))
    vm = lambda *s: pltpu.VMEM(s, F32)
    return pl.pallas_call(
        _wkv_kernel,
        out_shape=jax.ShapeDtypeStruct((B, Lp, D), F32),
        grid=(Lp // SCAN_TC,),
        in_specs=[blk] * 6 + [_full(rk_m.shape), _full(lg_t.shape), _full(lb_t.shape)],
        out_specs=blk,
        scratch_shapes=[vm(RWKV_HEAD, 2 * D)] + [vm(2 * D // LANES, SCAN_SUB * SUBLANES, LANES)] * 5
        + [vm(SCAN_SUB * RWKV_HEAD, LANES)] + [vm(SCAN_SUB * SUBLANES, LANES)] * 3
        + [vm(SCAN_SUB * RWKV_HEAD, LANES)],
        compiler_params=_cparams(("arbitrary",)),
        name="wkv_scan",
    )(r, w, k, v, a, b, rk_m, lg_t, lb_t)


def _mm_res_ln_kernel(has_gate, *refs):
    if has_gate:
        z_ref, g_ref, w_ref, h_ref, lg_ref, lb_ref, o_ref = refs
        z = (z_ref[...] * g_ref[...]).astype(BF16)
    else:
        z_ref, w_ref, h_ref, lg_ref, lb_ref, o_ref = refs
        z = z_ref[...].astype(BF16)
    y = DN_ALPHA * h_ref[...] + _dot(z, w_ref[...])
    o_ref[...] = _layer_norm(y, lg_ref[...], lb_ref[...])


def _mm_res_ln(z, gate, w, h, lg, lb):
    Np, D = h.shape
    TM = ROW_TM
    tile = pl.BlockSpec((TM, D), lambda i: (i, 0))
    has_gate = gate is not None
    args = [z] + ([gate] if has_gate else []) + [w, h, lg, lb]
    specs = [tile] + ([tile] if has_gate else []) + [_full(w.shape), tile, _full(lg.shape), _full(lb.shape)]
    return pl.pallas_call(
        functools.partial(_mm_res_ln_kernel, has_gate),
        out_shape=jax.ShapeDtypeStruct((Np, D), F32),
        grid=(Np // TM,),
        in_specs=specs,
        out_specs=tile,
        compiler_params=_cparams(("parallel",)),
        name="mm_res_ln",
    )(*args)


def _proj_kernel(x_ref, w_ref, o_ref):
    o_ref[...] = _dot(x_ref[...].astype(BF16), w_ref[...]).astype(o_ref.dtype)


def _proj(x, w):
    Np, D = x.shape
    Nout = w.shape[1]
    TM = ROW_TM
    return pl.pallas_call(
        _proj_kernel,
        out_shape=jax.ShapeDtypeStruct((Np, Nout), BF16),
        grid=(Np // TM,),
        in_specs=[pl.BlockSpec((TM, D), lambda i: (i, 0)), _full(w.shape)],
        out_specs=pl.BlockSpec((TM, Nout), lambda i: (i, 0)),
        compiler_params=_cparams(("parallel",)),
        name="proj",
    )(x, w)


def _attn_kernel(lam_init, lam_ref, sg_ref, q_ref, k_ref, v_ref, o_ref,
                 m1_ref, l1_ref, acc1_ref, m2_ref, l2_ref, acc2_ref):
    T = ATT_T
    qi = pl.program_id(2)
    q = q_ref[0]
    lane = lax.broadcasted_iota(I32, q.shape, 1)
    zero = jnp.zeros_like(q)
    q1 = jnp.where(lane < DIFF_HD, q, zero)
    q2 = jnp.where(lane >= DIFF_HD, q, zero)
    m1_ref[...] = jnp.full_like(m1_ref, -1e30)
    m2_ref[...] = jnp.full_like(m2_ref, -1e30)
    l1_ref[...] = jnp.zeros_like(l1_ref)
    l2_ref[...] = jnp.zeros_like(l2_ref)
    acc1_ref[...] = jnp.zeros_like(acc1_ref)
    acc2_ref[...] = jnp.zeros_like(acc2_ref)
    nt = (((1,), (1,)), ((), ()))

    def update(s, vb, m_ref, l_ref, acc_ref):
        m_old = m_ref[...]
        m_new = jnp.maximum(m_old, jnp.max(s, axis=1, keepdims=True))
        alpha = jnp.exp(m_old - m_new)
        p = jnp.exp(s - m_new[:, :1])
        l_ref[...] = alpha * l_ref[...] + jnp.sum(p, axis=1, keepdims=True)
        acc_ref[...] = alpha * acc_ref[...] + _dot(p.astype(BF16), vb)
        m_ref[...] = m_new

    def chunk(kc, masked):
        k0 = pl.multiple_of(kc * T, T)
        kb = k_ref[0, pl.ds(k0, T), :]
        vb = v_ref[0, pl.ds(k0, T), :]
        s1 = lax.dot_general(q1, kb, nt, preferred_element_type=F32)
        s2 = lax.dot_general(q2, kb, nt, preferred_element_type=F32)
        if masked:
            rr = lax.broadcasted_iota(I32, s1.shape, 0)
            cc = lax.broadcasted_iota(I32, s1.shape, 1)
            keep = cc <= rr
            s1 = jnp.where(keep, s1, -jnp.inf)
            s2 = jnp.where(keep, s2, -jnp.inf)
        update(s1, vb, m1_ref, l1_ref, acc1_ref)
        update(s2, vb, m2_ref, l2_ref, acc2_ref)

    def body(kc, carry):
        chunk(kc, False)
        return carry

    lax.fori_loop(0, qi, body, 0)
    chunk(qi, True)

    lam_v = lam_ref[...]
    lam = (jnp.exp(jnp.sum(lam_v[0:1] * lam_v[1:2], axis=1, keepdims=True))
           - jnp.exp(jnp.sum(lam_v[2:3] * lam_v[3:4], axis=1, keepdims=True)) + lam_init)
    o = acc1_ref[...] / l1_ref[...] - lam * (acc2_ref[...] / l2_ref[...])
    o = o * lax.rsqrt(jnp.mean(o * o, axis=1, keepdims=True) + 1e-5) * sg_ref[...] * (1.0 - lam_init)
    o_ref[0] = o.astype(o_ref.dtype)


def _diff_attn(q, kv, lam4, subln_g, lam_init):
    B, Lp, D = q.shape
    T = ATT_T
    H = DIFF_HEADS
    HW = 2 * DIFF_HD
    vm = lambda *s: pltpu.VMEM(s, F32)
    return pl.pallas_call(
        functools.partial(_attn_kernel, lam_init),
        out_shape=jax.ShapeDtypeStruct((B, Lp, D), BF16),
        grid=(B, H, Lp // T),
        in_specs=[_full(lam4.shape), _full(subln_g.shape),
                  pl.BlockSpec((1, T, HW), lambda b, h, i: (b, i, h)),
                  pl.BlockSpec((1, Lp, HW), lambda b, h, i: (b, 0, h)),
                  pl.BlockSpec((1, Lp, HW), lambda b, h, i: (b, 0, H + h))],
        out_specs=pl.BlockSpec((1, T, HW), lambda b, h, i: (b, i, h)),
        scratch_shapes=[vm(T, LANES), vm(T, LANES), vm(T, HW), vm(T, LANES), vm(T, LANES), vm(T, HW)],
        compiler_params=_cparams(("parallel", "parallel", "arbitrary")),
        name="diff_attn",
    )(lam4, subln_g, q, kv, kv)


def _route_kernel(x_ref, rw_ref, rb_ref, sg_ref, su_ref, sd_ref, tri_ref,
                  sh_o, idx_o, gate_o, rank_o, cnt_o, carry_ref):
    TM = ROUTE_TM
    G, EG = N_GROUPS, N_EXPERTS // N_GROUPS

    @pl.when(pl.program_id(0) == 0)
    def _():
        carry_ref[...] = jnp.zeros_like(carry_ref)

    x = x_ref[...]
    xb = x.astype(BF16)
    hmid = _dot(xb, sg_ref[...])
    hmid = hmid * jax.nn.sigmoid(hmid) * _dot(xb, su_ref[...])
    sh_o[...] = _dot(hmid.astype(BF16), sd_ref[...])

    logit = lax.dot_general(rw_ref[...], x, (((1,), (1,)), ((), ())),
                            preferred_element_type=F32, precision=lax.Precision.HIGHEST)
    s = jax.nn.sigmoid(logit)
    s3 = s.reshape(G, EG, TM)
    sel3 = (s + rb_ref[...]).reshape(G, EG, TM)
    io_j = lax.broadcasted_iota(I32, (G, EG, TM), 1)
    io_g = lax.broadcasted_iota(I32, (G, 1, TM), 0)
    neg = -jnp.inf
    m1 = jnp.max(sel3, axis=1, keepdims=True)
    i1 = jnp.min(jnp.where(sel3 == m1, io_j, EG), axis=1, keepdims=True)
    m2 = jnp.max(jnp.where(io_j == i1, neg, sel3), axis=1, keepdims=True)
    gsc = m1 + m2
    gkeep = jnp.zeros((G, 1, TM), F32)
    for _ in range(TOPK_GROUPS):
        m = jnp.max(gsc, axis=0, keepdims=True)
        gi = jnp.min(jnp.where(gsc == m, io_g, G), axis=0, keepdims=True)
        hit = io_g == gi
        gkeep = jnp.where(hit, 1.0, gkeep)
        gsc = jnp.where(hit, neg, gsc)
    cur = jnp.where(gkeep > 0.0, sel3, neg)
    io_e = io_g * EG + io_j
    hits, idxs, ws = [], [], []
    for _ in range(TOP_K):
        m = jnp.max(jnp.max(cur, axis=1, keepdims=True), axis=0, keepdims=True)
        ei = jnp.min(jnp.min(jnp.where(cur == m, io_e, N_EXPERTS), axis=1, keepdims=True), axis=0, keepdims=True)
        hit = io_e == ei
        ws.append(jnp.sum(jnp.sum(jnp.where(hit, s3, 0.0), axis=1, keepdims=True), axis=0, keepdims=True))
        cur = jnp.where(hit, neg, cur)
        hits.append(hit)
        idxs.append(ei)
    wsum = ws[0]
    for wv in ws[1:]:
        wsum = wsum + wv
    scale = ROUTED_SCALE / wsum
    onehot = jnp.zeros((G, EG, TM), F32)
    for hit in hits:
        onehot = jnp.where(hit, 1.0, onehot)
    oh2 = onehot.reshape(N_EXPERTS, TM)
    rank_full = (_dot(oh2.astype(BF16), tri_ref[...]) + carry_ref[:, :1]).reshape(G, EG, TM)
    for kk in range(TOP_K):
        rk = jnp.sum(jnp.sum(jnp.where(hits[kk], rank_full, 0.0), axis=1, keepdims=True), axis=0, keepdims=True)
        idx_o[kk:kk + 1, :] = idxs[kk].reshape(1, TM)
        gate_o[kk:kk + 1, :] = (ws[kk] * scale).reshape(1, TM)
        rank_o[kk:kk + 1, :] = rk.reshape(1, TM).astype(I32)
    carry_ref[...] = carry_ref[...] + jnp.sum(oh2, axis=1, keepdims=True)
    cnt_o[...] = carry_ref[...]


def _route_shared(x, rw_t, rb, sg, su, sd, tri):
    Np, D = x.shape
    TM = ROUTE_TM
    tok = pl.BlockSpec((TOP_K, TM), lambda i: (0, i))
    return pl.pallas_call(
        _route_kernel,
        out_shape=[jax.ShapeDtypeStruct((Np, D), F32),
                   jax.ShapeDtypeStruct((TOP_K, Np), I32),
                   jax.ShapeDtypeStruct((TOP_K, Np), F32),
                   jax.ShapeDtypeStruct((TOP_K, Np), I32),
                   jax.ShapeDtypeStruct((N_EXPERTS, LANES), F32)],
        grid=(Np // TM,),
        in_specs=[pl.BlockSpec((TM, D), lambda i: (i, 0)), _full(rw_t.shape), _full(rb.shape),
                  _full(sg.shape), _full(su.shape), _full(sd.shape), _full(tri.shape)],
        out_specs=[pl.BlockSpec((TM, D), lambda i: (i, 0)), tok, tok, tok, _full((N_EXPERTS, LANES))],
        scratch_shapes=[pltpu.VMEM((N_EXPERTS, LANES), F32)],
        compiler_params=_cparams(("arbitrary",)),
        name="moe_route",
    )(x, rw_t, rb, sg, su, sd, tri)


def _row_copy(src_hbm, src_row, dst_ref, dst_row, sem):
    return pltpu.make_async_copy(src_hbm.at[pl.ds(src_row, 1)], dst_ref.at[pl.ds(dst_row, 1)], sem)


def _dispatch_kernel(ps_ref, idx_ref, rank_ref, x_hbm, xs_hbm, sem):
    n0 = pl.program_id(0) * DISP_TM

    def issue(n, c):
        for kk in range(TOP_K):
            dst = ps_ref[idx_ref[kk, n]] + rank_ref[kk, n]
            _row_copy(x_hbm, n0 + n, xs_hbm, dst, sem).start()
        return c

    lax.fori_loop(0, DISP_TM, issue, 0)

    def drain(n, c):
        for kk in range(TOP_K):
            _row_copy(x_hbm, 0, xs_hbm, 0, sem).wait()
        return c

    lax.fori_loop(0, DISP_TM, drain, 0)


def _dispatch(x, idx_t, rank_t, pad_start, n_rows):
    Np, D = x.shape
    tok = pl.BlockSpec((TOP_K, DISP_TM), lambda i, ps: (0, i), memory_space=pltpu.SMEM)
    return pl.pallas_call(
        _dispatch_kernel,
        out_shape=jax.ShapeDtypeStruct((n_rows, D), F32),
        grid_spec=pltpu.PrefetchScalarGridSpec(
            num_scalar_prefetch=1,
            grid=(Np // DISP_TM,),
            in_specs=[tok, tok, pl.BlockSpec(memory_space=pl.ANY)],
            out_specs=pl.BlockSpec(memory_space=pl.ANY),
            scratch_shapes=[pltpu.SemaphoreType.DMA(())],
        ),
        compiler_params=_cparams(("arbitrary",)),
        name="moe_dispatch",
    )(pad_start, idx_t, rank_t, x)


def _expert_kernel(be_ref, nu_ref, xs_ref, wg_ref, wu_ref, wd_ref, ys_ref):
    @pl.when(pl.program_id(0) < nu_ref[0])
    def _():
        x = xs_ref[...].astype(BF16)
        g = _dot(x, wg_ref[0])
        hmid = g * jax.nn.sigmoid(g) * _dot(x, wu_ref[0])
        ys_ref[...] = _dot(hmid.astype(BF16), wd_ref[0])


def _experts(xs, block_e, n_used, wg, wu, wd):
    P, D = xs.shape
    F = wg.shape[2]
    nb = P // MOE_BLK
    rows = pl.BlockSpec((MOE_BLK, D), lambda i, be, nu: (jnp.minimum(i, nu[0] - 1), 0))
    wspec = lambda s: pl.BlockSpec((1,) + s, lambda i, be, nu: (be[jnp.minimum(i, nu[0] - 1)], 0, 0))
    return pl.pallas_call(
        _expert_kernel,
        out_shape=jax.ShapeDtypeStruct((P, D), F32),
        grid_spec=pltpu.PrefetchScalarGridSpec(
            num_scalar_prefetch=2,
            grid=(nb,),
            in_specs=[rows, wspec((D, F)), wspec((D, F)), wspec((F, D))],
            out_specs=rows,
        ),
        compiler_params=_cparams(("arbitrary",)),
        name="moe_experts",
    )(block_e, n_used, xs, wg, wu, wd)


def _combine_kernel(ps_ref, idx_ref, rank_ref, gate_ref, sh_ref, h_ref, lg_ref, lb_ref, ys_hbm, o_ref,
                    buf, sem):
    def issue(n, c):
        for kk in range(TOP_K):
            src = ps_ref[idx_ref[kk, n]] + rank_ref[kk, n]
            _row_copy(ys_hbm, src, buf.at[kk], n, sem).start()
        return c

    lax.fori_loop(0, COMB_TM, issue, 0)

    def drain(n, c):
        for kk in range(TOP_K):
            _row_copy(ys_hbm, 0, buf.at[kk], n, sem).wait()
        return c

    lax.fori_loop(0, COMB_TM, drain, 0)
    gate = gate_ref[...]
    ffn = sh_ref[...]
    for kk in range(TOP_K):
        ffn = ffn + gate[:, kk:kk + 1] * buf[kk]
    o_ref[...] = _layer_norm(DN_ALPHA * h_ref[...] + ffn, lg_ref[...], lb_ref[...])


def _combine(ys, idx_t, rank_t, pad_start, gate, shared, h, lg, lb):
    Np, D = h.shape
    TM = COMB_TM
    tok = pl.BlockSpec((TOP_K, TM), lambda i, ps: (0, i), memory_space=pltpu.SMEM)
    tile = pl.BlockSpec((TM, D), lambda i, ps: (i, 0))
    vec = pl.BlockSpec((1, D), lambda i, ps: (0, 0))
    return pl.pallas_call(
        _combine_kernel,
        out_shape=jax.ShapeDtypeStruct((Np, D), F32),
        grid_spec=pltpu.PrefetchScalarGridSpec(
            num_scalar_prefetch=1,
            grid=(Np // TM,),
            in_specs=[tok, tok, pl.BlockSpec((TM, TOP_K), lambda i, ps: (i, 0)), tile, tile, vec, vec,
                      pl.BlockSpec(memory_space=pl.ANY)],
            out_specs=tile,
            scratch_shapes=[pltpu.VMEM((TOP_K, TM, D), F32), pltpu.SemaphoreType.DMA(())],
        ),
        compiler_params=_cparams(("arbitrary",)),
        name="moe_combine",
    )(pad_start, idx_t, rank_t, gate, shared, h, lg, lb, ys)


def _moe_layer(h2, l, lg, lb, router_w, router_b, w_gate, w_up, w_down, sh_gate, sh_up, sh_down, tri):
    Np, D = h2.shape
    shared, idx_t, gate_t, rank_t, cnt = _route_shared(
        h2, router_w[l].T, router_b[l].reshape(N_EXPERTS, 1),
        sh_gate[l].astype(BF16), sh_up[l].astype(BF16), sh_down[l].astype(BF16), tri)
    counts = cnt[:, 0].astype(I32)
    padded = (counts + MOE_BLK - 1) // MOE_BLK * MOE_BLK
    pad_end = jnp.cumsum(padded)
    pad_start = (pad_end - padded).astype(I32)
    n_blocks = Np * TOP_K // MOE_BLK + N_EXPERTS
    n_used = (pad_end[-1:] // MOE_BLK).astype(I32)
    block_e = jnp.minimum(
        jnp.searchsorted(pad_end, jnp.arange(n_blocks, dtype=I32) * MOE_BLK, side="right"), N_EXPERTS - 1).astype(I32)
    xs = _dispatch(h2, idx_t, rank_t, pad_start, n_blocks * MOE_BLK)
    ys = _experts(xs, block_e, n_used, w_gate[l].astype(BF16), w_up[l].astype(BF16), w_down[l].astype(BF16))
    return _combine(ys, idx_t, rank_t, pad_start, gate_t.T, shared, h2, lg, lb)


def _pad_cols(w, n):
    return jnp.pad(w, ((0, 0), (0, n - w.shape[1])))


def _pad_rows(w, n):
    return jnp.pad(w, ((0, n - w.shape[0]), (0, 0)))


def _trunk(x, meta_tokens, ln_mix_g, ln_mix_b, ln_ffn_g, ln_ffn_b,
           rw_mu, rw_w_rkv, rw_w0, rw_w_l1, rw_w_l2, rw_a0, rw_a_l1, rw_a_l2,
           rw_g_l1, rw_g_l2, rw_k_k, rw_k_a, rw_r_k, rw_lnx_g, rw_lnx_b, rw_w_out,
           rw_v0, rw_v_l1, rw_v_l2, kv_w,
           da_w_q, da_lam_q1, da_lam_k1, da_lam_q2, da_lam_k2, da_subln_g, da_w_out,
           moe_router_w, moe_router_b, moe_w_gate, moe_w_up, moe_w_down,
           moe_sh_gate, moe_sh_up, moe_sh_down):
    B, S, D = x.shape
    L = S + N_META
    Lp = -(-L // SEQ_ALIGN) * SEQ_ALIGN
    Np = B * Lp
    assert D == D_MODEL and Np % ROW_TM == 0 and Lp % PROJ_TM == 0 and Lp % SCAN_TC == 0 and Lp % ATT_T == 0
    meta = jnp.broadcast_to(meta_tokens[None].astype(x.dtype), (B, N_META, D))
    h = jnp.concatenate([meta, x, jnp.zeros((B, Lp - L, D), x.dtype)], axis=1)

    pc = _perm_cols()
    gs_np, gb_np = _head_sum_mats()
    gs, gb = jnp.asarray(gs_np), jnp.asarray(gb_np)
    tri = jnp.asarray(np.triu(np.ones((ROUTE_TM, ROUTE_TM), np.float32), 1)).astype(BF16)
    row = lambda vec: vec.reshape(1, -1)
    lane = np.arange(2 * D_MODEL)
    m_key, m_head = lane // 32, lane % 16
    c128 = np.arange(LANES) % 16
    vrow = np.arange(RWKV_HEAD)

    v_first = None
    kv = None
    for l in range(DEPTH):
        if l < N_A_LAYERS:
            p = {
                "mu": _pad_rows(rw_mu[l], SUBLANES),
                "wr": rw_w_rkv[l, 0][:, pc].astype(BF16),
                "wk": rw_w_rkv[l, 1][:, pc].astype(BF16),
                "wv": rw_w_rkv[l, 2][:, pc].astype(BF16),
                "w0": row(rw_w0[l][pc]),
                "wl1": _pad_cols(rw_w_l1[l], LANES).astype(BF16),
                "wl2": _pad_rows(rw_w_l2[l][:, pc], LANES).astype(BF16),
                "a0": row(rw_a0[l][pc]),
                "al1": _pad_cols(rw_a_l1[l], LANES).astype(BF16),
                "al2": _pad_rows(rw_a_l2[l][:, pc], LANES).astype(BF16),
                "gl1": _pad_cols(rw_g_l1[l], 2 * LANES).astype(BF16),
                "gl2": _pad_rows(rw_g_l2[l][:, pc], 2 * LANES).astype(BF16),
                "kk": row(rw_k_k[l][pc]),
                "ka": row(rw_k_a[l][pc]),
                "gs": gs,
                "gb": gb,
            }
            if l > 0:
                p["v0"] = row(rw_v0[l - 1][pc])
                p["vl1"] = _pad_cols(rw_v_l1[l - 1], LANES).astype(BF16)
                p["vl2"] = _pad_rows(rw_v_l2[l - 1][:, pc], LANES).astype(BF16)
            r, w, k, v, a, b, g = _rwkv_proj(h, v_first if l > 0 else None, p)
            if l == 0:
                v_first = v
            rk_m = rw_r_k[l][m_head, m_key].reshape(1, 2 * D_MODEL)
            lg_t = rw_lnx_g[l].reshape(RWKV_HEADS, RWKV_HEAD)[c128[None, :], vrow[:, None]]
            lb_t = rw_lnx_b[l].reshape(RWKV_HEADS, RWKV_HEAD)[c128[None, :], vrow[:, None]]
            z = _wkv_scan(r, w, k, v, a, b, rk_m, lg_t, lb_t)
            h2 = _mm_res_ln(z.reshape(Np, D), g.reshape(Np, D), rw_w_out[l][pc, :].astype(BF16),
                            h.reshape(Np, D), row(ln_mix_g[l]), row(ln_mix_b[l]))
        else:
            j = l - N_A_LAYERS
            h2 = h.reshape(Np, D)
            if kv is None:
                kv = _proj(h2, kv_w.astype(BF16)).reshape(B, Lp, 2 * D)
            q = _proj(h2, (da_w_q[j] * (DIFF_HD ** -0.5)).astype(BF16)).reshape(B, Lp, D)
            lam_init = 0.8 - 0.6 * math.exp(-0.3 * l)
            lam4 = jnp.stack([da_lam_q1[j], da_lam_k1[j], da_lam_q2[j], da_lam_k2[j]])
            o = _diff_attn(q, kv, _pad_rows(lam4, SUBLANES), row(da_subln_g[j]), lam_init)
            h2 = _mm_res_ln(o.reshape(Np, D), None, da_w_out[j].astype(BF16), h2,
                            row(ln_mix_g[l]), row(ln_mix_b[l]))
        h2 = _moe_layer(h2, l, row(ln_ffn_g[l]), row(ln_ffn_b[l]), moe_router_w, moe_router_b,
                        moe_w_gate, moe_w_up, moe_w_down, moe_sh_gate, moe_sh_up, moe_sh_down, tri)
        h = h2.reshape(B, Lp, D)
    return h[:, N_META:L]


_trunk_jit = jax.jit(_trunk)


def kernel(x, meta_tokens, ln_mix_g, ln_mix_b, ln_ffn_g, ln_ffn_b, rw_mu, rw_w_rkv, rw_w0, rw_w_l1, rw_w_l2, rw_a0, rw_a_l1, rw_a_l2, rw_g_l1, rw_g_l2, rw_k_k, rw_k_a, rw_r_k, rw_lnx_g, rw_lnx_b, rw_w_out, rw_v0, rw_v_l1, rw_v_l2, kv_w, da_w_q, da_lam_q1, da_lam_k1, da_lam_q2, da_lam_k2, da_subln_g, da_w_out, moe_router_w, moe_router_b, moe_w_gate, moe_w_up, moe_w_down, moe_sh_gate, moe_sh_up, moe_sh_down):
    return _trunk_jit(x, meta_tokens, ln_mix_g, ln_mix_b, ln_ffn_g, ln_ffn_b, rw_mu, rw_w_rkv, rw_w0, rw_w_l1,
                      rw_w_l2, rw_a0, rw_a_l1, rw_a_l2, rw_g_l1, rw_g_l2, rw_k_k, rw_k_a, rw_r_k, rw_lnx_g,
                      rw_lnx_b, rw_w_out, rw_v0, rw_v_l1, rw_v_l2, kv_w, da_w_q, da_lam_q1, da_lam_k1,
                      da_lam_q2, da_lam_k2, da_subln_g, da_w_out, moe_router_w, moe_router_b, moe_w_gate,
                      moe_w_up, moe_w_down, moe_sh_gate, moe_sh_up, moe_sh_down)
```

```python
import functools
import math

import numpy as np
import jax
import jax.numpy as jnp
from jax import lax
from jax.experimental import pallas as pl
from jax.experimental.pallas import tpu as pltpu

F32 = jnp.float32
BF16 = jnp.bfloat16
I32 = jnp.int32

D_MODEL = 1024
DEPTH = 4
N_META = 16
N_A_LAYERS = DEPTH // 2
RWKV_HEAD = 64
RWKV_HEADS = D_MODEL // RWKV_HEAD
RWKV_GN_EPS = 64e-5
DIFF_HEADS = 8
DIFF_HD = D_MODEL // (2 * DIFF_HEADS)
N_EXPERTS = 64
TOP_K = 8
N_GROUPS = 8
TOPK_GROUPS = 4
EXPERT_FF = D_MODEL // 4
ROUTED_SCALE = 2.5
DN_ALPHA = (2 * DEPTH) ** 0.25
LN_EPS = 1e-5

LANES = 128
SUBLANES = 8
VMEM_LIMIT_BYTES = 56 * 1024 * 1024

SEQ_ALIGN = 256
PROJ_TM = 256
ROW_TM = 512
SCAN_TC = 128
SCAN_SUB = SUBLANES
ATT_T = 256
ROUTE_TM = 256
MOE_BLK = 256
DISP_TM = 512
COMB_TM = 128

_SIGMA = (0, 4, 1, 5, 2, 6, 3, 7)


def _cparams(sem):
    return pltpu.CompilerParams(dimension_semantics=sem, vmem_limit_bytes=VMEM_LIMIT_BYTES)


def _dot(a, b):
    return jnp.dot(a, b, preferred_element_type=F32)


def _dot_hi(a, b):
    return jnp.dot(a, b, preferred_element_type=F32, precision=lax.Precision.HIGHEST)


def _full(shape):
    nd = len(shape)
    return pl.BlockSpec(shape, lambda *_: (0,) * nd)


def _layer_norm(y, g, b):
    mu = jnp.mean(y, axis=-1, keepdims=True)
    d = y - mu
    var = jnp.mean(d * d, axis=-1, keepdims=True)
    return d * lax.rsqrt(var + LN_EPS) * g + b


def _perm_cols():
    L = np.arange(D_MODEL)
    j, p, hh = L // LANES, (L % LANES) // 16, L % 16
    i = 8 * j + np.asarray(_SIGMA)[p]
    return (hh * RWKV_HEAD + i).astype(np.int32)


def _head_sum_mats():
    L = np.arange(D_MODEL)
    c = np.arange(LANES)
    gs = (L[:, None] % 16 == c[None, :] % 16).astype(np.float32)
    gb = ((c[:, None] < 16) & (c[:, None] == L[None, :] % 16)).astype(np.float32)
    return gs, gb


def _rwkv_proj_kernel(has_vres, *refs):
    if has_vres:
        (x_ref, xp_ref, mu_ref, wr_ref, wk_ref, wv_ref, w0_ref, wl1_ref, wl2_ref, a0_ref, al1_ref,
         al2_ref, gl1_ref, gl2_ref, kk_ref, ka_ref, gs_ref, gb_ref, vf_ref, v0_ref, vl1_ref, vl2_ref,
         r_o, w_o, k_o, v_o, a_o, b_o, g_o) = refs
    else:
        (x_ref, xp_ref, mu_ref, wr_ref, wk_ref, wv_ref, w0_ref, wl1_ref, wl2_ref, a0_ref, al1_ref,
         al2_ref, gl1_ref, gl2_ref, kk_ref, ka_ref, gs_ref, gb_ref,
         r_o, w_o, k_o, v_o, a_o, b_o, g_o) = refs
    i = pl.program_id(1)
    x = x_ref[0]
    prev = xp_ref[0][SUBLANES - 1:SUBLANES, :]
    prev = jnp.where(i == 0, 0.0, prev)
    row = lax.broadcasted_iota(I32, x.shape, 0)
    xprev = jnp.where(row == 0, prev, pltpu.roll(x, 1, 0))
    dx = xprev - x

    def mix(s):
        return (x + dx * mu_ref[s:s + 1, :]).astype(BF16)

    xv = mix(2)
    r = _dot(mix(0), wr_ref[...])
    k = _dot(mix(1), wk_ref[...])
    v = _dot(xv, wv_ref[...])
    zw = w0_ref[...] + _dot(jnp.tanh(_dot(mix(3), wl1_ref[...])).astype(BF16), wl2_ref[...])
    decay = jnp.exp(-math.exp(-0.5) * jax.nn.sigmoid(zw))
    a = jax.nn.sigmoid(a0_ref[...] + _dot(_dot(mix(4), al1_ref[...]).astype(BF16), al2_ref[...]))
    g = _dot(jax.nn.sigmoid(_dot(mix(5), gl1_ref[...])).astype(BF16), gl2_ref[...])
    if has_vres:
        gate_v = jax.nn.sigmoid(v0_ref[...] + _dot(_dot(xv, vl1_ref[...]).astype(BF16), vl2_ref[...]))
        v = v + (vf_ref[0] - v) * gate_v
    kk = k * kk_ref[...]
    ss = _dot_hi(kk * kk, gs_ref[...])
    inv = 1.0 / jnp.maximum(jnp.sqrt(ss), 1e-12)
    kk = kk * _dot_hi(inv, gb_ref[...])
    k = k * (1.0 + (a - 1.0) * ka_ref[...])
    r_o[0] = r
    w_o[0] = decay
    k_o[0] = k
    v_o[0] = v
    a_o[0] = -kk
    b_o[0] = kk * a
    g_o[0] = g


def _rwkv_proj(h, vfirst, p):
    B, Lp, D = h.shape
    TM = PROJ_TM
    has_vres = vfirst is not None
    tile = pl.BlockSpec((1, TM, D), lambda b, i: (b, i, 0))
    prev8 = pl.BlockSpec((1, SUBLANES, D), lambda b, i: (b, jnp.maximum(i * (TM // SUBLANES) - 1, 0), 0))
    names = ["mu", "wr", "wk", "wv", "w0", "wl1", "wl2", "a0", "al1", "al2", "gl1", "gl2", "kk", "ka", "gs", "gb"]
    args = [h, h] + [p[n] for n in names]
    specs = [tile, prev8] + [_full(p[n].shape) for n in names]
    if has_vres:
        args += [vfirst, p["v0"], p["vl1"], p["vl2"]]
        specs += [tile, _full(p["v0"].shape), _full(p["vl1"].shape), _full(p["vl2"].shape)]
    out = jax.ShapeDtypeStruct((B, Lp, D), F32)
    return pl.pallas_call(
        functools.partial(_rwkv_proj_kernel, has_vres),
        out_shape=[out] * 7,
        grid=(B, Lp // TM),
        in_specs=specs,
        out_specs=[tile] * 7,
        compiler_params=_cparams(("parallel", "arbitrary")),
        name="rwkv_proj",
    )(*args)


def _wkv_kernel(r_ref, w_ref, k_ref, v_ref, a_ref, b_ref, rk_ref, lg_ref, lb_ref, o_ref,
                s_ref, ma_ref, mwr_ref, mw_ref, mb_ref, mk_ref, zv_ref, br_ref, kr_ref, rkk_ref, yc_ref):
    NV = RWKV_HEAD
    NCOL = 2 * D_MODEL // LANES
    NRB = NV // SUBLANES

    @pl.when(pl.program_id(0) == 0)
    def _():
        s_ref[...] = jnp.zeros_like(s_ref)

    lane = lax.broadcasted_iota(I32, (SUBLANES, LANES), 1)
    even = ((lane // 16) % 2) == 0
    grp = lane // 32

    def cs(j):
        return slice(j * LANES, (j + 1) * LANES)

    def merge(ref, t0):
        x0 = ref[0, pl.ds(t0, SUBLANES), :]
        x1 = ref[1, pl.ds(t0, SUBLANES), :]
        cols = []
        for j in range(D_MODEL // LANES):
            a0 = x0[:, cs(j)]
            a1 = x1[:, cs(j)]
            cols.append(jnp.where(even, a0, pltpu.roll(a1, 16, 1)))
            cols.append(jnp.where(even, pltpu.roll(a0, LANES - 16, 1), a1))
        return cols

    def fold(x):
        return (x + pltpu.roll(x, 32, 1)) + (pltpu.roll(x, 64, 1) + pltpu.roll(x, 96, 1))

    def colsum(xs):
        acc = xs[0]
        for x in xs[1:]:
            acc = acc + x
        return acc

    def sub_chunk(c, carry):
        t0 = pl.multiple_of(c * SCAN_SUB, SCAN_SUB)
        R = merge(r_ref, t0)
        W = merge(w_ref, t0)
        K = merge(k_ref, t0)
        V = merge(v_ref, t0)
        A = merge(a_ref, t0)
        Bm = merge(b_ref, t0)
        per_step = pl.ds(0, SCAN_SUB, stride=SUBLANES)
        for j in range(NCOL):
            ma_ref[j, per_step, :] = A[j]
            mw_ref[j, per_step, :] = W[j]
            mwr_ref[j, per_step, :] = W[j] * R[j]
            mb_ref[j, per_step, :] = Bm[j]
            mk_ref[j, per_step, :] = K[j]
        br_ref[per_step, :] = fold(colsum([Bm[j] * R[j] for j in range(NCOL)]))
        kr_ref[per_step, :] = fold(colsum([K[j] * R[j] for j in range(NCOL)]))
        rkk_ref[per_step, :] = fold(colsum([K[j] * R[j] * rk_ref[:, cs(j)] for j in range(NCOL)]))
        for j in range(NCOL):
            for q in range(4):
                vi = 4 * j + q
                z = fold(jnp.where(grp == q, V[j], 0.0))
                zv_ref[pl.ds(vi, SUBLANES, stride=NV), :] = z

        def step(t, carry2):
            trow = pl.ds(pl.multiple_of(t * SUBLANES, SUBLANES), 1)
            acc_a = [None] * NRB
            acc_y = [None] * NRB
            for j in range(NCOL):
                a_row = ma_ref[j, trow, :]
                wr_row = mwr_ref[j, trow, :]
                for i in range(NRB):
                    s = s_ref[i * SUBLANES:(i + 1) * SUBLANES, cs(j)]
                    pa = s * a_row
                    py = s * wr_row
                    acc_a[i] = pa if j == 0 else acc_a[i] + pa
                    acc_y[i] = py if j == 0 else acc_y[i] + py
            sa = [fold(x) for x in acc_a]
            yp = [fold(x) for x in acc_y]
            base = pl.multiple_of(t * NV, NV)
            vcol = [zv_ref[pl.ds(base + i * SUBLANES, SUBLANES), :] for i in range(NRB)]
            for j in range(NCOL):
                w_row = mw_ref[j, trow, :]
                b_row = mb_ref[j, trow, :]
                k_row = mk_ref[j, trow, :]
                for i in range(NRB):
                    blk = (slice(i * SUBLANES, (i + 1) * SUBLANES), cs(j))
                    s_ref[blk] = s_ref[blk] * w_row + sa[i] * b_row + vcol[i] * k_row
            brt = br_ref[trow, :]
            krt = kr_ref[trow, :]
            rkt = rkk_ref[trow, :]
            y = [yp[i] + sa[i] * brt + vcol[i] * krt for i in range(NRB)]
            mean = jnp.sum(colsum(y), axis=0, keepdims=True) * (1.0 / NV)
            d = [yi - mean for yi in y]
            var = jnp.sum(colsum([di * di for di in d]), axis=0, keepdims=True) * (1.0 / NV)
            inv = lax.rsqrt(var + RWKV_GN_EPS)
            for i in range(NRB):
                rows = slice(i * SUBLANES, (i + 1) * SUBLANES)
                out = d[i] * inv * lg_ref[rows, :] + lb_ref[rows, :] + vcol[i] * rkt
                yc_ref[pl.ds(base + i * SUBLANES, SUBLANES), :] = out
            return carry2

        lax.fori_loop(0, SCAN_SUB, step, 0)

        cols = []
        for j in range(NCOL):
            acc = None
            for q in range(4):
                z = yc_ref[pl.ds(4 * j + q, SUBLANES, stride=NV), :]
                acc = z if q == 0 else jnp.where(grp == q, z, acc)
            cols.append(acc)
        for j in range(D_MODEL // LANES):
            c0, c1 = cols[2 * j], cols[2 * j + 1]
            o_ref[0, pl.ds(t0, SUBLANES), cs(j)] = jnp.where(even, c0, pltpu.roll(c1, 16, 1))
            o_ref[1, pl.ds(t0, SUBLANES), cs(j)] = jnp.where(even, pltpu.roll(c0, LANES - 16, 1), c1)
        return carry

    lax.fori_loop(0, SCAN_TC // SCAN_SUB, sub_chunk, 0)


def _wkv_scan(r, w, k, v, a, b, rk_m, lg_t, lb_t):
    B, Lp, D = r.shape
    assert B == 2, "the scan packs exactly two batch rows into the lane dimension"
    blk = pl.BlockSpec((B, SCAN_TC, D), lambda i: (0, i, 0))
    vm = lambda *s: pltpu.VMEM(s, F32)
    return pl.pallas_call(
        _wkv_kernel,
        out_shape=jax.ShapeDtypeStruct((B, Lp, D), F32),
        grid=(Lp // SCAN_TC,),
        in_specs=[blk] * 6 + [_full(rk_m.shape), _full(lg_t.shape), _full(lb_t.shape)],
        out_specs=blk,
        scratch_shapes=[vm(RWKV_HEAD, 2 * D)] + [vm(2 * D // LANES, SCAN_SUB * SUBLANES, LANES)] * 5
        + [vm(SCAN_SUB * RWKV_HEAD, LANES)] + [vm(SCAN_SUB * SUBLANES, LANES)] * 3
        + [vm(SCAN_SUB * RWKV_HEAD, LANES)],
        compiler_params=_cparams(("arbitrary",)),
        name="wkv_scan",
    )(r, w, k, v, a, b, rk_m, lg_t, lb_t)


def _mm_res_ln_kernel(has_gate, *refs):
    if has_gate:
        z_ref, g_ref, w_ref, h_ref, lg_ref, lb_ref, o_ref = refs
        z = (z_ref[...] * g_ref[...]).astype(BF16)
    else:
        z_ref, w_ref, h_ref, lg_ref, lb_ref, o_ref = refs
        z = z_ref[...].astype(BF16)
    y = DN_ALPHA * h_ref[...] + _dot(z, w_ref[...])
    o_ref[...] = _layer_norm(y, lg_ref[...], lb_ref[...])


def _mm_res_ln(z, gate, w, h, lg, lb):
    Np, D = h.shape
    TM = ROW_TM
    tile = pl.BlockSpec((TM, D), lambda i: (i, 0))
    has_gate = gate is not None
    args = [z] + ([gate] if has_gate else []) + [w, h, lg, lb]
    specs = [tile] + ([tile] if has_gate else []) + [_full(w.shape), tile, _full(lg.shape), _full(lb.shape)]
    return pl.pallas_call(
        functools.partial(_mm_res_ln_kernel, has_gate),
        out_shape=jax.ShapeDtypeStruct((Np, D), F32),
        grid=(Np // TM,),
        in_specs=specs,
        out_specs=tile,
        compiler_params=_cparams(("parallel",)),
        name="mm_res_ln",
    )(*args)


def _proj_kernel(x_ref, w_ref, o_ref):
    o_ref[...] = _dot(x_ref[...].astype(BF16), w_ref[...]).astype(o_ref.dtype)


def _proj(x, w):
    Np, D = x.shape
    Nout = w.shape[1]
    TM = ROW_TM
    return pl.pallas_call(
        _proj_kernel,
        out_shape=jax.ShapeDtypeStruct((Np, Nout), BF16),
        grid=(Np // TM,),
        in_specs=[pl.BlockSpec((TM, D), lambda i: (i, 0)), _full(w.shape)],
        out_specs=pl.BlockSpec((TM, Nout), lambda i: (i, 0)),
        compiler_params=_cparams(("parallel",)),
        name="proj",
    )(x, w)


def _attn_kernel(lam_init, lam_ref, sg_ref, q_ref, k_ref, v_ref, o_ref,
                 m1_ref, l1_ref, acc1_ref, m2_ref, l2_ref, acc2_ref):
    T = ATT_T
    qi = pl.program_id(2)
    q = q_ref[0]
    lane = lax.broadcasted_iota(I32, q.shape, 1)
    zero = jnp.zeros_like(q)
    q1 = jnp.where(lane < DIFF_HD, q, zero)
    q2 = jnp.where(lane >= DIFF_HD, q, zero)
    m1_ref[...] = jnp.full_like(m1_ref, -1e30)
    m2_ref[...] = jnp.full_like(m2_ref, -1e30)
    l1_ref[...] = jnp.zeros_like(l1_ref)
    l2_ref[...] = jnp.zeros_like(l2_ref)
    acc1_ref[...] = jnp.zeros_like(acc1_ref)
    acc2_ref[...] = jnp.zeros_like(acc2_ref)
    nt = (((1,), (1,)), ((), ()))

    def update(s, vb, m_ref, l_ref, acc_ref):
        m_old = m_ref[...]
        m_new = jnp.maximum(m_old, jnp.max(s, axis=1, keepdims=True))
        alpha = jnp.exp(m_old - m_new)
        p = jnp.exp(s - m_new[:, :1])
        l_ref[...] = alpha * l_ref[...] + jnp.sum(p, axis=1, keepdims=True)
        acc_ref[...] = alpha * acc_ref[...] + _dot(p.astype(BF16), vb)
        m_ref[...] = m_new

    def chunk(kc, masked):
        k0 = pl.multiple_of(kc * T, T)
        kb = k_ref[0, pl.ds(k0, T), :]
        vb = v_ref[0, pl.ds(k0, T), :]
        s1 = lax.dot_general(q1, kb, nt, preferred_element_type=F32)
        s2 = lax.dot_general(q2, kb, nt, preferred_element_type=F32)
        if masked:
            rr = lax.broadcasted_iota(I32, s1.shape, 0)
            cc = lax.broadcasted_iota(I32, s1.shape, 1)
            keep = cc <= rr
            s1 = jnp.where(keep, s1, -jnp.inf)
            s2 = jnp.where(keep, s2, -jnp.inf)
        update(s1, vb, m1_ref, l1_ref, acc1_ref)
        update(s2, vb, m2_ref, l2_ref, acc2_ref)

    def body(kc, carry):
        chunk(kc, False)
        return carry

    lax.fori_loop(0, qi, body, 0)
    chunk(qi, True)

    lam_v = lam_ref[...]
    lam = (jnp.exp(jnp.sum(lam_v[0:1] * lam_v[1:2], axis=1, keepdims=True))
           - jnp.exp(jnp.sum(lam_v[2:3] * lam_v[3:4], axis=1, keepdims=True)) + lam_init)
    o = acc1_ref[...] / l1_ref[...] - lam * (acc2_ref[...] / l2_ref[...])
    o = o * lax.rsqrt(jnp.mean(o * o, axis=1, keepdims=True) + 1e-5) * sg_ref[...] * (1.0 - lam_init)
    o_ref[0] = o.astype(o_ref.dtype)


def _diff_attn(q, kv, lam4, subln_g, lam_init):
    B, Lp, D = q.shape
    T = ATT_T
    H = DIFF_HEADS
    HW = 2 * DIFF_HD
    vm = lambda *s: pltpu.VMEM(s, F32)
    return pl.pallas_call(
        functools.partial(_attn_kernel, lam_init),
        out_shape=jax.ShapeDtypeStruct((B, Lp, D), BF16),
        grid=(B, H, Lp // T),
        in_specs=[_full(lam4.shape), _full(subln_g.shape),
                  pl.BlockSpec((1, T, HW), lambda b, h, i: (b, i, h)),
                  pl.BlockSpec((1, Lp, HW), lambda b, h, i: (b, 0, h)),
                  pl.BlockSpec((1, Lp, HW), lambda b, h, i: (b, 0, H + h))],
        out_specs=pl.BlockSpec((1, T, HW), lambda b, h, i: (b, i, h)),
        scratch_shapes=[vm(T, LANES), vm(T, LANES), vm(T, HW), vm(T, LANES), vm(T, LANES), vm(T, HW)],
        compiler_params=_cparams(("parallel", "parallel", "arbitrary")),
        name="diff_attn",
    )(lam4, subln_g, q, kv, kv)


def _route_kernel(x_ref, rw_ref, rb_ref, sg_ref, su_ref, sd_ref, tri_ref,
                  sh_o, idx_o, gate_o, rank_o, cnt_o, carry_ref):
    TM = ROUTE_TM
    G, EG = N_GROUPS, N_EXPERTS // N_GROUPS

    @pl.when(pl.program_id(0) == 0)
    def _():
        carry_ref[...] = jnp.zeros_like(carry_ref)

    x = x_ref[...]
    xb = x.astype(BF16)
    hmid = _dot(xb, sg_ref[...])
    hmid = hmid * jax.nn.sigmoid(hmid) * _dot(xb, su_ref[...])
    sh_o[...] = _dot(hmid.astype(BF16), sd_ref[...])

    logit = lax.dot_general(rw_ref[...], x, (((1,), (1,)), ((), ())),
                            preferred_element_type=F32, precision=lax.Precision.HIGHEST)
    s = jax.nn.sigmoid(logit)
    s3 = s.reshape(G, EG, TM)
    sel3 = (s + rb_ref[...]).reshape(G, EG, TM)
    io_j = lax.broadcasted_iota(I32, (G, EG, TM), 1)
    io_g = lax.broadcasted_iota(I32, (G, 1, TM), 0)
    neg = -jnp.inf
    m1 = jnp.max(sel3, axis=1, keepdims=True)
    i1 = jnp.min(jnp.where(sel3 == m1, io_j, EG), axis=1, keepdims=True)
    m2 = jnp.max(jnp.where(io_j == i1, neg, sel3), axis=1, keepdims=True)
    gsc = m1 + m2
    gkeep = jnp.zeros((G, 1, TM), F32)
    for _ in range(TOPK_GROUPS):
        m = jnp.max(gsc, axis=0, keepdims=True)
        gi = jnp.min(jnp.where(gsc == m, io_g, G), axis=0, keepdims=True)
        hit = io_g == gi
        gkeep = jnp.where(hit, 1.0, gkeep)
        gsc = jnp.where(hit, neg, gsc)
    cur = jnp.where(gkeep > 0.0, sel3, neg)
    io_e = io_g * EG + io_j
    hits, idxs, ws = [], [], []
    for _ in range(TOP_K):
        m = jnp.max(jnp.max(cur, axis=1, keepdims=True), axis=0, keepdims=True)
        ei = jnp.min(jnp.min(jnp.where(cur == m, io_e, N_EXPERTS), axis=1, keepdims=True), axis=0, keepdims=True)
        hit = io_e == ei
        ws.append(jnp.sum(jnp.sum(jnp.where(hit, s3, 0.0), axis=1, keepdims=True), axis=0, keepdims=True))
        cur = jnp.where(hit, neg, cur)
        hits.append(hit)
        idxs.append(ei)
    wsum = ws[0]
    for wv in ws[1:]:
        wsum = wsum + wv
    scale = ROUTED_SCALE / wsum
    onehot = jnp.zeros((G, EG, TM), F32)
    for hit in hits:
        onehot = jnp.where(hit, 1.0, onehot)
    oh2 = onehot.reshape(N_EXPERTS, TM)
    rank_full = (_dot(oh2.astype(BF16), tri_ref[...]) + carry_ref[:, :1]).reshape(G, EG, TM)
    for kk in range(TOP_K):
        rk = jnp.sum(jnp.sum(jnp.where(hits[kk], rank_full, 0.0), axis=1, keepdims=True), axis=0, keepdims=True)
        idx_o[kk:kk + 1, :] = idxs[kk].reshape(1, TM)
        gate_o[kk:kk + 1, :] = (ws[kk] * scale).reshape(1, TM)
        rank_o[kk:kk + 1, :] = rk.reshape(1, TM).astype(I32)
    carry_ref[...] = carry_ref[...] + jnp.sum(oh2, axis=1, keepdims=True)
    cnt_o[...] = carry_ref[...]


def _route_shared(x, rw_t, rb, sg, su, sd, tri):
    Np, D = x.shape
    TM = ROUTE_TM
    tok = pl.BlockSpec((TOP_K, TM), lambda i: (0, i))
    return pl.pallas_call(
        _route_kernel,
        out_shape=[jax.ShapeDtypeStruct((Np, D), F32),
                   jax.ShapeDtypeStruct((TOP_K, Np), I32),
                   jax.ShapeDtypeStruct((TOP_K, Np), F32),
                   jax.ShapeDtypeStruct((TOP_K, Np), I32),
                   jax.ShapeDtypeStruct((N_EXPERTS, LANES), F32)],
        grid=(Np // TM,),
        in_specs=[pl.BlockSpec((TM, D), lambda i: (i, 0)), _full(rw_t.shape), _full(rb.shape),
                  _full(sg.shape), _full(su.shape), _full(sd.shape), _full(tri.shape)],
        out_specs=[pl.BlockSpec((TM, D), lambda i: (i, 0)), tok, tok, tok, _full((N_EXPERTS, LANES))],
        scratch_shapes=[pltpu.VMEM((N_EXPERTS, LANES), F32)],
        compiler_params=_cparams(("arbitrary",)),
        name="moe_route",
    )(x, rw_t, rb, sg, su, sd, tri)


def _row_copy(src_ref, src_row, dst_ref, dst_row, sem):
    return pltpu.make_async_copy(src_ref.at[pl.ds(src_row, 1)], dst_ref.at[pl.ds(dst_row, 1)], sem)


def _dispatch_kernel(ps_ref, idx_ref, rank_ref, x_ref, xs_hbm, sem):
    def issue(n, c):
        for kk in range(TOP_K):
            dst = ps_ref[idx_ref[kk, n]] + rank_ref[kk, n]
            _row_copy(x_ref, n, xs_hbm, dst, sem).start()
        return c

    lax.fori_loop(0, DISP_TM, issue, 0)

    def drain(n, c):
        for kk in range(TOP_K):
            _row_copy(x_ref, n, xs_hbm, 0, sem).wait()
        return c

    lax.fori_loop(0, DISP_TM, drain, 0)


def _dispatch(x, idx_t, rank_t, pad_start, n_rows):
    Np, D = x.shape
    tok = pl.BlockSpec((TOP_K, DISP_TM), lambda i, ps: (0, i), memory_space=pltpu.SMEM)
    return pl.pallas_call(
        _dispatch_kernel,
        out_shape=jax.ShapeDtypeStruct((n_rows, D), F32),
        grid_spec=pltpu.PrefetchScalarGridSpec(
            num_scalar_prefetch=1,
            grid=(Np // DISP_TM,),
            in_specs=[tok, tok, pl.BlockSpec((DISP_TM, D), lambda i, ps: (i, 0))],
            out_specs=pl.BlockSpec(memory_space=pl.ANY),
            scratch_shapes=[pltpu.SemaphoreType.DMA(())],
        ),
        compiler_params=_cparams(("arbitrary",)),
        name="moe_dispatch",
    )(pad_start, idx_t, rank_t, x)


def _expert_kernel(be_ref, nu_ref, xs_ref, wg_ref, wu_ref, wd_ref, ys_ref):
    @pl.when(pl.program_id(0) < nu_ref[0])
    def _():
        x = xs_ref[...].astype(BF16)
        g = _dot(x, wg_ref[0])
        hmid = g * jax.nn.sigmoid(g) * _dot(x, wu_ref[0])
        ys_ref[...] = _dot(hmid.astype(BF16), wd_ref[0])


def _experts(xs, block_e, n_used, wg, wu, wd):
    P, D = xs.shape
    F = wg.shape[2]
    nb = P // MOE_BLK
    rows = pl.BlockSpec((MOE_BLK, D), lambda i, be, nu: (jnp.minimum(i, nu[0] - 1), 0))
    wspec = lambda s: pl.BlockSpec((1,) + s, lambda i, be, nu: (be[jnp.minimum(i, nu[0] - 1)], 0, 0))
    return pl.pallas_call(
        _expert_kernel,
        out_shape=jax.ShapeDtypeStruct((P, D), F32),
        grid_spec=pltpu.PrefetchScalarGridSpec(
            num_scalar_prefetch=2,
            grid=(nb,),
            in_specs=[rows, wspec((D, F)), wspec((D, F)), wspec((F, D))],
            out_specs=rows,
        ),
        compiler_params=_cparams(("arbitrary",)),
        name="moe_experts",
    )(block_e, n_used, xs, wg, wu, wd)


def _combine_kernel(ps_ref, idx_ref, rank_ref, gate_ref, sh_ref, h_ref, lg_ref, lb_ref, ys_hbm, o_ref,
                    buf, sem):
    def issue(n, c):
        for kk in range(TOP_K):
            src = ps_ref[idx_ref[kk, n]] + rank_ref[kk, n]
            _row_copy(ys_hbm, src, buf.at[kk], n, sem).start()
        return c

    lax.fori_loop(0, COMB_TM, issue, 0)

    def drain(n, c):
        for kk in range(TOP_K):
            _row_copy(ys_hbm, 0, buf.at[kk], n, sem).wait()
        return c

    lax.fori_loop(0, COMB_TM, drain, 0)
    gate = gate_ref[...]
    ffn = sh_ref[...]
    for kk in range(TOP_K):
        ffn = ffn + gate[:, kk:kk + 1] * buf[kk]
    o_ref[...] = _layer_norm(DN_ALPHA * h_ref[...] + ffn, lg_ref[...], lb_ref[...])


def _combine(ys, idx_t, rank_t, pad_start, gate, shared, h, lg, lb):
    Np, D = h.shape
    TM = COMB_TM
    tok = pl.BlockSpec((TOP_K, TM), lambda i, ps: (0, i), memory_space=pltpu.SMEM)
    tile = pl.BlockSpec((TM, D), lambda i, ps: (i, 0))
    vec = pl.BlockSpec((1, D), lambda i, ps: (0, 0))
    return pl.pallas_call(
        _combine_kernel,
        out_shape=jax.ShapeDtypeStruct((Np, D), F32),
        grid_spec=pltpu.PrefetchScalarGridSpec(
            num_scalar_prefetch=1,
            grid=(Np // TM,),
            in_specs=[tok, tok, pl.BlockSpec((TM, TOP_K), lambda i, ps: (i, 0)), tile, tile, vec, vec,
                      pl.BlockSpec(memory_space=pl.ANY)],
            out_specs=tile,
            scratch_shapes=[pltpu.VMEM((TOP_K, TM, D), F32), pltpu.SemaphoreType.DMA(())],
        ),
        compiler_params=_cparams(("arbitrary",)),
        name="moe_combine",
    )(pad_start, idx_t, rank_t, gate, shared, h, lg, lb, ys)


def _moe_layer(h2, l, lg, lb, router_w, router_b, w_gate, w_up, w_down, sh_gate, sh_up, sh_down, tri):
    Np, D = h2.shape
    shared, idx_t, gate_t, rank_t, cnt = _route_shared(
        h2, router_w[l].T, router_b[l].reshape(N_EXPERTS, 1),
        sh_gate[l].astype(BF16), sh_up[l].astype(BF16), sh_down[l].astype(BF16), tri)
    counts = cnt[:, 0].astype(I32)
    padded = (counts + MOE_BLK - 1) // MOE_BLK * MOE_BLK
    pad_end = jnp.cumsum(padded)
    pad_start = (pad_end - padded).astype(I32)
    n_blocks = Np * TOP_K // MOE_BLK + N_EXPERTS
    n_used = (pad_end[-1:] // MOE_BLK).astype(I32)
    blk_row0 = jnp.arange(n_blocks, dtype=I32) * MOE_BLK
    block_e = jnp.minimum(jnp.sum((pad_end[None, :] <= blk_row0[:, None]).astype(I32), axis=1), N_EXPERTS - 1)
    xs = _dispatch(h2, idx_t, rank_t, pad_start, n_blocks * MOE_BLK)
    ys = _experts(xs, block_e, n_used, w_gate[l].astype(BF16), w_up[l].astype(BF16), w_down[l].astype(BF16))
    return _combine(ys, idx_t, rank_t, pad_start, gate_t.T, shared, h2, lg, lb)


def _pad_cols(w, n):
    return jnp.pad(w, ((0, 0), (0, n - w.shape[1])))


def _pad_rows(w, n):
    return jnp.pad(w, ((0, n - w.shape[0]), (0, 0)))


def _trunk(x, meta_tokens, ln_mix_g, ln_mix_b, ln_ffn_g, ln_ffn_b,
           rw_mu, rw_w_rkv, rw_w0, rw_w_l1, rw_w_l2, rw_a0, rw_a_l1, rw_a_l2,
           rw_g_l1, rw_g_l2, rw_k_k, rw_k_a, rw_r_k, rw_lnx_g, rw_lnx_b, rw_w_out,
           rw_v0, rw_v_l1, rw_v_l2, kv_w,
           da_w_q, da_lam_q1, da_lam_k1, da_lam_q2, da_lam_k2, da_subln_g, da_w_out,
           moe_router_w, moe_router_b, moe_w_gate, moe_w_up, moe_w_down,
           moe_sh_gate, moe_sh_up, moe_sh_down):
    B, S, D = x.shape
    L = S + N_META
    Lp = -(-L // SEQ_ALIGN) * SEQ_ALIGN
    Np = B * Lp
    assert D == D_MODEL and Np % ROW_TM == 0 and Lp % PROJ_TM == 0 and Lp % SCAN_TC == 0 and Lp % ATT_T == 0
    meta = jnp.broadcast_to(meta_tokens[None].astype(x.dtype), (B, N_META, D))
    h = jnp.concatenate([meta, x, jnp.zeros((B, Lp - L, D), x.dtype)], axis=1)

    pc = _perm_cols()
    gs_np, gb_np = _head_sum_mats()
    gs, gb = jnp.asarray(gs_np), jnp.asarray(gb_np)
    tri = jnp.asarray(np.triu(np.ones((ROUTE_TM, ROUTE_TM), np.float32), 1)).astype(BF16)
    row = lambda vec: vec.reshape(1, -1)
    lane = np.arange(2 * D_MODEL)
    m_key, m_head = lane // 32, lane % 16
    c128 = np.arange(LANES) % 16
    vrow = np.arange(RWKV_HEAD)

    v_first = None
    kv = None
    for l in range(DEPTH):
        if l < N_A_LAYERS:
            p = {
                "mu": _pad_rows(rw_mu[l], SUBLANES),
                "wr": rw_w_rkv[l, 0][:, pc].astype(BF16),
                "wk": rw_w_rkv[l, 1][:, pc].astype(BF16),
                "wv": rw_w_rkv[l, 2][:, pc].astype(BF16),
                "w0": row(rw_w0[l][pc]),
                "wl1": _pad_cols(rw_w_l1[l], LANES).astype(BF16),
                "wl2": _pad_rows(rw_w_l2[l][:, pc], LANES).astype(BF16),
                "a0": row(rw_a0[l][pc]),
                "al1": _pad_cols(rw_a_l1[l], LANES).astype(BF16),
                "al2": _pad_rows(rw_a_l2[l][:, pc], LANES).astype(BF16),
                "gl1": _pad_cols(rw_g_l1[l], 2 * LANES).astype(BF16),
                "gl2": _pad_rows(rw_g_l2[l][:, pc], 2 * LANES).astype(BF16),
                "kk": row(rw_k_k[l][pc]),
                "ka": row(rw_k_a[l][pc]),
                "gs": gs,
                "gb": gb,
            }
            if l > 0:
                p["v0"] = row(rw_v0[l - 1][pc])
                p["vl1"] = _pad_cols(rw_v_l1[l - 1], LANES).astype(BF16)
                p["vl2"] = _pad_rows(rw_v_l2[l - 1][:, pc], LANES).astype(BF16)
            r, w, k, v, a, b, g = _rwkv_proj(h, v_first if l > 0 else None, p)
            if l == 0:
                v_first = v
            rk_m = rw_r_k[l][m_head, m_key].reshape(1, 2 * D_MODEL)
            lg_t = rw_lnx_g[l].reshape(RWKV_HEADS, RWKV_HEAD)[c128[None, :], vrow[:, None]]
            lb_t = rw_lnx_b[l].reshape(RWKV_HEADS, RWKV_HEAD)[c128[None, :], vrow[:, None]]
            z = _wkv_scan(r, w, k, v, a, b, rk_m, lg_t, lb_t)
            h2 = _mm_res_ln(z.reshape(Np, D), g.reshape(Np, D), rw_w_out[l][pc, :].astype(BF16),
                            h.reshape(Np, D), row(ln_mix_g[l]), row(ln_mix_b[l]))
        else:
            j = l - N_A_LAYERS
            h2 = h.reshape(Np, D)
            if kv is None:
                kv = _proj(h2, kv_w.astype(BF16)).reshape(B, Lp, 2 * D)
            q = _proj(h2, (da_w_q[j] * (DIFF_HD ** -0.5)).astype(BF16)).reshape(B, Lp, D)
            lam_init = 0.8 - 0.6 * math.exp(-0.3 * l)
            lam4 = jnp.stack([da_lam_q1[j], da_lam_k1[j], da_lam_q2[j], da_lam_k2[j]])
            o = _diff_attn(q, kv, _pad_rows(lam4, SUBLANES), row(da_subln_g[j]), lam_init)
            h2 = _mm_res_ln(o.reshape(Np, D), None, da_w_out[j].astype(BF16), h2,
                            row(ln_mix_g[l]), row(ln_mix_b[l]))
        h2 = _moe_layer(h2, l, row(ln_ffn_g[l]), row(ln_ffn_b[l]), moe_router_w, moe_router_b,
                        moe_w_gate, moe_w_up, moe_w_down, moe_sh_gate, moe_sh_up, moe_sh_down, tri)
        h = h2.reshape(B, Lp, D)
    return h[:, N_META:L]


_trunk_jit = jax.jit(_trunk)


def kernel(x, meta_tokens, ln_mix_g, ln_mix_b, ln_ffn_g, ln_ffn_b, rw_mu, rw_w_rkv, rw_w0, rw_w_l1, rw_w_l2, rw_a0, rw_a_l1, rw_a_l2, rw_g_l1, rw_g_l2, rw_k_k, rw_k_a, rw_r_k, rw_lnx_g, rw_lnx_b, rw_w_out, rw_v0, rw_v_l1, rw_v_l2, kv_w, da_w_q, da_lam_q1, da_lam_k1, da_lam_q2, da_lam_k2, da_subln_g, da_w_out, moe_router_w, moe_router_b, moe_w_gate, moe_w_up, moe_w_down, moe_sh_gate, moe_sh_up, moe_sh_down):
    return _trunk_jit(x, meta_tokens, ln_mix_g, ln_mix_b, ln_ffn_g, ln_ffn_b, rw_mu, rw_w_rkv, rw_w0, rw_w_l1,
                      rw_w_l2, rw_a0, rw_a_l1, rw_a_l2, rw_g_l1, rw_g_l2, rw_k_k, rw_k_a, rw_r_k, rw_lnx_g,
                      rw_lnx_b, rw_w_out, rw_v0, rw_v_l1, rw_v_l2, kv_w, da_w_q, da_lam_q1, da_lam_k1,
                      da_lam_q2, da_lam_k2, da_subln_g, da_w_out, moe_router_w, moe_router_b, moe_w_gate,
                      moe_w_up, moe_w_down, moe_sh_gate, moe_sh_up, moe_sh_down)
```

```python
import functools
import math

import numpy as np
import jax
import jax.numpy as jnp
from jax import lax
from jax.experimental import pallas as pl
from jax.experimental.pallas import tpu as pltpu

F32 = jnp.float32
BF16 = jnp.bfloat16
I32 = jnp.int32

D_MODEL = 1024
DEPTH = 4
N_META = 16
N_A_LAYERS = DEPTH // 2
RWKV_HEAD = 64
RWKV_HEADS = D_MODEL // RWKV_HEAD
RWKV_GN_EPS = 64e-5
DIFF_HEADS = 8
DIFF_HD = D_MODEL // (2 * DIFF_HEADS)
N_EXPERTS = 64
TOP_K = 8
N_GROUPS = 8
TOPK_GROUPS = 4
EXPERT_FF = D_MODEL // 4
ROUTED_SCALE = 2.5
DN_ALPHA = (2 * DEPTH) ** 0.25
LN_EPS = 1e-5

LANES = 128
SUBLANES = 8
VMEM_LIMIT_BYTES = 56 * 1024 * 1024

SEQ_ALIGN = 256
PROJ_TM = 256
ROW_TM = 512
SCAN_TC = 128
SCAN_SUB = SUBLANES
ATT_T = 256
ROUTE_TM = 256
MOE_BLK = 256
DISP_TM = 512
COMB_TM = 128

_SIGMA = (0, 4, 1, 5, 2, 6, 3, 7)


def _cparams(sem):
    return pltpu.CompilerParams(dimension_semantics=sem, vmem_limit_bytes=VMEM_LIMIT_BYTES)


def _dot(a, b):
    return jnp.dot(a, b, preferred_element_type=F32)


def _dot_hi(a, b):
    return jnp.dot(a, b, preferred_element_type=F32, precision=lax.Precision.HIGHEST)


def _full(shape):
    nd = len(shape)
    return pl.BlockSpec(shape, lambda *_: (0,) * nd)


def _layer_norm(y, g, b):
    mu = jnp.mean(y, axis=-1, keepdims=True)
    d = y - mu
    var = jnp.mean(d * d, axis=-1, keepdims=True)
    return d * lax.rsqrt(var + LN_EPS) * g + b


def _perm_cols():
    L = np.arange(D_MODEL)
    j, p, hh = L // LANES, (L % LANES) // 16, L % 16
    i = 8 * j + np.asarray(_SIGMA)[p]
    return (hh * RWKV_HEAD + i).astype(np.int32)


def _head_sum_mats():
    L = np.arange(D_MODEL)
    c = np.arange(LANES)
    gs = (L[:, None] % 16 == c[None, :] % 16).astype(np.float32)
    gb = ((c[:, None] < 16) & (c[:, None] == L[None, :] % 16)).astype(np.float32)
    return gs, gb


def _rwkv_proj_kernel(has_vres, *refs):
    if has_vres:
        (x_ref, xp_ref, mu_ref, wr_ref, wk_ref, wv_ref, w0_ref, wl1_ref, wl2_ref, a0_ref, al1_ref,
         al2_ref, gl1_ref, gl2_ref, kk_ref, ka_ref, gs_ref, gb_ref, vf_ref, v0_ref, vl1_ref, vl2_ref,
         r_o, w_o, k_o, v_o, a_o, b_o, g_o) = refs
    else:
        (x_ref, xp_ref, mu_ref, wr_ref, wk_ref, wv_ref, w0_ref, wl1_ref, wl2_ref, a0_ref, al1_ref,
         al2_ref, gl1_ref, gl2_ref, kk_ref, ka_ref, gs_ref, gb_ref,
         r_o, w_o, k_o, v_o, a_o, b_o, g_o) = refs
    i = pl.program_id(1)
    x = x_ref[0]
    prev = xp_ref[0][SUBLANES - 1:SUBLANES, :]
    prev = jnp.where(i == 0, 0.0, prev)
    row = lax.broadcasted_iota(I32, x.shape, 0)
    xprev = jnp.where(row == 0, prev, pltpu.roll(x, 1, 0))
    dx = xprev - x

    def mix(s):
        return (x + dx * mu_ref[s:s + 1, :]).astype(BF16)

    xv = mix(2)
    r = _dot(mix(0), wr_ref[...])
    k = _dot(mix(1), wk_ref[...])
    v = _dot(xv, wv_ref[...])
    zw = w0_ref[...] + _dot(jnp.tanh(_dot(mix(3), wl1_ref[...])).astype(BF16), wl2_ref[...])
    decay = jnp.exp(-math.exp(-0.5) * jax.nn.sigmoid(zw))
    a = jax.nn.sigmoid(a0_ref[...] + _dot(_dot(mix(4), al1_ref[...]).astype(BF16), al2_ref[...]))
    g = _dot(jax.nn.sigmoid(_dot(mix(5), gl1_ref[...])).astype(BF16), gl2_ref[...])
    if has_vres:
        gate_v = jax.nn.sigmoid(v0_ref[...] + _dot(_dot(xv, vl1_ref[...]).astype(BF16), vl2_ref[...]))
        v = v + (vf_ref[0] - v) * gate_v
    kk = k * kk_ref[...]
    ss = _dot_hi(kk * kk, gs_ref[...])
    inv = 1.0 / jnp.maximum(jnp.sqrt(ss), 1e-12)
    kk = kk * _dot_hi(inv, gb_ref[...])
    k = k * (1.0 + (a - 1.0) * ka_ref[...])
    r_o[0] = r
    w_o[0] = decay
    k_o[0] = k
    v_o[0] = v
    a_o[0] = -kk
    b_o[0] = kk * a
    g_o[0] = g


def _rwkv_proj(h, vfirst, p):
    B, Lp, D = h.shape
    TM = PROJ_TM
    has_vres = vfirst is not None
    tile = pl.BlockSpec((1, TM, D), lambda b, i: (b, i, 0))
    prev8 = pl.BlockSpec((1, SUBLANES, D), lambda b, i: (b, jnp.maximum(i * (TM // SUBLANES) - 1, 0), 0))
    names = ["mu", "wr", "wk", "wv", "w0", "wl1", "wl2", "a0", "al1", "al2", "gl1", "gl2", "kk", "ka", "gs", "gb"]
    args = [h, h] + [p[n] for n in names]
    specs = [tile, prev8] + [_full(p[n].shape) for n in names]
    if has_vres:
        args += [vfirst, p["v0"], p["vl1"], p["vl2"]]
        specs += [tile, _full(p["v0"].shape), _full(p["vl1"].shape), _full(p["vl2"].shape)]
    out = jax.ShapeDtypeStruct((B, Lp, D), F32)
    return pl.pallas_call(
        functools.partial(_rwkv_proj_kernel, has_vres),
        out_shape=[out] * 7,
        grid=(B, Lp // TM),
        in_specs=specs,
        out_specs=[tile] * 7,
        compiler_params=_cparams(("parallel", "arbitrary")),
        name="rwkv_proj",
    )(*args)


def _wkv_kernel(r_ref, w_ref, k_ref, v_ref, a_ref, b_ref, rk_ref, lg_ref, lb_ref, o_ref,
                s_ref, ma_ref, mwr_ref, mw_ref, mb_ref, mk_ref, zv_ref, br_ref, kr_ref, rkk_ref, yc_ref):
    NV = RWKV_HEAD
    NCOL = 2 * D_MODEL // LANES
    NRB = NV // SUBLANES

    @pl.when(pl.program_id(0) == 0)
    def _():
        s_ref[...] = jnp.zeros_like(s_ref)

    lane = lax.broadcasted_iota(I32, (SUBLANES, LANES), 1)
    even = ((lane // 16) % 2) == 0
    grp = lane // 32

    def cs(j):
        return slice(j * LANES, (j + 1) * LANES)

    def merge(ref, t0):
        x0 = ref[0, pl.ds(t0, SUBLANES), :]
        x1 = ref[1, pl.ds(t0, SUBLANES), :]
        cols = []
        for j in range(D_MODEL // LANES):
            a0 = x0[:, cs(j)]
            a1 = x1[:, cs(j)]
            cols.append(jnp.where(even, a0, pltpu.roll(a1, 16, 1)))
            cols.append(jnp.where(even, pltpu.roll(a0, LANES - 16, 1), a1))
        return cols

    def fold(x):
        return (x + pltpu.roll(x, 32, 1)) + (pltpu.roll(x, 64, 1) + pltpu.roll(x, 96, 1))

    def colsum(xs):
        acc = xs[0]
        for x in xs[1:]:
            acc = acc + x
        return acc

    def sub_chunk(c, carry):
        t0 = pl.multiple_of(c * SCAN_SUB, SCAN_SUB)
        R = merge(r_ref, t0)
        W = merge(w_ref, t0)
        K = merge(k_ref, t0)
        V = merge(v_ref, t0)
        A = merge(a_ref, t0)
        Bm = merge(b_ref, t0)
        per_step = pl.ds(0, SCAN_SUB, stride=SUBLANES)
        for j in range(NCOL):
            ma_ref[j, per_step, :] = A[j]
            mw_ref[j, per_step, :] = W[j]
            mwr_ref[j, per_step, :] = W[j] * R[j]
            mb_ref[j, per_step, :] = Bm[j]
            mk_ref[j, per_step, :] = K[j]
        br_ref[per_step, :] = fold(colsum([Bm[j] * R[j] for j in range(NCOL)]))
        kr_ref[per_step, :] = fold(colsum([K[j] * R[j] for j in range(NCOL)]))
        rkk_ref[per_step, :] = fold(colsum([K[j] * R[j] * rk_ref[:, cs(j)] for j in range(NCOL)]))
        for j in range(NCOL):
            for q in range(4):
                vi = 4 * j + q
                z = fold(jnp.where(grp == q, V[j], 0.0))
                zv_ref[pl.ds(vi, SUBLANES, stride=NV), :] = z

        def step(t, carry2):
            trow = pl.ds(pl.multiple_of(t * SUBLANES, SUBLANES), 1)
            acc_a = [None] * NRB
            acc_y = [None] * NRB
            for j in range(NCOL):
                a_row = ma_ref[j, trow, :]
                wr_row = mwr_ref[j, trow, :]
                for i in range(NRB):
                    s = s_ref[i * SUBLANES:(i + 1) * SUBLANES, cs(j)]
                    pa = s * a_row
                    py = s * wr_row
                    acc_a[i] = pa if j == 0 else acc_a[i] + pa
                    acc_y[i] = py if j == 0 else acc_y[i] + py
            sa = [fold(x) for x in acc_a]
            yp = [fold(x) for x in acc_y]
            base = pl.multiple_of(t * NV, NV)
            vcol = [zv_ref[pl.ds(base + i * SUBLANES, SUBLANES), :] for i in range(NRB)]
            for j in range(NCOL):
                w_row = mw_ref[j, trow, :]
                b_row = mb_ref[j, trow, :]
                k_row = mk_ref[j, trow, :]
                for i in range(NRB):
                    blk = (slice(i * SUBLANES, (i + 1) * SUBLANES), cs(j))
                    s_ref[blk] = s_ref[blk] * w_row + sa[i] * b_row + vcol[i] * k_row
            brt = br_ref[trow, :]
            krt = kr_ref[trow, :]
            rkt = rkk_ref[trow, :]
            y = [yp[i] + sa[i] * brt + vcol[i] * krt for i in range(NRB)]
            mean = jnp.sum(colsum(y), axis=0, keepdims=True) * (1.0 / NV)
            d = [yi - mean for yi in y]
            var = jnp.sum(colsum([di * di for di in d]), axis=0, keepdims=True) * (1.0 / NV)
            inv = lax.rsqrt(var + RWKV_GN_EPS)
            for i in range(NRB):
                rows = slice(i * SUBLANES, (i + 1) * SUBLANES)
                out = d[i] * inv * lg_ref[rows, :] + lb_ref[rows, :] + vcol[i] * rkt
                yc_ref[pl.ds(base + i * SUBLANES, SUBLANES), :] = out
            return carry2

        lax.fori_loop(0, SCAN_SUB, step, 0)

        cols = []
        for j in range(NCOL):
            acc = None
            for q in range(4):
                z = yc_ref[pl.ds(4 * j + q, SUBLANES, stride=NV), :]
                acc = z if q == 0 else jnp.where(grp == q, z, acc)
            cols.append(acc)
        for j in range(D_MODEL // LANES):
            c0, c1 = cols[2 * j], cols[2 * j + 1]
            o_ref[0, pl.ds(t0, SUBLANES), cs(j)] = jnp.where(even, c0, pltpu.roll(c1, 16, 1))
            o_ref[1, pl.ds(t0, SUBLANES), cs(j)] = jnp.where(even, pltpu.roll(c0, LANES - 16, 1), c1)
        return carry

    lax.fori_loop(0, SCAN_TC // SCAN_SUB, sub_chunk, 0)


def _wkv_scan(r, w, k, v, a, b, rk_m, lg_t, lb_t):
    B, Lp, D = r.shape
    assert B == 2, "the scan packs exactly two batch rows into the lane dimension"
    blk = pl.BlockSpec((B, SCAN_TC, D), lambda i: (0, i, 0))
    vm = lambda *s: pltpu.VMEM(s, F32)
    return pl.pallas_call(
        _wkv_kernel,
        out_shape=jax.ShapeDtypeStruct((B, Lp, D), F32),
        grid=(Lp // SCAN_TC,),
        in_specs=[blk] * 6 + [_full(rk_m.shape), _full(lg_t.shape), _full(lb_t.shape)],
        out_specs=blk,
        scratch_shapes=[vm(RWKV_HEAD, 2 * D)] + [vm(2 * D // LANES, SCAN_SUB * SUBLANES, LANES)] * 5
        + [vm(SCAN_SUB * RWKV_HEAD, LANES)] + [vm(SCAN_SUB * SUBLANES, LANES)] * 3
        + [vm(SCAN_SUB * RWKV_HEAD, LANES)],
        compiler_params=_cparams(("arbitrary",)),
        name="wkv_scan",
    )(r, w, k, v, a, b, rk_m, lg_t, lb_t)


def _mm_res_ln_kernel(has_gate, *refs):
    if has_gate:
        z_ref, g_ref, w_ref, h_ref, lg_ref, lb_ref, o_ref = refs
        z = (z_ref[...] * g_ref[...]).astype(BF16)
    else:
        z_ref, w_ref, h_ref, lg_ref, lb_ref, o_ref = refs
        z = z_ref[...].astype(BF16)
    y = DN_ALPHA * h_ref[...] + _dot(z, w_ref[...])
    o_ref[...] = _layer_norm(y, lg_ref[...], lb_ref[...])


def _mm_res_ln(z, gate, w, h, lg, lb):
    Np, D = h.shape
    TM = ROW_TM
    tile = pl.BlockSpec((TM, D), lambda i: (i, 0))
    has_gate = gate is not None
    args = [z] + ([gate] if has_gate else []) + [w, h, lg, lb]
    specs = [tile] + ([tile] if has_gate else []) + [_full(w.shape), tile, _full(lg.shape), _full(lb.shape)]
    return pl.pallas_call(
        functools.partial(_mm_res_ln_kernel, has_gate),
        out_shape=jax.ShapeDtypeStruct((Np, D), F32),
        grid=(Np // TM,),
        in_specs=specs,
        out_specs=tile,
        compiler_params=_cparams(("parallel",)),
        name="mm_res_ln",
    )(*args)


def _proj_kernel(x_ref, w_ref, o_ref):
    o_ref[...] = _dot(x_ref[...].astype(BF16), w_ref[...]).astype(o_ref.dtype)


def _proj(x, w):
    Np, D = x.shape
    Nout = w.shape[1]
    TM = ROW_TM
    return pl.pallas_call(
        _proj_kernel,
        out_shape=jax.ShapeDtypeStruct((Np, Nout), BF16),
        grid=(Np // TM,),
        in_specs=[pl.BlockSpec((TM, D), lambda i: (i, 0)), _full(w.shape)],
        out_specs=pl.BlockSpec((TM, Nout), lambda i: (i, 0)),
        compiler_params=_cparams(("parallel",)),
        name="proj",
    )(x, w)


def _attn_kernel(lam_init, lam_ref, sg_ref, q_ref, k_ref, v_ref, o_ref,
                 sa_ref, sb_ref, m1_ref, l1_ref, acc1_ref, m2_ref, l2_ref, acc2_ref):
    T = ATT_T
    qi = pl.program_id(2)
    last = pl.num_programs(2) - 1
    q = q_ref[0]
    lane = lax.broadcasted_iota(I32, q.shape, 1)
    zero = jnp.zeros_like(q)
    q1 = jnp.where(lane < DIFF_HD, q, zero)
    q2 = jnp.where(lane >= DIFF_HD, q, zero)
    m1_ref[...] = jnp.full_like(m1_ref, -1e30)
    m2_ref[...] = jnp.full_like(m2_ref, -1e30)
    l1_ref[...] = jnp.zeros_like(l1_ref)
    l2_ref[...] = jnp.zeros_like(l2_ref)
    acc1_ref[...] = jnp.zeros_like(acc1_ref)
    acc2_ref[...] = jnp.zeros_like(acc2_ref)
    nt = (((1,), (1,)), ((), ()))
    ones = jnp.ones((T, LANES), BF16)
    rel = lax.broadcasted_iota(I32, (T, T), 1) - lax.broadcasted_iota(I32, (T, T), 0)

    def chunk_start(kc):
        return pl.multiple_of(jnp.minimum(kc, last) * T, T)

    def halves(x):
        return [x[:, c * LANES:(c + 1) * LANES] for c in range(x.shape[1] // LANES)]

    def scores(g, buf):
        for half in range(2):
            kb = k_ref[0, pl.ds(chunk_start(2 * g + half), T), :]
            buf[0, :, half * T:(half + 1) * T] = lax.dot_general(q1, kb, nt, preferred_element_type=F32)
            buf[1, :, half * T:(half + 1) * T] = lax.dot_general(q2, kb, nt, preferred_element_type=F32)

    def consume(g, buf, masked):
        vext = jnp.concatenate(
            [jnp.concatenate([v_ref[0, pl.ds(chunk_start(2 * g + half), T), :], ones], axis=1) for half in range(2)],
            axis=0)
        for si, m_ref, l_ref, acc_ref in ((0, m1_ref, l1_ref, acc1_ref), (1, m2_ref, l2_ref, acc2_ref)):
            s = buf[si]
            if masked:
                s = jnp.concatenate(
                    [jnp.where(rel <= (qi - (2 * g + half)) * T, s[:, half * T:(half + 1) * T], -jnp.inf)
                     for half in range(2)], axis=1)
            parts = halves(s)
            smax = jnp.max(jnp.maximum(jnp.maximum(parts[0], parts[1]), jnp.maximum(parts[2], parts[3])),
                           axis=1, keepdims=True)
            m_old = m_ref[...]
            m_new = jnp.maximum(m_old, smax)
            alpha = jnp.exp2(m_old - m_new)
            p = jnp.exp2(s - jnp.concatenate([m_new] * len(parts), axis=1)).astype(BF16)
            d = _dot(p, vext)
            acc_ref[...] = alpha * acc_ref[...] + d[:, :LANES]
            l_ref[...] = alpha * l_ref[...] + d[:, LANES:]
            m_ref[...] = m_new

    n_free = qi // 2
    scores(0, sa_ref)

    def pair(i, carry):
        g = 2 * i
        scores(g + 1, sb_ref)
        consume(g, sa_ref, False)
        scores(g + 2, sa_ref)
        consume(g + 1, sb_ref, False)
        return carry

    n_pairs = n_free // 2
    lax.fori_loop(0, n_pairs, pair, 0)
    g_a = 2 * n_pairs
    odd = n_free % 2 == 1

    @pl.when(odd)
    def _():
        scores(g_a + 1, sb_ref)

    consume(g_a, sa_ref, True)

    @pl.when(odd)
    def _():
        consume(g_a + 1, sb_ref, True)


    lam_v = lam_ref[...]
    lam = (jnp.exp(jnp.sum(lam_v[0:1] * lam_v[1:2], axis=1, keepdims=True))
           - jnp.exp(jnp.sum(lam_v[2:3] * lam_v[3:4], axis=1, keepdims=True)) + lam_init)
    o = acc1_ref[...] / l1_ref[...] - lam * (acc2_ref[...] / l2_ref[...])
    o = o * lax.rsqrt(jnp.mean(o * o, axis=1, keepdims=True) + 1e-5) * sg_ref[...] * (1.0 - lam_init)
    o_ref[0] = o.astype(o_ref.dtype)


def _diff_attn(q, kv, lam4, subln_g, lam_init):
    B, Lp, D = q.shape
    T = ATT_T
    H = DIFF_HEADS
    HW = 2 * DIFF_HD
    vm = lambda *s: pltpu.VMEM(s, F32)
    return pl.pallas_call(
        functools.partial(_attn_kernel, lam_init),
        out_shape=jax.ShapeDtypeStruct((B, Lp, D), BF16),
        grid=(B, H, Lp // T),
        in_specs=[_full(lam4.shape), _full(subln_g.shape),
                  pl.BlockSpec((1, T, HW), lambda b, h, i: (b, i, h)),
                  pl.BlockSpec((1, Lp, HW), lambda b, h, i: (b, 0, h)),
                  pl.BlockSpec((1, Lp, HW), lambda b, h, i: (b, 0, H + h))],
        out_specs=pl.BlockSpec((1, T, HW), lambda b, h, i: (b, i, h)),
        scratch_shapes=[vm(2, T, 2 * T), vm(2, T, 2 * T),
                        vm(T, LANES), vm(T, LANES), vm(T, HW), vm(T, LANES), vm(T, LANES), vm(T, HW)],
        compiler_params=_cparams(("parallel", "parallel", "arbitrary")),
        name="diff_attn",
    )(lam4, subln_g, q, kv, kv)


def _route_kernel(x_ref, rw_ref, rb_ref, sg_ref, su_ref, sd_ref, tri_ref,
                  sh_o, idx_o, gate_o, rank_o, cnt_o, carry_ref):
    TM = ROUTE_TM
    G, EG = N_GROUPS, N_EXPERTS // N_GROUPS

    @pl.when(pl.program_id(0) == 0)
    def _():
        carry_ref[...] = jnp.zeros_like(carry_ref)

    x = x_ref[...]
    xb = x.astype(BF16)
    hmid = _dot(xb, sg_ref[...])
    hmid = hmid * jax.nn.sigmoid(hmid) * _dot(xb, su_ref[...])
    sh_o[...] = _dot(hmid.astype(BF16), sd_ref[...])

    logit = lax.dot_general(rw_ref[...], x, (((1,), (1,)), ((), ())),
                            preferred_element_type=F32, precision=lax.Precision.HIGHEST)
    s = jax.nn.sigmoid(logit)
    s3 = s.reshape(G, EG, TM)
    sel3 = (s + rb_ref[...]).reshape(G, EG, TM)
    io_j = lax.broadcasted_iota(I32, (G, EG, TM), 1)
    io_g = lax.broadcasted_iota(I32, (G, 1, TM), 0)
    neg = -jnp.inf
    m1 = jnp.max(sel3, axis=1, keepdims=True)
    i1 = jnp.min(jnp.where(sel3 == m1, io_j, EG), axis=1, keepdims=True)
    m2 = jnp.max(jnp.where(io_j == i1, neg, sel3), axis=1, keepdims=True)
    gsc = m1 + m2
    gkeep = jnp.zeros((G, 1, TM), F32)
    for _ in range(TOPK_GROUPS):
        m = jnp.max(gsc, axis=0, keepdims=True)
        gi = jnp.min(jnp.where(gsc == m, io_g, G), axis=0, keepdims=True)
        hit = io_g == gi
        gkeep = jnp.where(hit, 1.0, gkeep)
        gsc = jnp.where(hit, neg, gsc)
    cur = jnp.where(gkeep > 0.0, sel3, neg)
    io_e = io_g * EG + io_j
    hits, idxs, ws = [], [], []
    for _ in range(TOP_K):
        m = jnp.max(jnp.max(cur, axis=1, keepdims=True), axis=0, keepdims=True)
        ei = jnp.min(jnp.min(jnp.where(cur == m, io_e, N_EXPERTS), axis=1, keepdims=True), axis=0, keepdims=True)
        hit = io_e == ei
        ws.append(jnp.sum(jnp.sum(jnp.where(hit, s3, 0.0), axis=1, keepdims=True), axis=0, keepdims=True))
        cur = jnp.where(hit, neg, cur)
        hits.append(hit)
        idxs.append(ei)
    wsum = ws[0]
    for wv in ws[1:]:
        wsum = wsum + wv
    scale = ROUTED_SCALE / wsum
    onehot = jnp.zeros((G, EG, TM), F32)
    for hit in hits:
        onehot = jnp.where(hit, 1.0, onehot)
    oh2 = onehot.reshape(N_EXPERTS, TM)
    rank_full = (_dot(oh2.astype(BF16), tri_ref[...]) + carry_ref[:, :1]).reshape(G, EG, TM)
    for kk in range(TOP_K):
        rk = jnp.sum(jnp.sum(jnp.where(hits[kk], rank_full, 0.0), axis=1, keepdims=True), axis=0, keepdims=True)
        idx_o[kk:kk + 1, :] = idxs[kk].reshape(1, TM)
        gate_o[kk:kk + 1, :] = (ws[kk] * scale).reshape(1, TM)
        rank_o[kk:kk + 1, :] = rk.reshape(1, TM).astype(I32)
    carry_ref[...] = carry_ref[...] + jnp.sum(oh2, axis=1, keepdims=True)
    cnt_o[...] = carry_ref[...]


def _route_shared(x, rw_t, rb, sg, su, sd, tri):
    Np, D = x.shape
    TM = ROUTE_TM
    tok = pl.BlockSpec((TOP_K, TM), lambda i: (0, i))
    return pl.pallas_call(
        _route_kernel,
        out_shape=[jax.ShapeDtypeStruct((Np, D), F32),
                   jax.ShapeDtypeStruct((TOP_K, Np), I32),
                   jax.ShapeDtypeStruct((TOP_K, Np), F32),
                   jax.ShapeDtypeStruct((TOP_K, Np), I32),
                   jax.ShapeDtypeStruct((N_EXPERTS, LANES), F32)],
        grid=(Np // TM,),
        in_specs=[pl.BlockSpec((TM, D), lambda i: (i, 0)), _full(rw_t.shape), _full(rb.shape),
                  _full(sg.shape), _full(su.shape), _full(sd.shape), _full(tri.shape)],
        out_specs=[pl.BlockSpec((TM, D), lambda i: (i, 0)), tok, tok, tok, _full((N_EXPERTS, LANES))],
        scratch_shapes=[pltpu.VMEM((N_EXPERTS, LANES), F32)],
        compiler_params=_cparams(("arbitrary",)),
        name="moe_route",
    )(x, rw_t, rb, sg, su, sd, tri)


def _row_copy(src_ref, src_row, dst_ref, dst_row, sem):
    return pltpu.make_async_copy(src_ref.at[pl.ds(src_row, 1)], dst_ref.at[pl.ds(dst_row, 1)], sem)


def _dispatch_kernel(ps_ref, idx_ref, rank_ref, x_ref, xs_hbm, sem):
    def issue(n, c):
        for kk in range(TOP_K):
            dst = ps_ref[idx_ref[kk, n]] + rank_ref[kk, n]
            _row_copy(x_ref, n, xs_hbm, dst, sem).start()
        return c

    lax.fori_loop(0, DISP_TM, issue, 0)

    def drain(n, c):
        for kk in range(TOP_K):
            _row_copy(x_ref, n, xs_hbm, 0, sem).wait()
        return c

    lax.fori_loop(0, DISP_TM, drain, 0)


def _dispatch(x, idx_t, rank_t, pad_start, n_rows):
    Np, D = x.shape
    tok = pl.BlockSpec((TOP_K, DISP_TM), lambda i, ps: (0, i), memory_space=pltpu.SMEM)
    return pl.pallas_call(
        _dispatch_kernel,
        out_shape=jax.ShapeDtypeStruct((n_rows, D), F32),
        grid_spec=pltpu.PrefetchScalarGridSpec(
            num_scalar_prefetch=1,
            grid=(Np // DISP_TM,),
            in_specs=[tok, tok, pl.BlockSpec((DISP_TM, D), lambda i, ps: (i, 0))],
            out_specs=pl.BlockSpec(memory_space=pl.ANY),
            scratch_shapes=[pltpu.SemaphoreType.DMA(())],
        ),
        compiler_params=_cparams(("arbitrary",)),
        name="moe_dispatch",
    )(pad_start, idx_t, rank_t, x)


def _expert_kernel(be_ref, nu_ref, xs_ref, wg_ref, wu_ref, wd_ref, ys_ref):
    @pl.when(pl.program_id(0) < nu_ref[0])
    def _():
        x = xs_ref[...].astype(BF16)
        g = _dot(x, wg_ref[0])
        hmid = g * jax.nn.sigmoid(g) * _dot(x, wu_ref[0])
        ys_ref[...] = _dot(hmid.astype(BF16), wd_ref[0])


def _experts(xs, block_e, n_used, wg, wu, wd):
    P, D = xs.shape
    F = wg.shape[2]
    nb = P // MOE_BLK
    rows = pl.BlockSpec((MOE_BLK, D), lambda i, be, nu: (jnp.minimum(i, nu[0] - 1), 0))
    wspec = lambda s: pl.BlockSpec((1,) + s, lambda i, be, nu: (be[jnp.minimum(i, nu[0] - 1)], 0, 0))
    return pl.pallas_call(
        _expert_kernel,
        out_shape=jax.ShapeDtypeStruct((P, D), F32),
        grid_spec=pltpu.PrefetchScalarGridSpec(
            num_scalar_prefetch=2,
            grid=(nb,),
            in_specs=[rows, wspec((D, F)), wspec((D, F)), wspec((F, D))],
            out_specs=rows,
        ),
        compiler_params=_cparams(("arbitrary",)),
        name="moe_experts",
    )(block_e, n_used, xs, wg, wu, wd)


def _combine_kernel(ps_ref, idx_ref, rank_ref, gate_ref, sh_ref, h_ref, lg_ref, lb_ref, ys_hbm, o_ref,
                    buf, sem):
    def issue(n, c):
        for kk in range(TOP_K):
            src = ps_ref[idx_ref[kk, n]] + rank_ref[kk, n]
            _row_copy(ys_hbm, src, buf.at[kk], n, sem).start()
        return c

    lax.fori_loop(0, COMB_TM, issue, 0)

    def drain(n, c):
        for kk in range(TOP_K):
            _row_copy(ys_hbm, 0, buf.at[kk], n, sem).wait()
        return c

    lax.fori_loop(0, COMB_TM, drain, 0)
    gate = gate_ref[...]
    ffn = sh_ref[...]
    for kk in range(TOP_K):
        ffn = ffn + gate[:, kk:kk + 1] * buf[kk]
    o_ref[...] = _layer_norm(DN_ALPHA * h_ref[...] + ffn, lg_ref[...], lb_ref[...])


def _combine(ys, idx_t, rank_t, pad_start, gate, shared, h, lg, lb):
    Np, D = h.shape
    TM = COMB_TM
    tok = pl.BlockSpec((TOP_K, TM), lambda i, ps: (0, i), memory_space=pltpu.SMEM)
    tile = pl.BlockSpec((TM, D), lambda i, ps: (i, 0))
    vec = pl.BlockSpec((1, D), lambda i, ps: (0, 0))
    return pl.pallas_call(
        _combine_kernel,
        out_shape=jax.ShapeDtypeStruct((Np, D), F32),
        grid_spec=pltpu.PrefetchScalarGridSpec(
            num_scalar_prefetch=1,
            grid=(Np // TM,),
            in_specs=[tok, tok, pl.BlockSpec((TM, TOP_K), lambda i, ps: (i, 0)), tile, tile, vec, vec,
                      pl.BlockSpec(memory_space=pl.ANY)],
            out_specs=tile,
            scratch_shapes=[pltpu.VMEM((TOP_K, TM, D), F32), pltpu.SemaphoreType.DMA(())],
        ),
        compiler_params=_cparams(("arbitrary",)),
        name="moe_combine",
    )(pad_start, idx_t, rank_t, gate, shared, h, lg, lb, ys)


def _moe_layer(h2, l, lg, lb, router_w, router_b, w_gate, w_up, w_down, sh_gate, sh_up, sh_down, tri):
    Np, D = h2.shape
    shared, idx_t, gate_t, rank_t, cnt = _route_shared(
        h2, router_w[l].T, router_b[l].reshape(N_EXPERTS, 1),
        sh_gate[l].astype(BF16), sh_up[l].astype(BF16), sh_down[l].astype(BF16), tri)
    counts = cnt[:, 0].astype(I32)
    padded = (counts + MOE_BLK - 1) // MOE_BLK * MOE_BLK
    pad_end = jnp.cumsum(padded)
    pad_start = (pad_end - padded).astype(I32)
    n_blocks = Np * TOP_K // MOE_BLK + N_EXPERTS
    n_used = (pad_end[-1:] // MOE_BLK).astype(I32)
    blk_row0 = jnp.arange(n_blocks, dtype=I32) * MOE_BLK
    block_e = jnp.minimum(jnp.sum((pad_end[None, :] <= blk_row0[:, None]).astype(I32), axis=1), N_EXPERTS - 1)
    xs = _dispatch(h2, idx_t, rank_t, pad_start, n_blocks * MOE_BLK)
    ys = _experts(xs, block_e, n_used, w_gate[l].astype(BF16), w_up[l].astype(BF16), w_down[l].astype(BF16))
    return _combine(ys, idx_t, rank_t, pad_start, gate_t.T, shared, h2, lg, lb)


def _pad_cols(w, n):
    return jnp.pad(w, ((0, 0), (0, n - w.shape[1])))


def _pad_rows(w, n):
    return jnp.pad(w, ((0, n - w.shape[0]), (0, 0)))


def _trunk(x, meta_tokens, ln_mix_g, ln_mix_b, ln_ffn_g, ln_ffn_b,
           rw_mu, rw_w_rkv, rw_w0, rw_w_l1, rw_w_l2, rw_a0, rw_a_l1, rw_a_l2,
           rw_g_l1, rw_g_l2, rw_k_k, rw_k_a, rw_r_k, rw_lnx_g, rw_lnx_b, rw_w_out,
           rw_v0, rw_v_l1, rw_v_l2, kv_w,
           da_w_q, da_lam_q1, da_lam_k1, da_lam_q2, da_lam_k2, da_subln_g, da_w_out,
           moe_router_w, moe_router_b, moe_w_gate, moe_w_up, moe_w_down,
           moe_sh_gate, moe_sh_up, moe_sh_down):
    B, S, D = x.shape
    L = S + N_META
    Lp = -(-L // SEQ_ALIGN) * SEQ_ALIGN
    Np = B * Lp
    assert D == D_MODEL and Np % ROW_TM == 0 and Lp % PROJ_TM == 0 and Lp % SCAN_TC == 0 and Lp % ATT_T == 0
    meta = jnp.broadcast_to(meta_tokens[None].astype(x.dtype), (B, N_META, D))
    h = jnp.concatenate([meta, x, jnp.zeros((B, Lp - L, D), x.dtype)], axis=1)

    pc = _perm_cols()
    gs_np, gb_np = _head_sum_mats()
    gs, gb = jnp.asarray(gs_np), jnp.asarray(gb_np)
    tri = jnp.asarray(np.triu(np.ones((ROUTE_TM, ROUTE_TM), np.float32), 1)).astype(BF16)
    row = lambda vec: vec.reshape(1, -1)
    lane = np.arange(2 * D_MODEL)
    m_key, m_head = lane // 32, lane % 16
    c128 = np.arange(LANES) % 16
    vrow = np.arange(RWKV_HEAD)

    v_first = None
    kv = None
    for l in range(DEPTH):
        if l < N_A_LAYERS:
            p = {
                "mu": _pad_rows(rw_mu[l], SUBLANES),
                "wr": rw_w_rkv[l, 0][:, pc].astype(BF16),
                "wk": rw_w_rkv[l, 1][:, pc].astype(BF16),
                "wv": rw_w_rkv[l, 2][:, pc].astype(BF16),
                "w0": row(rw_w0[l][pc]),
                "wl1": _pad_cols(rw_w_l1[l], LANES).astype(BF16),
                "wl2": _pad_rows(rw_w_l2[l][:, pc], LANES).astype(BF16),
                "a0": row(rw_a0[l][pc]),
                "al1": _pad_cols(rw_a_l1[l], LANES).astype(BF16),
                "al2": _pad_rows(rw_a_l2[l][:, pc], LANES).astype(BF16),
                "gl1": _pad_cols(rw_g_l1[l], 2 * LANES).astype(BF16),
                "gl2": _pad_rows(rw_g_l2[l][:, pc], 2 * LANES).astype(BF16),
                "kk": row(rw_k_k[l][pc]),
                "ka": row(rw_k_a[l][pc]),
                "gs": gs,
                "gb": gb,
            }
            if l > 0:
                p["v0"] = row(rw_v0[l - 1][pc])
                p["vl1"] = _pad_cols(rw_v_l1[l - 1], LANES).astype(BF16)
                p["vl2"] = _pad_rows(rw_v_l2[l - 1][:, pc], LANES).astype(BF16)
            r, w, k, v, a, b, g = _rwkv_proj(h, v_first if l > 0 else None, p)
            if l == 0:
                v_first = v
            rk_m = rw_r_k[l][m_head, m_key].reshape(1, 2 * D_MODEL)
            lg_t = rw_lnx_g[l].reshape(RWKV_HEADS, RWKV_HEAD)[c128[None, :], vrow[:, None]]
            lb_t = rw_lnx_b[l].reshape(RWKV_HEADS, RWKV_HEAD)[c128[None, :], vrow[:, None]]
            z = _wkv_scan(r, w, k, v, a, b, rk_m, lg_t, lb_t)
            h2 = _mm_res_ln(z.reshape(Np, D), g.reshape(Np, D), rw_w_out[l][pc, :].astype(BF16),
                            h.reshape(Np, D), row(ln_mix_g[l]), row(ln_mix_b[l]))
        else:
            j = l - N_A_LAYERS
            h2 = h.reshape(Np, D)
            if kv is None:
                kv = _proj(h2, kv_w.astype(BF16)).reshape(B, Lp, 2 * D)
            q_scale = DIFF_HD ** -0.5 * math.log2(math.e)
            q = _proj(h2, (da_w_q[j] * q_scale).astype(BF16)).reshape(B, Lp, D)
            lam_init = 0.8 - 0.6 * math.exp(-0.3 * l)
            lam4 = jnp.stack([da_lam_q1[j], da_lam_k1[j], da_lam_q2[j], da_lam_k2[j]])
            o = _diff_attn(q, kv, _pad_rows(lam4, SUBLANES), row(da_subln_g[j]), lam_init)
            h2 = _mm_res_ln(o.reshape(Np, D), None, da_w_out[j].astype(BF16), h2,
                            row(ln_mix_g[l]), row(ln_mix_b[l]))
        h2 = _moe_layer(h2, l, row(ln_ffn_g[l]), row(ln_ffn_b[l]), moe_router_w, moe_router_b,
                        moe_w_gate, moe_w_up, moe_w_down, moe_sh_gate, moe_sh_up, moe_sh_down, tri)
        h = h2.reshape(B, Lp, D)
    return h[:, N_META:L]


_trunk_jit = jax.jit(_trunk)


def kernel(x, meta_tokens, ln_mix_g, ln_mix_b, ln_ffn_g, ln_ffn_b, rw_mu, rw_w_rkv, rw_w0, rw_w_l1, rw_w_l2, rw_a0, rw_a_l1, rw_a_l2, rw_g_l1, rw_g_l2, rw_k_k, rw_k_a, rw_r_k, rw_lnx_g, rw_lnx_b, rw_w_out, rw_v0, rw_v_l1, rw_v_l2, kv_w, da_w_q, da_lam_q1, da_lam_k1, da_lam_q2, da_lam_k2, da_subln_g, da_w_out, moe_router_w, moe_router_b, moe_w_gate, moe_w_up, moe_w_down, moe_sh_gate, moe_sh_up, moe_sh_down):
    return _trunk_jit(x, meta_tokens, ln_mix_g, ln_mix_b, ln_ffn_g, ln_ffn_b, rw_mu, rw_w_rkv, rw_w0, rw_w_l1,
                      rw_w_l2, rw_a0, rw_a_l1, rw_a_l2, rw_g_l1, rw_g_l2, rw_k_k, rw_k_a, rw_r_k, rw_lnx_g,
                      rw_lnx_b, rw_w_out, rw_v0, rw_v_l1, rw_v_l2, kv_w, da_w_q, da_lam_q1, da_lam_k1,
                      da_lam_q2, da_lam_k2, da_subln_g, da_w_out, moe_router_w, moe_router_b, moe_w_gate,
                      moe_w_up, moe_w_down, moe_sh_gate, moe_sh_up, moe_sh_down)
```

```python
import functools
import math

import numpy as np
import jax
import jax.numpy as jnp
from jax import lax
from jax.experimental import pallas as pl
from jax.experimental.pallas import tpu as pltpu

F32 = jnp.float32
BF16 = jnp.bfloat16
I32 = jnp.int32

D_MODEL = 1024
DEPTH = 4
N_META = 16
N_A_LAYERS = DEPTH // 2
RWKV_HEAD = 64
RWKV_HEADS = D_MODEL // RWKV_HEAD
RWKV_GN_EPS = 64e-5
DIFF_HEADS = 8
DIFF_HD = D_MODEL // (2 * DIFF_HEADS)
N_EXPERTS = 64
TOP_K = 8
N_GROUPS = 8
TOPK_GROUPS = 4
EXPERT_FF = D_MODEL // 4
ROUTED_SCALE = 2.5
DN_ALPHA = (2 * DEPTH) ** 0.25
LN_EPS = 1e-5

LANES = 128
SUBLANES = 8
VMEM_LIMIT_BYTES = 56 * 1024 * 1024
DMA_QUEUES = 2

SEQ_ALIGN = 256
PROJ_TM = 256
ROW_TM = 512
SCAN_TC = 128
SCAN_SUB = SUBLANES
ATT_T = 256
ROUTE_TM = 256
MOE_BLK = 512
DISP_TM = 512
COMB_TM = 128

_SIGMA = (0, 4, 1, 5, 2, 6, 3, 7)


def _cparams(sem):
    return pltpu.CompilerParams(dimension_semantics=sem, vmem_limit_bytes=VMEM_LIMIT_BYTES)


def _dot(a, b):
    return jnp.dot(a, b, preferred_element_type=F32)


def _dot_hi(a, b):
    return jnp.dot(a, b, preferred_element_type=F32, precision=lax.Precision.HIGHEST)


def _full(shape):
    nd = len(shape)
    return pl.BlockSpec(shape, lambda *_: (0,) * nd)


def _layer_norm(y, g, b):
    mu = jnp.mean(y, axis=-1, keepdims=True)
    d = y - mu
    var = jnp.mean(d * d, axis=-1, keepdims=True)
    return d * lax.rsqrt(var + LN_EPS) * g + b


def _perm_cols():
    L = np.arange(D_MODEL)
    j, p, hh = L // LANES, (L % LANES) // 16, L % 16
    i = 8 * j + np.asarray(_SIGMA)[p]
    return (hh * RWKV_HEAD + i).astype(np.int32)


def _head_sum_mats():
    L = np.arange(D_MODEL)
    c = np.arange(LANES)
    gs = (L[:, None] % 16 == c[None, :] % 16).astype(np.float32)
    gb = ((c[:, None] < 16) & (c[:, None] == L[None, :] % 16)).astype(np.float32)
    return gs, gb


def _rwkv_proj_kernel(has_vres, *refs):
    if has_vres:
        (x_ref, xp_ref, mu_ref, wr_ref, wk_ref, wv_ref, w0_ref, wl1_ref, wl2_ref, a0_ref, al1_ref,
         al2_ref, gl1_ref, gl2_ref, kk_ref, ka_ref, gs_ref, gb_ref, vf_ref, v0_ref, vl1_ref, vl2_ref,
         r_o, w_o, k_o, v_o, a_o, b_o, g_o) = refs
    else:
        (x_ref, xp_ref, mu_ref, wr_ref, wk_ref, wv_ref, w0_ref, wl1_ref, wl2_ref, a0_ref, al1_ref,
         al2_ref, gl1_ref, gl2_ref, kk_ref, ka_ref, gs_ref, gb_ref,
         r_o, w_o, k_o, v_o, a_o, b_o, g_o) = refs
    i = pl.program_id(1)
    x = x_ref[0]
    prev = xp_ref[0][SUBLANES - 1:SUBLANES, :]
    prev = jnp.where(i == 0, 0.0, prev)
    row = lax.broadcasted_iota(I32, x.shape, 0)
    xprev = jnp.where(row == 0, prev, pltpu.roll(x, 1, 0))
    dx = xprev - x

    def mix(s):
        return (x + dx * mu_ref[s:s + 1, :]).astype(BF16)

    xv = mix(2)
    r = _dot(mix(0), wr_ref[...])
    k = _dot(mix(1), wk_ref[...])
    v = _dot(xv, wv_ref[...])
    zw = w0_ref[...] + _dot(jnp.tanh(_dot(mix(3), wl1_ref[...])).astype(BF16), wl2_ref[...])
    decay = jnp.exp(-math.exp(-0.5) * jax.nn.sigmoid(zw))
    a = jax.nn.sigmoid(a0_ref[...] + _dot(_dot(mix(4), al1_ref[...]).astype(BF16), al2_ref[...]))
    g = _dot(jax.nn.sigmoid(_dot(mix(5), gl1_ref[...])).astype(BF16), gl2_ref[...])
    if has_vres:
        gate_v = jax.nn.sigmoid(v0_ref[...] + _dot(_dot(xv, vl1_ref[...]).astype(BF16), vl2_ref[...]))
        v = v + (vf_ref[0] - v) * gate_v
    kk = k * kk_ref[...]
    ss = _dot_hi(kk * kk, gs_ref[...])
    inv = 1.0 / jnp.maximum(jnp.sqrt(ss), 1e-12)
    kk = kk * _dot_hi(inv, gb_ref[...])
    k = k * (1.0 + (a - 1.0) * ka_ref[...])
    r_o[0] = r
    w_o[0] = decay
    k_o[0] = k
    v_o[0] = v
    a_o[0] = -kk
    b_o[0] = kk * a
    g_o[0] = g


def _rwkv_proj(h, vfirst, p):
    B, Lp, D = h.shape
    TM = PROJ_TM
    has_vres = vfirst is not None
    tile = pl.BlockSpec((1, TM, D), lambda b, i: (b, i, 0))
    prev8 = pl.BlockSpec((1, SUBLANES, D), lambda b, i: (b, jnp.maximum(i * (TM // SUBLANES) - 1, 0), 0))
    names = ["mu", "wr", "wk", "wv", "w0", "wl1", "wl2", "a0", "al1", "al2", "gl1", "gl2", "kk", "ka", "gs", "gb"]
    args = [h, h] + [p[n] for n in names]
    specs = [tile, prev8] + [_full(p[n].shape) for n in names]
    if has_vres:
        args += [vfirst, p["v0"], p["vl1"], p["vl2"]]
        specs += [tile, _full(p["v0"].shape), _full(p["vl1"].shape), _full(p["vl2"].shape)]
    out = jax.ShapeDtypeStruct((B, Lp, D), F32)
    return pl.pallas_call(
        functools.partial(_rwkv_proj_kernel, has_vres),
        out_shape=[out] * 7,
        grid=(B, Lp // TM),
        in_specs=specs,
        out_specs=[tile] * 7,
        compiler_params=_cparams(("parallel", "arbitrary")),
        name="rwkv_proj",
    )(*args)


def _wkv_kernel(r_ref, w_ref, k_ref, v_ref, a_ref, b_ref, rk_ref, lg_ref, lb_ref, o_ref,
                s_ref, ma_ref, mwr_ref, mw_ref, mb_ref, mk_ref, zv_ref, br_ref, kr_ref, rkk_ref, yc_ref):
    NV = RWKV_HEAD
    NCOL = 2 * D_MODEL // LANES
    NRB = NV // SUBLANES

    @pl.when(pl.program_id(0) == 0)
    def _():
        s_ref[...] = jnp.zeros_like(s_ref)

    lane = lax.broadcasted_iota(I32, (SUBLANES, LANES), 1)
    even = ((lane // 16) % 2) == 0
    grp = lane // 32

    def cs(j):
        return slice(j * LANES, (j + 1) * LANES)

    def merge(ref, t0):
        x0 = ref[0, pl.ds(t0, SUBLANES), :]
        x1 = ref[1, pl.ds(t0, SUBLANES), :]
        cols = []
        for j in range(D_MODEL // LANES):
            a0 = x0[:, cs(j)]
            a1 = x1[:, cs(j)]
            cols.append(jnp.where(even, a0, pltpu.roll(a1, 16, 1)))
            cols.append(jnp.where(even, pltpu.roll(a0, LANES - 16, 1), a1))
        return cols

    def fold(x):
        return (x + pltpu.roll(x, 32, 1)) + (pltpu.roll(x, 64, 1) + pltpu.roll(x, 96, 1))

    def colsum(xs):
        acc = xs[0]
        for x in xs[1:]:
            acc = acc + x
        return acc

    def sub_chunk(c, carry):
        t0 = pl.multiple_of(c * SCAN_SUB, SCAN_SUB)
        R = merge(r_ref, t0)
        W = merge(w_ref, t0)
        K = merge(k_ref, t0)
        V = merge(v_ref, t0)
        A = merge(a_ref, t0)
        Bm = merge(b_ref, t0)
        per_step = pl.ds(0, SCAN_SUB, stride=SUBLANES)
        for j in range(NCOL):
            ma_ref[j, per_step, :] = A[j]
            mw_ref[j, per_step, :] = W[j]
            mwr_ref[j, per_step, :] = W[j] * R[j]
            mb_ref[j, per_step, :] = Bm[j]
            mk_ref[j, per_step, :] = K[j]
        br_ref[per_step, :] = fold(colsum([Bm[j] * R[j] for j in range(NCOL)]))
        kr_ref[per_step, :] = fold(colsum([K[j] * R[j] for j in range(NCOL)]))
        rkk_ref[per_step, :] = fold(colsum([K[j] * R[j] * rk_ref[:, cs(j)] for j in range(NCOL)]))
        for j in range(NCOL):
            rolled = [V[j]] + [pltpu.roll(V[j], 32 * m, 1) for m in range(1, 4)]
            for q in range(4):
                z = rolled[(3 - q) % 4]
                for pos in range(2, -1, -1):
                    z = jnp.where(grp == pos, rolled[(pos - q) % 4], z)
                zv_ref[pl.ds(4 * j + q, SUBLANES, stride=NV), :] = z

        def tree_sum(xs):
            while len(xs) > 1:
                xs = [xs[a] + xs[a + 1] for a in range(0, len(xs), 2)]
            return xs[0]

        def step(t, carry2):
            trow = pl.ds(pl.multiple_of(t * SUBLANES, SUBLANES), 1)
            base = pl.multiple_of(t * NV, NV)
            brt = br_ref[trow, :]
            krt = kr_ref[trow, :]
            rkt = rkk_ref[trow, :]
            vcol, y = [], []
            for i in range(NRB):
                rows = slice(i * SUBLANES, (i + 1) * SUBLANES)
                srow = [s_ref[rows, cs(j)] for j in range(NCOL)]
                sa = fold(tree_sum([srow[j] * ma_ref[j, trow, :] for j in range(NCOL)]))
                yp = fold(tree_sum([srow[j] * mwr_ref[j, trow, :] for j in range(NCOL)]))
                vc = zv_ref[pl.ds(base + i * SUBLANES, SUBLANES), :]
                for j in range(NCOL):
                    s_ref[rows, cs(j)] = (srow[j] * mw_ref[j, trow, :] + sa * mb_ref[j, trow, :]
                                          + vc * mk_ref[j, trow, :])
                vcol.append(vc)
                y.append(yp + sa * brt + vc * krt)
            mean = jnp.sum(colsum(y), axis=0, keepdims=True) * (1.0 / NV)
            d = [yi - mean for yi in y]
            var = jnp.sum(colsum([di * di for di in d]), axis=0, keepdims=True) * (1.0 / NV)
            inv = lax.rsqrt(var + RWKV_GN_EPS)
            for i in range(NRB):
                rows = slice(i * SUBLANES, (i + 1) * SUBLANES)
                out = d[i] * inv * lg_ref[rows, :] + lb_ref[rows, :] + vcol[i] * rkt
                yc_ref[pl.ds(base + i * SUBLANES, SUBLANES), :] = out
            return carry2

        lax.fori_loop(0, SCAN_SUB, step, 0)

        cols = []
        for j in range(NCOL):
            acc = None
            for q in range(4):
                z = yc_ref[pl.ds(4 * j + q, SUBLANES, stride=NV), :]
                acc = z if q == 0 else jnp.where(grp == q, z, acc)
            cols.append(acc)
        for j in range(D_MODEL // LANES):
            c0, c1 = cols[2 * j], cols[2 * j + 1]
            o_ref[0, pl.ds(t0, SUBLANES), cs(j)] = jnp.where(even, c0, pltpu.roll(c1, 16, 1))
            o_ref[1, pl.ds(t0, SUBLANES), cs(j)] = jnp.where(even, pltpu.roll(c0, LANES - 16, 1), c1)
        return carry

    lax.fori_loop(0, SCAN_TC // SCAN_SUB, sub_chunk, 0)


def _wkv_scan(r, w, k, v, a, b, rk_m, lg_t, lb_t):
    B, Lp, D = r.shape
    assert B == 2, "the scan packs exactly two batch rows into the lane dimension"
    blk = pl.BlockSpec((B, SCAN_TC, D), lambda i: (0, i, 0))
    vm = lambda *s: pltpu.VMEM(s, F32)
    return pl.pallas_call(
        _wkv_kernel,
        out_shape=jax.ShapeDtypeStruct((B, Lp, D), F32),
        grid=(Lp // SCAN_TC,),
        in_specs=[blk] * 6 + [_full(rk_m.shape), _full(lg_t.shape), _full(lb_t.shape)],
        out_specs=blk,
        scratch_shapes=[vm(RWKV_HEAD, 2 * D)] + [vm(2 * D // LANES, SCAN_SUB * SUBLANES, LANES)] * 5
        + [vm(SCAN_SUB * RWKV_HEAD, LANES)] + [vm(SCAN_SUB * SUBLANES, LANES)] * 3
        + [vm(SCAN_SUB * RWKV_HEAD, LANES)],
        compiler_params=_cparams(("arbitrary",)),
        name="wkv_scan",
    )(r, w, k, v, a, b, rk_m, lg_t, lb_t)


def _mm_res_ln_kernel(has_gate, *refs):
    if has_gate:
        z_ref, g_ref, w_ref, h_ref, lg_ref, lb_ref, o_ref = refs
        z = (z_ref[...] * g_ref[...]).astype(BF16)
    else:
        z_ref, w_ref, h_ref, lg_ref, lb_ref, o_ref = refs
        z = z_ref[...].astype(BF16)
    y = DN_ALPHA * h_ref[...] + _dot(z, w_ref[...])
    o_ref[...] = _layer_norm(y, lg_ref[...], lb_ref[...])


def _mm_res_ln(z, gate, w, h, lg, lb):
    Np, D = h.shape
    TM = ROW_TM
    tile = pl.BlockSpec((TM, D), lambda i: (i, 0))
    has_gate = gate is not None
    args = [z] + ([gate] if has_gate else []) + [w, h, lg, lb]
    specs = [tile] + ([tile] if has_gate else []) + [_full(w.shape), tile, _full(lg.shape), _full(lb.shape)]
    return pl.pallas_call(
        functools.partial(_mm_res_ln_kernel, has_gate),
        out_shape=jax.ShapeDtypeStruct((Np, D), F32),
        grid=(Np // TM,),
        in_specs=specs,
        out_specs=tile,
        compiler_params=_cparams(("parallel",)),
        name="mm_res_ln",
    )(*args)


def _proj_kernel(x_ref, w_ref, o_ref):
    o_ref[...] = _dot(x_ref[...].astype(BF16), w_ref[...]).astype(o_ref.dtype)


def _proj(x, w):
    Np, D = x.shape
    Nout = w.shape[1]
    TM = ROW_TM
    return pl.pallas_call(
        _proj_kernel,
        out_shape=jax.ShapeDtypeStruct((Np, Nout), BF16),
        grid=(Np // TM,),
        in_specs=[pl.BlockSpec((TM, D), lambda i: (i, 0)), _full(w.shape)],
        out_specs=pl.BlockSpec((TM, Nout), lambda i: (i, 0)),
        compiler_params=_cparams(("parallel",)),
        name="proj",
    )(x, w)


def _attn_kernel(lam_init, lam_ref, sg_ref, q_ref, k_ref, v_ref, o_ref,
                 sa_ref, sb_ref, m1_ref, l1_ref, acc1_ref, m2_ref, l2_ref, acc2_ref):
    T = ATT_T
    qi = pl.program_id(2)
    last = pl.num_programs(2) - 1
    q = q_ref[0]
    lane = lax.broadcasted_iota(I32, q.shape, 1)
    zero = jnp.zeros_like(q)
    q1 = jnp.where(lane < DIFF_HD, q, zero)
    q2 = jnp.where(lane >= DIFF_HD, q, zero)
    m1_ref[...] = jnp.full_like(m1_ref, -1e30)
    m2_ref[...] = jnp.full_like(m2_ref, -1e30)
    l1_ref[...] = jnp.zeros_like(l1_ref)
    l2_ref[...] = jnp.zeros_like(l2_ref)
    acc1_ref[...] = jnp.zeros_like(acc1_ref)
    acc2_ref[...] = jnp.zeros_like(acc2_ref)
    nt = (((1,), (1,)), ((), ()))
    ones = jnp.ones((T, LANES), BF16)
    rel = lax.broadcasted_iota(I32, (T, T), 1) - lax.broadcasted_iota(I32, (T, T), 0)

    def chunk_start(kc):
        return pl.multiple_of(jnp.minimum(kc, last) * T, T)

    def halves(x):
        return [x[:, c * LANES:(c + 1) * LANES] for c in range(x.shape[1] // LANES)]

    def scores(g, buf):
        for half in range(2):
            kb = k_ref[0, pl.ds(chunk_start(2 * g + half), T), :]
            buf[0, :, half * T:(half + 1) * T] = lax.dot_general(q1, kb, nt, preferred_element_type=F32)
            buf[1, :, half * T:(half + 1) * T] = lax.dot_general(q2, kb, nt, preferred_element_type=F32)

    def consume(g, buf, masked):
        vext = jnp.concatenate(
            [jnp.concatenate([v_ref[0, pl.ds(chunk_start(2 * g + half), T), :], ones], axis=1) for half in range(2)],
            axis=0)
        for si, m_ref, l_ref, acc_ref in ((0, m1_ref, l1_ref, acc1_ref), (1, m2_ref, l2_ref, acc2_ref)):
            s = buf[si]
            if masked:
                s = jnp.concatenate(
                    [jnp.where(rel <= (qi - (2 * g + half)) * T, s[:, half * T:(half + 1) * T], -jnp.inf)
                     for half in range(2)], axis=1)
            parts = halves(s)
            smax = jnp.max(jnp.maximum(jnp.maximum(parts[0], parts[1]), jnp.maximum(parts[2], parts[3])),
                           axis=1, keepdims=True)
            m_old = m_ref[...]
            m_new = jnp.maximum(m_old, smax)
            alpha = jnp.exp2(m_old - m_new)
            p = jnp.exp2(s - jnp.concatenate([m_new] * len(parts), axis=1)).astype(BF16)
            d = _dot(p, vext)
            acc_ref[...] = alpha * acc_ref[...] + d[:, :LANES]
            l_ref[...] = alpha * l_ref[...] + d[:, LANES:]
            m_ref[...] = m_new

    n_free = qi // 2
    scores(0, sa_ref)

    def pair(i, carry):
        g = 2 * i
        scores(g + 1, sb_ref)
        consume(g, sa_ref, False)
        scores(g + 2, sa_ref)
        consume(g + 1, sb_ref, False)
        return carry

    n_pairs = n_free // 2
    lax.fori_loop(0, n_pairs, pair, 0)
    g_a = 2 * n_pairs
    odd = n_free % 2 == 1

    @pl.when(odd)
    def _():
        scores(g_a + 1, sb_ref)

    consume(g_a, sa_ref, True)

    @pl.when(odd)
    def _():
        consume(g_a + 1, sb_ref, True)


    lam_v = lam_ref[...]
    lam = (jnp.exp(jnp.sum(lam_v[0:1] * lam_v[1:2], axis=1, keepdims=True))
           - jnp.exp(jnp.sum(lam_v[2:3] * lam_v[3:4], axis=1, keepdims=True)) + lam_init)
    o = acc1_ref[...] / l1_ref[...] - lam * (acc2_ref[...] / l2_ref[...])
    o = o * lax.rsqrt(jnp.mean(o * o, axis=1, keepdims=True) + 1e-5) * sg_ref[...] * (1.0 - lam_init)
    o_ref[0] = o.astype(o_ref.dtype)


def _diff_attn(q, kv, lam4, subln_g, lam_init):
    B, Lp, D = q.shape
    T = ATT_T
    H = DIFF_HEADS
    HW = 2 * DIFF_HD
    vm = lambda *s: pltpu.VMEM(s, F32)
    return pl.pallas_call(
        functools.partial(_attn_kernel, lam_init),
        out_shape=jax.ShapeDtypeStruct((B, Lp, D), BF16),
        grid=(B, H, Lp // T),
        in_specs=[_full(lam4.shape), _full(subln_g.shape),
                  pl.BlockSpec((1, T, HW), lambda b, h, i: (b, i, h)),
                  pl.BlockSpec((1, Lp, HW), lambda b, h, i: (b, 0, h)),
                  pl.BlockSpec((1, Lp, HW), lambda b, h, i: (b, 0, H + h))],
        out_specs=pl.BlockSpec((1, T, HW), lambda b, h, i: (b, i, h)),
        scratch_shapes=[vm(2, T, 2 * T), vm(2, T, 2 * T),
                        vm(T, LANES), vm(T, LANES), vm(T, HW), vm(T, LANES), vm(T, LANES), vm(T, HW)],
        compiler_params=_cparams(("parallel", "parallel", "arbitrary")),
        name="diff_attn",
    )(lam4, subln_g, q, kv, kv)


def _route_kernel(x_ref, rw_ref, rb_ref, sg_ref, su_ref, sd_ref, tri_ref,
                  sh_o, idx_o, gate_o, rank_o, cnt_o, carry_ref):
    TM = ROUTE_TM
    G, EG = N_GROUPS, N_EXPERTS // N_GROUPS

    @pl.when(pl.program_id(0) == 0)
    def _():
        carry_ref[...] = jnp.zeros_like(carry_ref)

    x = x_ref[...]
    xb = x.astype(BF16)
    hmid = _dot(xb, sg_ref[...])
    hmid = hmid * jax.nn.sigmoid(hmid) * _dot(xb, su_ref[...])
    sh_o[...] = _dot(hmid.astype(BF16), sd_ref[...])

    logit = lax.dot_general(rw_ref[...], x, (((1,), (1,)), ((), ())),
                            preferred_element_type=F32, precision=lax.Precision.HIGHEST)
    s = jax.nn.sigmoid(logit)
    s3 = s.reshape(G, EG, TM)
    sel3 = (s + rb_ref[...]).reshape(G, EG, TM)
    io_j = lax.broadcasted_iota(I32, (G, EG, TM), 1)
    io_g = lax.broadcasted_iota(I32, (G, 1, TM), 0)
    neg = -jnp.inf
    m1 = jnp.max(sel3, axis=1, keepdims=True)
    i1 = jnp.min(jnp.where(sel3 == m1, io_j, EG), axis=1, keepdims=True)
    m2 = jnp.max(jnp.where(io_j == i1, neg, sel3), axis=1, keepdims=True)
    gsc = m1 + m2
    gkeep = jnp.zeros((G, 1, TM), F32)
    for _ in range(TOPK_GROUPS):
        m = jnp.max(gsc, axis=0, keepdims=True)
        gi = jnp.min(jnp.where(gsc == m, io_g, G), axis=0, keepdims=True)
        hit = io_g == gi
        gkeep = jnp.where(hit, 1.0, gkeep)
        gsc = jnp.where(hit, neg, gsc)
    cur = jnp.where(gkeep > 0.0, sel3, neg)
    io_e = io_g * EG + io_j
    hits, idxs, ws = [], [], []
    for _ in range(TOP_K):
        m = jnp.max(jnp.max(cur, axis=1, keepdims=True), axis=0, keepdims=True)
        ei = jnp.min(jnp.min(jnp.where(cur == m, io_e, N_EXPERTS), axis=1, keepdims=True), axis=0, keepdims=True)
        hit = io_e == ei
        ws.append(jnp.sum(jnp.sum(jnp.where(hit, s3, 0.0), axis=1, keepdims=True), axis=0, keepdims=True))
        cur = jnp.where(hit, neg, cur)
        hits.append(hit)
        idxs.append(ei)
    wsum = ws[0]
    for wv in ws[1:]:
        wsum = wsum + wv
    scale = ROUTED_SCALE / wsum
    onehot = jnp.zeros((G, EG, TM), F32)
    for hit in hits:
        onehot = jnp.where(hit, 1.0, onehot)
    oh2 = onehot.reshape(N_EXPERTS, TM)
    rank_full = (_dot(oh2.astype(BF16), tri_ref[...]) + carry_ref[:, :1]).reshape(G, EG, TM)
    for kk in range(TOP_K):
        rk = jnp.sum(jnp.sum(jnp.where(hits[kk], rank_full, 0.0), axis=1, keepdims=True), axis=0, keepdims=True)
        idx_o[kk:kk + 1, :] = idxs[kk].reshape(1, TM)
        gate_o[kk:kk + 1, :] = (ws[kk] * scale).reshape(1, TM)
        rank_o[kk:kk + 1, :] = rk.reshape(1, TM).astype(I32)
    carry_ref[...] = carry_ref[...] + jnp.sum(oh2, axis=1, keepdims=True)
    cnt_o[...] = carry_ref[...]


def _route_shared(x, rw_t, rb, sg, su, sd, tri):
    Np, D = x.shape
    TM = ROUTE_TM
    tok = pl.BlockSpec((TOP_K, TM), lambda i: (0, i))
    return pl.pallas_call(
        _route_kernel,
        out_shape=[jax.ShapeDtypeStruct((Np, D), F32),
                   jax.ShapeDtypeStruct((TOP_K, Np), I32),
                   jax.ShapeDtypeStruct((TOP_K, Np), F32),
                   jax.ShapeDtypeStruct((TOP_K, Np), I32),
                   jax.ShapeDtypeStruct((N_EXPERTS, LANES), F32)],
        grid=(Np // TM,),
        in_specs=[pl.BlockSpec((TM, D), lambda i: (i, 0)), _full(rw_t.shape), _full(rb.shape),
                  _full(sg.shape), _full(su.shape), _full(sd.shape), _full(tri.shape)],
        out_specs=[pl.BlockSpec((TM, D), lambda i: (i, 0)), tok, tok, tok, _full((N_EXPERTS, LANES))],
        scratch_shapes=[pltpu.VMEM((N_EXPERTS, LANES), F32)],
        compiler_params=_cparams(("arbitrary",)),
        name="moe_route",
    )(x, rw_t, rb, sg, su, sd, tri)


def _row_copy(src_ref, src_row, dst_ref, dst_row, sem):
    return pltpu.make_async_copy(src_ref.at[pl.ds(src_row, 1)], dst_ref.at[pl.ds(dst_row, 1)], sem)


def _dispatch_kernel(ps_ref, idx_ref, rank_ref, x_ref, xs_hbm, sem):
    def issue(n, c):
        for kk in range(TOP_K):
            dst = ps_ref[idx_ref[kk, n]] + rank_ref[kk, n]
            _row_copy(x_ref, n, xs_hbm, dst, sem).start(priority=kk % DMA_QUEUES)
        return c

    lax.fori_loop(0, DISP_TM, issue, 0)

    def drain(n, c):
        for kk in range(TOP_K):
            _row_copy(x_ref, n, xs_hbm, 0, sem).wait()
        return c

    lax.fori_loop(0, DISP_TM, drain, 0)


def _dispatch(x, idx_t, rank_t, pad_start, n_rows):
    Np, D = x.shape
    tok = pl.BlockSpec((TOP_K, DISP_TM), lambda i, ps: (0, i), memory_space=pltpu.SMEM)
    return pl.pallas_call(
        _dispatch_kernel,
        out_shape=jax.ShapeDtypeStruct((n_rows, D), F32),
        grid_spec=pltpu.PrefetchScalarGridSpec(
            num_scalar_prefetch=1,
            grid=(Np // DISP_TM,),
            in_specs=[tok, tok, pl.BlockSpec((DISP_TM, D), lambda i, ps: (i, 0))],
            out_specs=pl.BlockSpec(memory_space=pl.ANY),
            scratch_shapes=[pltpu.SemaphoreType.DMA(())],
        ),
        compiler_params=_cparams(("arbitrary",)),
        name="moe_dispatch",
    )(pad_start, idx_t, rank_t, x)


def _expert_kernel(be_ref, nu_ref, xs_ref, wg_ref, wu_ref, wd_ref, ys_ref):
    @pl.when(pl.program_id(0) < nu_ref[0])
    def _():
        x = xs_ref[...].astype(BF16)
        g = _dot(x, wg_ref[0])
        hmid = g * jax.nn.sigmoid(g) * _dot(x, wu_ref[0])
        ys_ref[...] = _dot(hmid.astype(BF16), wd_ref[0])


def _experts(xs, block_e, n_used, wg, wu, wd):
    P, D = xs.shape
    F = wg.shape[2]
    nb = P // MOE_BLK
    rows = pl.BlockSpec((MOE_BLK, D), lambda i, be, nu: (jnp.minimum(i, nu[0] - 1), 0))
    wspec = lambda s: pl.BlockSpec((1,) + s, lambda i, be, nu: (be[jnp.minimum(i, nu[0] - 1)], 0, 0))
    return pl.pallas_call(
        _expert_kernel,
        out_shape=jax.ShapeDtypeStruct((P, D), F32),
        grid_spec=pltpu.PrefetchScalarGridSpec(
            num_scalar_prefetch=2,
            grid=(nb,),
            in_specs=[rows, wspec((D, F)), wspec((D, F)), wspec((F, D))],
            out_specs=rows,
        ),
        compiler_params=_cparams(("arbitrary",)),
        name="moe_experts",
    )(block_e, n_used, xs, wg, wu, wd)


def _combine_kernel(ps_ref, idx_ref, rank_ref, gate_ref, sh_ref, h_ref, lg_ref, lb_ref, ys_hbm, o_ref,
                    buf, sem):
    def issue(n, c):
        for kk in range(TOP_K):
            src = ps_ref[idx_ref[kk, n]] + rank_ref[kk, n]
            _row_copy(ys_hbm, src, buf.at[kk], n, sem).start(priority=kk % DMA_QUEUES)
        return c

    lax.fori_loop(0, COMB_TM, issue, 0)

    def drain(n, c):
        for kk in range(TOP_K):
            _row_copy(ys_hbm, 0, buf.at[kk], n, sem).wait()
        return c

    lax.fori_loop(0, COMB_TM, drain, 0)
    gate = gate_ref[...]
    ffn = sh_ref[...]
    for kk in range(TOP_K):
        ffn = ffn + gate[:, kk:kk + 1] * buf[kk]
    o_ref[...] = _layer_norm(DN_ALPHA * h_ref[...] + ffn, lg_ref[...], lb_ref[...])


def _combine(ys, idx_t, rank_t, pad_start, gate, shared, h, lg, lb):
    Np, D = h.shape
    TM = COMB_TM
    tok = pl.BlockSpec((TOP_K, TM), lambda i, ps: (0, i), memory_space=pltpu.SMEM)
    tile = pl.BlockSpec((TM, D), lambda i, ps: (i, 0))
    vec = pl.BlockSpec((1, D), lambda i, ps: (0, 0))
    return pl.pallas_call(
        _combine_kernel,
        out_shape=jax.ShapeDtypeStruct((Np, D), F32),
        grid_spec=pltpu.PrefetchScalarGridSpec(
            num_scalar_prefetch=1,
            grid=(Np // TM,),
            in_specs=[tok, tok, pl.BlockSpec((TM, TOP_K), lambda i, ps: (i, 0)), tile, tile, vec, vec,
                      pl.BlockSpec(memory_space=pl.ANY)],
            out_specs=tile,
            scratch_shapes=[pltpu.VMEM((TOP_K, TM, D), F32), pltpu.SemaphoreType.DMA(())],
        ),
        compiler_params=_cparams(("arbitrary",)),
        name="moe_combine",
    )(pad_start, idx_t, rank_t, gate, shared, h, lg, lb, ys)


def _moe_layer(h2, l, lg, lb, router_w, router_b, w_gate, w_up, w_down, sh_gate, sh_up, sh_down, tri):
    Np, D = h2.shape
    shared, idx_t, gate_t, rank_t, cnt = _route_shared(
        h2, router_w[l].T, router_b[l].reshape(N_EXPERTS, 1),
        sh_gate[l].astype(BF16), sh_up[l].astype(BF16), sh_down[l].astype(BF16), tri)
    counts = cnt[:, 0].astype(I32)
    padded = (counts + MOE_BLK - 1) // MOE_BLK * MOE_BLK
    pad_end = jnp.cumsum(padded)
    pad_start = (pad_end - padded).astype(I32)
    n_blocks = Np * TOP_K // MOE_BLK + N_EXPERTS
    n_used = (pad_end[-1:] // MOE_BLK).astype(I32)
    blk_row0 = jnp.arange(n_blocks, dtype=I32) * MOE_BLK
    block_e = jnp.minimum(jnp.sum((pad_end[None, :] <= blk_row0[:, None]).astype(I32), axis=1), N_EXPERTS - 1)
    xs = _dispatch(h2, idx_t, rank_t, pad_start, n_blocks * MOE_BLK)
    ys = _experts(xs, block_e, n_used, w_gate[l].astype(BF16), w_up[l].astype(BF16), w_down[l].astype(BF16))
    return _combine(ys, idx_t, rank_t, pad_start, gate_t.T, shared, h2, lg, lb)


def _pad_cols(w, n):
    return jnp.pad(w, ((0, 0), (0, n - w.shape[1])))


def _pad_rows(w, n):
    return jnp.pad(w, ((0, n - w.shape[0]), (0, 0)))


def _trunk(x, meta_tokens, ln_mix_g, ln_mix_b, ln_ffn_g, ln_ffn_b,
           rw_mu, rw_w_rkv, rw_w0, rw_w_l1, rw_w_l2, rw_a0, rw_a_l1, rw_a_l2,
           rw_g_l1, rw_g_l2, rw_k_k, rw_k_a, rw_r_k, rw_lnx_g, rw_lnx_b, rw_w_out,
           rw_v0, rw_v_l1, rw_v_l2, kv_w,
           da_w_q, da_lam_q1, da_lam_k1, da_lam_q2, da_lam_k2, da_subln_g, da_w_out,
           moe_router_w, moe_router_b, moe_w_gate, moe_w_up, moe_w_down,
           moe_sh_gate, moe_sh_up, moe_sh_down):
    B, S, D = x.shape
    L = S + N_META
    Lp = -(-L // SEQ_ALIGN) * SEQ_ALIGN
    Np = B * Lp
    assert D == D_MODEL and Np % ROW_TM == 0 and Lp % PROJ_TM == 0 and Lp % SCAN_TC == 0 and Lp % ATT_T == 0
    meta = jnp.broadcast_to(meta_tokens[None].astype(x.dtype), (B, N_META, D))
    h = jnp.concatenate([meta, x, jnp.zeros((B, Lp - L, D), x.dtype)], axis=1)

    pc = _perm_cols()
    gs_np, gb_np = _head_sum_mats()
    gs, gb = jnp.asarray(gs_np), jnp.asarray(gb_np)
    tri = jnp.asarray(np.triu(np.ones((ROUTE_TM, ROUTE_TM), np.float32), 1)).astype(BF16)
    row = lambda vec: vec.reshape(1, -1)
    lane = np.arange(2 * D_MODEL)
    m_key, m_head = lane // 32, lane % 16
    c128 = np.arange(LANES) % 16
    vrow = np.arange(RWKV_HEAD)

    v_first = None
    kv = None
    for l in range(DEPTH):
        if l < N_A_LAYERS:
            p = {
                "mu": _pad_rows(rw_mu[l], SUBLANES),
                "wr": rw_w_rkv[l, 0][:, pc].astype(BF16),
                "wk": rw_w_rkv[l, 1][:, pc].astype(BF16),
                "wv": rw_w_rkv[l, 2][:, pc].astype(BF16),
                "w0": row(rw_w0[l][pc]),
                "wl1": _pad_cols(rw_w_l1[l], LANES).astype(BF16),
                "wl2": _pad_rows(rw_w_l2[l][:, pc], LANES).astype(BF16),
                "a0": row(rw_a0[l][pc]),
                "al1": _pad_cols(rw_a_l1[l], LANES).astype(BF16),
                "al2": _pad_rows(rw_a_l2[l][:, pc], LANES).astype(BF16),
                "gl1": _pad_cols(rw_g_l1[l], 2 * LANES).astype(BF16),
                "gl2": _pad_rows(rw_g_l2[l][:, pc], 2 * LANES).astype(BF16),
                "kk": row(rw_k_k[l][pc]),
                "ka": row(rw_k_a[l][pc]),
                "gs": gs,
                "gb": gb,
            }
            if l > 0:
                p["v0"] = row(rw_v0[l - 1][pc])
                p["vl1"] = _pad_cols(rw_v_l1[l - 1], LANES).astype(BF16)
                p["vl2"] = _pad_rows(rw_v_l2[l - 1][:, pc], LANES).astype(BF16)
            r, w, k, v, a, b, g = _rwkv_proj(h, v_first if l > 0 else None, p)
            if l == 0:
                v_first = v
            rk_m = rw_r_k[l][m_head, m_key].reshape(1, 2 * D_MODEL)
            lg_t = rw_lnx_g[l].reshape(RWKV_HEADS, RWKV_HEAD)[c128[None, :], vrow[:, None]]
            lb_t = rw_lnx_b[l].reshape(RWKV_HEADS, RWKV_HEAD)[c128[None, :], vrow[:, None]]
            z = _wkv_scan(r, w, k, v, a, b, rk_m, lg_t, lb_t)
            h2 = _mm_res_ln(z.reshape(Np, D), g.reshape(Np, D), rw_w_out[l][pc, :].astype(BF16),
                            h.reshape(Np, D), row(ln_mix_g[l]), row(ln_mix_b[l]))
        else:
            j = l - N_A_LAYERS
            h2 = h.reshape(Np, D)
            if kv is None:
                kv = _proj(h2, kv_w.astype(BF16)).reshape(B, Lp, 2 * D)
            q_scale = DIFF_HD ** -0.5 * math.log2(math.e)
            q = _proj(h2, (da_w_q[j] * q_scale).astype(BF16)).reshape(B, Lp, D)
            lam_init = 0.8 - 0.6 * math.exp(-0.3 * l)
            lam4 = jnp.stack([da_lam_q1[j], da_lam_k1[j], da_lam_q2[j], da_lam_k2[j]])
            o = _diff_attn(q, kv, _pad_rows(lam4, SUBLANES), row(da_subln_g[j]), lam_init)
            h2 = _mm_res_ln(o.reshape(Np, D), None, da_w_out[j].astype(BF16), h2,
                            row(ln_mix_g[l]), row(ln_mix_b[l]))
        h2 = _moe_layer(h2, l, row(ln_ffn_g[l]), row(ln_ffn_b[l]), moe_router_w, moe_router_b,
                        moe_w_gate, moe_w_up, moe_w_down, moe_sh_gate, moe_sh_up, moe_sh_down, tri)
        h = h2.reshape(B, Lp, D)
    return h[:, N_META:L]


_trunk_jit = jax.jit(_trunk)


def kernel(x, meta_tokens, ln_mix_g, ln_mix_b, ln_ffn_g, ln_ffn_b, rw_mu, rw_w_rkv, rw_w0, rw_w_l1, rw_w_l2, rw_a0, rw_a_l1, rw_a_l2, rw_g_l1, rw_g_l2, rw_k_k, rw_k_a, rw_r_k, rw_lnx_g, rw_lnx_b, rw_w_out, rw_v0, rw_v_l1, rw_v_l2, kv_w, da_w_q, da_lam_q1, da_lam_k1, da_lam_q2, da_lam_k2, da_subln_g, da_w_out, moe_router_w, moe_router_b, moe_w_gate, moe_w_up, moe_w_down, moe_sh_gate, moe_sh_up, moe_sh_down):
    return _trunk_jit(x, meta_tokens, ln_mix_g, ln_mix_b, ln_ffn_g, ln_ffn_b, rw_mu, rw_w_rkv, rw_w0, rw_w_l1,
                      rw_w_l2, rw_a0, rw_a_l1, rw_a_l2, rw_g_l1, rw_g_l2, rw_k_k, rw_k_a, rw_r_k, rw_lnx_g,
                      rw_lnx_b, rw_w_out, rw_v0, rw_v_l1, rw_v_l2, kv_w, da_w_q, da_lam_q1, da_lam_k1,
                      da_lam_q2, da_lam_k2, da_subln_g, da_w_out, moe_router_w, moe_router_b, moe_w_gate,
                      moe_w_up, moe_w_down, moe_sh_gate, moe_sh_up, moe_sh_down)
```

```python
import functools
import math

import numpy as np
import jax
import jax.numpy as jnp
from jax import lax
from jax.experimental import pallas as pl
from jax.experimental.pallas import tpu as pltpu

F32 = jnp.float32
BF16 = jnp.bfloat16
I32 = jnp.int32

D_MODEL = 1024
DEPTH = 4
N_META = 16
N_A_LAYERS = DEPTH // 2
RWKV_HEAD = 64
RWKV_HEADS = D_MODEL // RWKV_HEAD
RWKV_GN_EPS = 64e-5
DIFF_HEADS = 8
DIFF_HD = D_MODEL // (2 * DIFF_HEADS)
N_EXPERTS = 64
TOP_K = 8
N_GROUPS = 8
TOPK_GROUPS = 4
EXPERT_FF = D_MODEL // 4
ROUTED_SCALE = 2.5
DN_ALPHA = (2 * DEPTH) ** 0.25
LN_EPS = 1e-5

LANES = 128
SUBLANES = 8
VMEM_LIMIT_BYTES = 56 * 1024 * 1024
DMA_QUEUES = 2
ROW_TILES = D_MODEL // LANES

SEQ_ALIGN = 256
PROJ_TM = 256
ROW_TM = 512
SCAN_TC = 128
SCAN_SUB = SUBLANES
ATT_T = 256
ROUTE_TM = 256
MOE_BLK = 512
DISP_TM = 512
COMB_TM = 128

_SIGMA = (0, 4, 1, 5, 2, 6, 3, 7)


def _cparams(sem):
    return pltpu.CompilerParams(dimension_semantics=sem, vmem_limit_bytes=VMEM_LIMIT_BYTES)


def _dot(a, b):
    return jnp.dot(a, b, preferred_element_type=F32)


def _dot_hi(a, b):
    return jnp.dot(a, b, preferred_element_type=F32, precision=lax.Precision.HIGHEST)


def _full(shape):
    nd = len(shape)
    return pl.BlockSpec(shape, lambda *_: (0,) * nd)


def _layer_norm(y, g, b):
    mu = jnp.mean(y, axis=-1, keepdims=True)
    d = y - mu
    var = jnp.mean(d * d, axis=-1, keepdims=True)
    return d * lax.rsqrt(var + LN_EPS) * g + b


def _perm_cols():
    L = np.arange(D_MODEL)
    j, p, hh = L // LANES, (L % LANES) // 16, L % 16
    i = 8 * j + np.asarray(_SIGMA)[p]
    return (hh * RWKV_HEAD + i).astype(np.int32)


def _head_sum_mats():
    L = np.arange(D_MODEL)
    c = np.arange(LANES)
    gs = (L[:, None] % 16 == c[None, :] % 16).astype(np.float32)
    gb = ((c[:, None] < 16) & (c[:, None] == L[None, :] % 16)).astype(np.float32)
    return gs, gb


def _rwkv_proj_kernel(has_vres, *refs):
    if has_vres:
        (x_ref, xp_ref, mu_ref, wr_ref, wk_ref, wv_ref, w0_ref, wl1_ref, wl2_ref, a0_ref, al1_ref,
         al2_ref, gl1_ref, gl2_ref, kk_ref, ka_ref, gs_ref, gb_ref, vf_ref, v0_ref, vl1_ref, vl2_ref,
         r_o, w_o, k_o, v_o, a_o, b_o, g_o) = refs
    else:
        (x_ref, xp_ref, mu_ref, wr_ref, wk_ref, wv_ref, w0_ref, wl1_ref, wl2_ref, a0_ref, al1_ref,
         al2_ref, gl1_ref, gl2_ref, kk_ref, ka_ref, gs_ref, gb_ref,
         r_o, w_o, k_o, v_o, a_o, b_o, g_o) = refs
    i = pl.program_id(1)
    x = x_ref[0]
    prev = xp_ref[0][SUBLANES - 1:SUBLANES, :]
    prev = jnp.where(i == 0, 0.0, prev)
    row = lax.broadcasted_iota(I32, x.shape, 0)
    xprev = jnp.where(row == 0, prev, pltpu.roll(x, 1, 0))
    dx = xprev - x

    def mix(s):
        return (x + dx * mu_ref[s:s + 1, :]).astype(BF16)

    xv = mix(2)
    r = _dot(mix(0), wr_ref[...])
    k = _dot(mix(1), wk_ref[...])
    v = _dot(xv, wv_ref[...])
    zw = w0_ref[...] + _dot(jnp.tanh(_dot(mix(3), wl1_ref[...])).astype(BF16), wl2_ref[...])
    decay = jnp.exp(-math.exp(-0.5) * jax.nn.sigmoid(zw))
    a = jax.nn.sigmoid(a0_ref[...] + _dot(_dot(mix(4), al1_ref[...]).astype(BF16), al2_ref[...]))
    g = _dot(jax.nn.sigmoid(_dot(mix(5), gl1_ref[...])).astype(BF16), gl2_ref[...])
    if has_vres:
        gate_v = jax.nn.sigmoid(v0_ref[...] + _dot(_dot(xv, vl1_ref[...]).astype(BF16), vl2_ref[...]))
        v = v + (vf_ref[0] - v) * gate_v
    kk = k * kk_ref[...]
    ss = _dot_hi(kk * kk, gs_ref[...])
    inv = 1.0 / jnp.maximum(jnp.sqrt(ss), 1e-12)
    kk = kk * _dot_hi(inv, gb_ref[...])
    k = k * (1.0 + (a - 1.0) * ka_ref[...])
    r_o[0] = r
    w_o[0] = decay
    k_o[0] = k
    v_o[0] = v
    a_o[0] = -kk
    b_o[0] = kk * a
    g_o[0] = g


def _rwkv_proj(h, vfirst, p):
    B, Lp, D = h.shape
    TM = PROJ_TM
    has_vres = vfirst is not None
    tile = pl.BlockSpec((1, TM, D), lambda b, i: (b, i, 0))
    prev8 = pl.BlockSpec((1, SUBLANES, D), lambda b, i: (b, jnp.maximum(i * (TM // SUBLANES) - 1, 0), 0))
    names = ["mu", "wr", "wk", "wv", "w0", "wl1", "wl2", "a0", "al1", "al2", "gl1", "gl2", "kk", "ka", "gs", "gb"]
    args = [h, h] + [p[n] for n in names]
    specs = [tile, prev8] + [_full(p[n].shape) for n in names]
    if has_vres:
        args += [vfirst, p["v0"], p["vl1"], p["vl2"]]
        specs += [tile, _full(p["v0"].shape), _full(p["vl1"].shape), _full(p["vl2"].shape)]
    out = jax.ShapeDtypeStruct((B, Lp, D), F32)
    return pl.pallas_call(
        functools.partial(_rwkv_proj_kernel, has_vres),
        out_shape=[out] * 7,
        grid=(B, Lp // TM),
        in_specs=specs,
        out_specs=[tile] * 7,
        compiler_params=_cparams(("parallel", "arbitrary")),
        name="rwkv_proj",
    )(*args)


def _wkv_kernel(r_ref, w_ref, k_ref, v_ref, a_ref, b_ref, rk_ref, lg_ref, lb_ref, o_ref,
                s_ref, ma_ref, mwr_ref, mw_ref, mb_ref, mk_ref, zv_ref, br_ref, kr_ref, rkk_ref, yc_ref):
    NV = RWKV_HEAD
    NCOL = 2 * D_MODEL // LANES
    NRB = NV // SUBLANES

    @pl.when(pl.program_id(0) == 0)
    def _():
        s_ref[...] = jnp.zeros_like(s_ref)

    lane = lax.broadcasted_iota(I32, (SUBLANES, LANES), 1)
    even = ((lane // 16) % 2) == 0
    grp = lane // 32

    def cs(j):
        return slice(j * LANES, (j + 1) * LANES)

    def merge(ref, t0):
        x0 = ref[0, pl.ds(t0, SUBLANES), :]
        x1 = ref[1, pl.ds(t0, SUBLANES), :]
        cols = []
        for j in range(D_MODEL // LANES):
            a0 = x0[:, cs(j)]
            a1 = x1[:, cs(j)]
            cols.append(jnp.where(even, a0, pltpu.roll(a1, 16, 1)))
            cols.append(jnp.where(even, pltpu.roll(a0, LANES - 16, 1), a1))
        return cols

    def fold(x):
        return (x + pltpu.roll(x, 32, 1)) + (pltpu.roll(x, 64, 1) + pltpu.roll(x, 96, 1))

    def colsum(xs):
        acc = xs[0]
        for x in xs[1:]:
            acc = acc + x
        return acc

    def sub_chunk(c, carry):
        t0 = pl.multiple_of(c * SCAN_SUB, SCAN_SUB)
        R = merge(r_ref, t0)
        W = merge(w_ref, t0)
        K = merge(k_ref, t0)
        V = merge(v_ref, t0)
        A = merge(a_ref, t0)
        Bm = merge(b_ref, t0)
        per_step = pl.ds(0, SCAN_SUB, stride=SUBLANES)
        for j in range(NCOL):
            ma_ref[j, per_step, :] = A[j]
            mw_ref[j, per_step, :] = W[j]
            mwr_ref[j, per_step, :] = W[j] * R[j]
            mb_ref[j, per_step, :] = Bm[j]
            mk_ref[j, per_step, :] = K[j]
        br_ref[per_step, :] = fold(colsum([Bm[j] * R[j] for j in range(NCOL)]))
        kr_ref[per_step, :] = fold(colsum([K[j] * R[j] for j in range(NCOL)]))
        rkk_ref[per_step, :] = fold(colsum([K[j] * R[j] * rk_ref[:, cs(j)] for j in range(NCOL)]))
        for j in range(NCOL):
            rolled = [V[j]] + [pltpu.roll(V[j], 32 * m, 1) for m in range(1, 4)]
            for q in range(4):
                z = rolled[(3 - q) % 4]
                for pos in range(2, -1, -1):
                    z = jnp.where(grp == pos, rolled[(pos - q) % 4], z)
                zv_ref[pl.ds(4 * j + q, SUBLANES, stride=NV), :] = z

        def tree_sum(xs):
            while len(xs) > 1:
                xs = [xs[a] + xs[a + 1] for a in range(0, len(xs), 2)]
            return xs[0]

        def step(t, carry2):
            trow = pl.ds(pl.multiple_of(t * SUBLANES, SUBLANES), 1)
            base = pl.multiple_of(t * NV, NV)
            brt = br_ref[trow, :]
            krt = kr_ref[trow, :]
            rkt = rkk_ref[trow, :]
            vcol, y = [], []
            for i in range(NRB):
                rows = slice(i * SUBLANES, (i + 1) * SUBLANES)
                srow = [s_ref[rows, cs(j)] for j in range(NCOL)]
                sa = fold(tree_sum([srow[j] * ma_ref[j, trow, :] for j in range(NCOL)]))
                yp = fold(tree_sum([srow[j] * mwr_ref[j, trow, :] for j in range(NCOL)]))
                vc = zv_ref[pl.ds(base + i * SUBLANES, SUBLANES), :]
                for j in range(NCOL):
                    s_ref[rows, cs(j)] = (srow[j] * mw_ref[j, trow, :] + sa * mb_ref[j, trow, :]
                                          + vc * mk_ref[j, trow, :])
                vcol.append(vc)
                y.append(yp + sa * brt + vc * krt)
            mean = jnp.sum(colsum(y), axis=0, keepdims=True) * (1.0 / NV)
            d = [yi - mean for yi in y]
            var = jnp.sum(colsum([di * di for di in d]), axis=0, keepdims=True) * (1.0 / NV)
            inv = lax.rsqrt(var + RWKV_GN_EPS)
            for i in range(NRB):
                rows = slice(i * SUBLANES, (i + 1) * SUBLANES)
                out = d[i] * inv * lg_ref[rows, :] + lb_ref[rows, :] + vcol[i] * rkt
                yc_ref[pl.ds(base + i * SUBLANES, SUBLANES), :] = out
            return carry2

        lax.fori_loop(0, SCAN_SUB, step, 0)

        cols = []
        for j in range(NCOL):
            acc = None
            for q in range(4):
                z = yc_ref[pl.ds(4 * j + q, SUBLANES, stride=NV), :]
                acc = z if q == 0 else jnp.where(grp == q, z, acc)
            cols.append(acc)
        for j in range(D_MODEL // LANES):
            c0, c1 = cols[2 * j], cols[2 * j + 1]
            o_ref[0, pl.ds(t0, SUBLANES), cs(j)] = jnp.where(even, c0, pltpu.roll(c1, 16, 1))
            o_ref[1, pl.ds(t0, SUBLANES), cs(j)] = jnp.where(even, pltpu.roll(c0, LANES - 16, 1), c1)
        return carry

    lax.fori_loop(0, SCAN_TC // SCAN_SUB, sub_chunk, 0)


def _wkv_scan(r, w, k, v, a, b, rk_m, lg_t, lb_t):
    B, Lp, D = r.shape
    assert B == 2, "the scan packs exactly two batch rows into the lane dimension"
    blk = pl.BlockSpec((B, SCAN_TC, D), lambda i: (0, i, 0))
    vm = lambda *s: pltpu.VMEM(s, F32)
    return pl.pallas_call(
        _wkv_kernel,
        out_shape=jax.ShapeDtypeStruct((B, Lp, D), F32),
        grid=(Lp // SCAN_TC,),
        in_specs=[blk] * 6 + [_full(rk_m.shape), _full(lg_t.shape), _full(lb_t.shape)],
        out_specs=blk,
        scratch_shapes=[vm(RWKV_HEAD, 2 * D)] + [vm(2 * D // LANES, SCAN_SUB * SUBLANES, LANES)] * 5
        + [vm(SCAN_SUB * RWKV_HEAD, LANES)] + [vm(SCAN_SUB * SUBLANES, LANES)] * 3
        + [vm(SCAN_SUB * RWKV_HEAD, LANES)],
        compiler_params=_cparams(("arbitrary",)),
        name="wkv_scan",
    )(r, w, k, v, a, b, rk_m, lg_t, lb_t)


def _mm_res_ln_kernel(has_gate, *refs):
    if has_gate:
        z_ref, g_ref, w_ref, h_ref, lg_ref, lb_ref, o_ref = refs
        z = (z_ref[...] * g_ref[...]).astype(BF16)
    else:
        z_ref, w_ref, h_ref, lg_ref, lb_ref, o_ref = refs
        z = z_ref[...].astype(BF16)
    y = DN_ALPHA * h_ref[...] + _dot(z, w_ref[...])
    o_ref[...] = _layer_norm(y, lg_ref[...], lb_ref[...])


def _mm_res_ln(z, gate, w, h, lg, lb):
    Np, D = h.shape
    TM = ROW_TM
    tile = pl.BlockSpec((TM, D), lambda i: (i, 0))
    has_gate = gate is not None
    args = [z] + ([gate] if has_gate else []) + [w, h, lg, lb]
    specs = [tile] + ([tile] if has_gate else []) + [_full(w.shape), tile, _full(lg.shape), _full(lb.shape)]
    return pl.pallas_call(
        functools.partial(_mm_res_ln_kernel, has_gate),
        out_shape=jax.ShapeDtypeStruct((Np, D), F32),
        grid=(Np // TM,),
        in_specs=specs,
        out_specs=tile,
        compiler_params=_cparams(("parallel",)),
        name="mm_res_ln",
    )(*args)


def _proj_kernel(x_ref, w_ref, o_ref):
    o_ref[...] = _dot(x_ref[...].astype(BF16), w_ref[...]).astype(o_ref.dtype)


def _proj(x, w):
    Np, D = x.shape
    Nout = w.shape[1]
    TM = ROW_TM
    return pl.pallas_call(
        _proj_kernel,
        out_shape=jax.ShapeDtypeStruct((Np, Nout), BF16),
        grid=(Np // TM,),
        in_specs=[pl.BlockSpec((TM, D), lambda i: (i, 0)), _full(w.shape)],
        out_specs=pl.BlockSpec((TM, Nout), lambda i: (i, 0)),
        compiler_params=_cparams(("parallel",)),
        name="proj",
    )(x, w)


def _attn_kernel(lam_init, lam_ref, sg_ref, q_ref, k_ref, v_ref, o_ref,
                 sa_ref, sb_ref, m1_ref, l1_ref, acc1_ref, m2_ref, l2_ref, acc2_ref):
    T = ATT_T
    qi = pl.program_id(2)
    last = pl.num_programs(2) - 1
    q = q_ref[0]
    lane = lax.broadcasted_iota(I32, q.shape, 1)
    zero = jnp.zeros_like(q)
    q1 = jnp.where(lane < DIFF_HD, q, zero)
    q2 = jnp.where(lane >= DIFF_HD, q, zero)
    m1_ref[...] = jnp.full_like(m1_ref, -1e30)
    m2_ref[...] = jnp.full_like(m2_ref, -1e30)
    l1_ref[...] = jnp.zeros_like(l1_ref)
    l2_ref[...] = jnp.zeros_like(l2_ref)
    acc1_ref[...] = jnp.zeros_like(acc1_ref)
    acc2_ref[...] = jnp.zeros_like(acc2_ref)
    nt = (((1,), (1,)), ((), ()))
    ones = jnp.ones((T, LANES), BF16)
    rel = lax.broadcasted_iota(I32, (T, T), 1) - lax.broadcasted_iota(I32, (T, T), 0)

    def chunk_start(kc):
        return pl.multiple_of(jnp.minimum(kc, last) * T, T)

    def halves(x):
        return [x[:, c * LANES:(c + 1) * LANES] for c in range(x.shape[1] // LANES)]

    def scores(g, buf):
        for half in range(2):
            kb = k_ref[0, pl.ds(chunk_start(2 * g + half), T), :]
            buf[0, :, half * T:(half + 1) * T] = lax.dot_general(q1, kb, nt, preferred_element_type=F32)
            buf[1, :, half * T:(half + 1) * T] = lax.dot_general(q2, kb, nt, preferred_element_type=F32)

    def consume(g, buf, masked):
        vext = jnp.concatenate(
            [jnp.concatenate([v_ref[0, pl.ds(chunk_start(2 * g + half), T), :], ones], axis=1) for half in range(2)],
            axis=0)
        for si, m_ref, l_ref, acc_ref in ((0, m1_ref, l1_ref, acc1_ref), (1, m2_ref, l2_ref, acc2_ref)):
            s = buf[si]
            if masked:
                s = jnp.concatenate(
                    [jnp.where(rel <= (qi - (2 * g + half)) * T, s[:, half * T:(half + 1) * T], -jnp.inf)
                     for half in range(2)], axis=1)
            parts = halves(s)
            smax = jnp.max(jnp.maximum(jnp.maximum(parts[0], parts[1]), jnp.maximum(parts[2], parts[3])),
                           axis=1, keepdims=True)
            m_old = m_ref[...]
            m_new = jnp.maximum(m_old, smax)
            alpha = jnp.exp2(m_old - m_new)
            p = jnp.exp2(s - jnp.concatenate([m_new] * len(parts), axis=1)).astype(BF16)
            d = _dot(p, vext)
            acc_ref[...] = alpha * acc_ref[...] + d[:, :LANES]
            l_ref[...] = alpha * l_ref[...] + d[:, LANES:]
            m_ref[...] = m_new

    n_free = qi // 2
    scores(0, sa_ref)

    def pair(i, carry):
        g = 2 * i
        scores(g + 1, sb_ref)
        consume(g, sa_ref, False)
        scores(g + 2, sa_ref)
        consume(g + 1, sb_ref, False)
        return carry

    n_pairs = n_free // 2
    lax.fori_loop(0, n_pairs, pair, 0)
    g_a = 2 * n_pairs
    odd = n_free % 2 == 1

    @pl.when(odd)
    def _():
        scores(g_a + 1, sb_ref)

    consume(g_a, sa_ref, True)

    @pl.when(odd)
    def _():
        consume(g_a + 1, sb_ref, True)


    lam_v = lam_ref[...]
    lam = (jnp.exp(jnp.sum(lam_v[0:1] * lam_v[1:2], axis=1, keepdims=True))
           - jnp.exp(jnp.sum(lam_v[2:3] * lam_v[3:4], axis=1, keepdims=True)) + lam_init)
    o = acc1_ref[...] / l1_ref[...] - lam * (acc2_ref[...] / l2_ref[...])
    o = o * lax.rsqrt(jnp.mean(o * o, axis=1, keepdims=True) + 1e-5) * sg_ref[...] * (1.0 - lam_init)
    o_ref[0] = o.astype(o_ref.dtype)


def _diff_attn(q, kv, lam4, subln_g, lam_init):
    B, Lp, D = q.shape
    T = ATT_T
    H = DIFF_HEADS
    HW = 2 * DIFF_HD
    vm = lambda *s: pltpu.VMEM(s, F32)
    return pl.pallas_call(
        functools.partial(_attn_kernel, lam_init),
        out_shape=jax.ShapeDtypeStruct((B, Lp, D), BF16),
        grid=(B, H, Lp // T),
        in_specs=[_full(lam4.shape), _full(subln_g.shape),
                  pl.BlockSpec((1, T, HW), lambda b, h, i: (b, i, h)),
                  pl.BlockSpec((1, Lp, HW), lambda b, h, i: (b, 0, h)),
                  pl.BlockSpec((1, Lp, HW), lambda b, h, i: (b, 0, H + h))],
        out_specs=pl.BlockSpec((1, T, HW), lambda b, h, i: (b, i, h)),
        scratch_shapes=[vm(2, T, 2 * T), vm(2, T, 2 * T),
                        vm(T, LANES), vm(T, LANES), vm(T, HW), vm(T, LANES), vm(T, LANES), vm(T, HW)],
        compiler_params=_cparams(("parallel", "parallel", "arbitrary")),
        name="diff_attn",
    )(lam4, subln_g, q, kv, kv)


def _route_kernel(x_ref, rw_ref, rb_ref, sg_ref, su_ref, sd_ref, tri_ref,
                  sh_o, idx_o, gate_o, rank_o, cnt_o, carry_ref):
    TM = ROUTE_TM
    G, EG = N_GROUPS, N_EXPERTS // N_GROUPS

    @pl.when(pl.program_id(0) == 0)
    def _():
        carry_ref[...] = jnp.zeros_like(carry_ref)

    x = x_ref[...]
    xb = x.astype(BF16)
    hmid = _dot(xb, sg_ref[...])
    hmid = hmid * jax.nn.sigmoid(hmid) * _dot(xb, su_ref[...])
    sh_o[...] = _dot(hmid.astype(BF16), sd_ref[...])

    logit = lax.dot_general(rw_ref[...], x, (((1,), (1,)), ((), ())),
                            preferred_element_type=F32, precision=lax.Precision.HIGHEST)
    s = jax.nn.sigmoid(logit)
    s3 = s.reshape(G, EG, TM)
    sel3 = (s + rb_ref[...]).reshape(G, EG, TM)
    io_j = lax.broadcasted_iota(I32, (G, EG, TM), 1)
    io_g = lax.broadcasted_iota(I32, (G, 1, TM), 0)
    neg = -jnp.inf
    m1 = jnp.max(sel3, axis=1, keepdims=True)
    i1 = jnp.min(jnp.where(sel3 == m1, io_j, EG), axis=1, keepdims=True)
    m2 = jnp.max(jnp.where(io_j == i1, neg, sel3), axis=1, keepdims=True)
    gsc = m1 + m2
    gkeep = jnp.zeros((G, 1, TM), F32)
    for _ in range(TOPK_GROUPS):
        m = jnp.max(gsc, axis=0, keepdims=True)
        gi = jnp.min(jnp.where(gsc == m, io_g, G), axis=0, keepdims=True)
        hit = io_g == gi
        gkeep = jnp.where(hit, 1.0, gkeep)
        gsc = jnp.where(hit, neg, gsc)
    cur = jnp.where(gkeep > 0.0, sel3, neg)
    io_e = io_g * EG + io_j
    hits, idxs, ws = [], [], []
    for _ in range(TOP_K):
        m = jnp.max(jnp.max(cur, axis=1, keepdims=True), axis=0, keepdims=True)
        ei = jnp.min(jnp.min(jnp.where(cur == m, io_e, N_EXPERTS), axis=1, keepdims=True), axis=0, keepdims=True)
        hit = io_e == ei
        ws.append(jnp.sum(jnp.sum(jnp.where(hit, s3, 0.0), axis=1, keepdims=True), axis=0, keepdims=True))
        cur = jnp.where(hit, neg, cur)
        hits.append(hit)
        idxs.append(ei)
    wsum = ws[0]
    for wv in ws[1:]:
        wsum = wsum + wv
    scale = ROUTED_SCALE / wsum
    onehot = jnp.zeros((G, EG, TM), F32)
    for hit in hits:
        onehot = jnp.where(hit, 1.0, onehot)
    oh2 = onehot.reshape(N_EXPERTS, TM)
    rank_full = (_dot(oh2.astype(BF16), tri_ref[...]) + carry_ref[:, :1]).reshape(G, EG, TM)
    for kk in range(TOP_K):
        rk = jnp.sum(jnp.sum(jnp.where(hits[kk], rank_full, 0.0), axis=1, keepdims=True), axis=0, keepdims=True)
        idx_o[kk:kk + 1, :] = idxs[kk].reshape(1, TM)
        gate_o[kk:kk + 1, :] = (ws[kk] * scale).reshape(1, TM)
        rank_o[kk:kk + 1, :] = rk.reshape(1, TM).astype(I32)
    carry_ref[...] = carry_ref[...] + jnp.sum(oh2, axis=1, keepdims=True)
    cnt_o[...] = carry_ref[...]


def _route_shared(x, rw_t, rb, sg, su, sd, tri):
    Np, D = x.shape
    TM = ROUTE_TM
    tok = pl.BlockSpec((TOP_K, TM), lambda i: (0, i))
    return pl.pallas_call(
        _route_kernel,
        out_shape=[jax.ShapeDtypeStruct((Np, D), F32),
                   jax.ShapeDtypeStruct((TOP_K, Np), I32),
                   jax.ShapeDtypeStruct((TOP_K, Np), F32),
                   jax.ShapeDtypeStruct((TOP_K, Np), I32),
                   jax.ShapeDtypeStruct((N_EXPERTS, LANES), F32)],
        grid=(Np // TM,),
        in_specs=[pl.BlockSpec((TM, D), lambda i: (i, 0)), _full(rw_t.shape), _full(rb.shape),
                  _full(sg.shape), _full(su.shape), _full(sd.shape), _full(tri.shape)],
        out_specs=[pl.BlockSpec((TM, D), lambda i: (i, 0)), tok, tok, tok, _full((N_EXPERTS, LANES))],
        scratch_shapes=[pltpu.VMEM((N_EXPERTS, LANES), F32)],
        compiler_params=_cparams(("arbitrary",)),
        name="moe_route",
    )(x, rw_t, rb, sg, su, sd, tri)


def _row_copy(src_ref, src_row, dst_ref, dst_row, sem):
    return pltpu.make_async_copy(src_ref.at[pl.ds(src_row, 1)], dst_ref.at[pl.ds(dst_row, 1)], sem)


def _tile_copy(src_ref, src_row, dst_ref, dst_row, sem):
    src = src_ref.at[pl.ds(pl.multiple_of(src_row * ROW_TILES, ROW_TILES), ROW_TILES)]
    dst = dst_ref.at[pl.ds(pl.multiple_of(dst_row * ROW_TILES, ROW_TILES), ROW_TILES)]
    return pltpu.make_async_copy(src, dst, sem)


def _dispatch_kernel(ps_ref, idx_ref, rank_ref, x_ref, xs_hbm, stage_ref, sem):
    for s in range(ROW_TILES):
        stage_ref[pl.ds(s, DISP_TM, stride=ROW_TILES), :] = x_ref[:, s * LANES:(s + 1) * LANES]

    def issue(n, c):
        for kk in range(TOP_K):
            dst = ps_ref[idx_ref[kk, n]] + rank_ref[kk, n]
            _tile_copy(stage_ref, n, xs_hbm, dst, sem).start(priority=kk % DMA_QUEUES)
        return c

    lax.fori_loop(0, DISP_TM, issue, 0)

    def drain(n, c):
        for kk in range(TOP_K):
            _tile_copy(stage_ref, n, xs_hbm, 0, sem).wait()
        return c

    lax.fori_loop(0, DISP_TM, drain, 0)


def _dispatch(x, idx_t, rank_t, pad_start, n_rows):
    Np, D = x.shape
    tok = pl.BlockSpec((TOP_K, DISP_TM), lambda i, ps: (0, i), memory_space=pltpu.SMEM)
    return pl.pallas_call(
        _dispatch_kernel,
        out_shape=jax.ShapeDtypeStruct((n_rows * ROW_TILES, LANES), F32),
        grid_spec=pltpu.PrefetchScalarGridSpec(
            num_scalar_prefetch=1,
            grid=(Np // DISP_TM,),
            in_specs=[tok, tok, pl.BlockSpec((DISP_TM, D), lambda i, ps: (i, 0))],
            out_specs=pl.BlockSpec(memory_space=pl.ANY),
            scratch_shapes=[pltpu.VMEM((DISP_TM * ROW_TILES, LANES), F32), pltpu.SemaphoreType.DMA(())],
        ),
        compiler_params=_cparams(("arbitrary",)),
        name="moe_dispatch",
    )(pad_start, idx_t, rank_t, x)


def _expert_kernel(be_ref, nu_ref, xs_ref, wg_ref, wu_ref, wd_ref, ys_ref):
    @pl.when(pl.program_id(0) < nu_ref[0])
    def _():
        x = jnp.concatenate([xs_ref[pl.ds(s, MOE_BLK, stride=ROW_TILES), :] for s in range(ROW_TILES)],
                            axis=1).astype(BF16)
        g = _dot(x, wg_ref[0])
        hmid = g * jax.nn.sigmoid(g) * _dot(x, wu_ref[0])
        ys_ref[...] = _dot(hmid.astype(BF16), wd_ref[0])


def _experts(xs, block_e, n_used, wg, wu, wd):
    D, F = wg.shape[1], wg.shape[2]
    P = xs.shape[0] // ROW_TILES
    nb = P // MOE_BLK
    rows = pl.BlockSpec((MOE_BLK, D), lambda i, be, nu: (jnp.minimum(i, nu[0] - 1), 0))
    tiles = pl.BlockSpec((MOE_BLK * ROW_TILES, LANES), lambda i, be, nu: (jnp.minimum(i, nu[0] - 1), 0))
    wspec = lambda s: pl.BlockSpec((1,) + s, lambda i, be, nu: (be[jnp.minimum(i, nu[0] - 1)], 0, 0))
    return pl.pallas_call(
        _expert_kernel,
        out_shape=jax.ShapeDtypeStruct((P, D), F32),
        grid_spec=pltpu.PrefetchScalarGridSpec(
            num_scalar_prefetch=2,
            grid=(nb,),
            in_specs=[tiles, wspec((D, F)), wspec((D, F)), wspec((F, D))],
            out_specs=rows,
        ),
        compiler_params=_cparams(("arbitrary",)),
        name="moe_experts",
    )(block_e, n_used, xs, wg, wu, wd)


def _combine_kernel(ps_ref, idx_ref, rank_ref, gate_ref, sh_ref, h_ref, lg_ref, lb_ref, ys_hbm, o_ref,
                    buf, sem):
    def issue(n, c):
        for kk in range(TOP_K):
            src = ps_ref[idx_ref[kk, n]] + rank_ref[kk, n]
            _row_copy(ys_hbm, src, buf.at[kk], n, sem).start(priority=kk % DMA_QUEUES)
        return c

    lax.fori_loop(0, COMB_TM, issue, 0)

    def drain(n, c):
        for kk in range(TOP_K):
            _row_copy(ys_hbm, 0, buf.at[kk], n, sem).wait()
        return c

    lax.fori_loop(0, COMB_TM, drain, 0)
    gate = gate_ref[...]
    ffn = sh_ref[...]
    for kk in range(TOP_K):
        ffn = ffn + gate[:, kk:kk + 1] * buf[kk]
    o_ref[...] = _layer_norm(DN_ALPHA * h_ref[...] + ffn, lg_ref[...], lb_ref[...])


def _combine(ys, idx_t, rank_t, pad_start, gate, shared, h, lg, lb):
    Np, D = h.shape
    TM = COMB_TM
    tok = pl.BlockSpec((TOP_K, TM), lambda i, ps: (0, i), memory_space=pltpu.SMEM)
    tile = pl.BlockSpec((TM, D), lambda i, ps: (i, 0))
    vec = pl.BlockSpec((1, D), lambda i, ps: (0, 0))
    return pl.pallas_call(
        _combine_kernel,
        out_shape=jax.ShapeDtypeStruct((Np, D), F32),
        grid_spec=pltpu.PrefetchScalarGridSpec(
            num_scalar_prefetch=1,
            grid=(Np // TM,),
            in_specs=[tok, tok, pl.BlockSpec((TM, TOP_K), lambda i, ps: (i, 0)), tile, tile, vec, vec,
                      pl.BlockSpec(memory_space=pl.ANY)],
            out_specs=tile,
            scratch_shapes=[pltpu.VMEM((TOP_K, TM, D), F32), pltpu.SemaphoreType.DMA(())],
        ),
        compiler_params=_cparams(("arbitrary",)),
        name="moe_combine",
    )(pad_start, idx_t, rank_t, gate, shared, h, lg, lb, ys)


def _moe_layer(h2, l, lg, lb, router_w, router_b, w_gate, w_up, w_down, sh_gate, sh_up, sh_down, tri):
    Np, D = h2.shape
    shared, idx_t, gate_t, rank_t, cnt = _route_shared(
        h2, router_w[l].T, router_b[l].reshape(N_EXPERTS, 1),
        sh_gate[l].astype(BF16), sh_up[l].astype(BF16), sh_down[l].astype(BF16), tri)
    counts = cnt[:, 0].astype(I32)
    padded = (counts + MOE_BLK - 1) // MOE_BLK * MOE_BLK
    pad_end = jnp.cumsum(padded)
    pad_start = (pad_end - padded).astype(I32)
    n_blocks = Np * TOP_K // MOE_BLK + N_EXPERTS
    n_used = (pad_end[-1:] // MOE_BLK).astype(I32)
    blk_row0 = jnp.arange(n_blocks, dtype=I32) * MOE_BLK
    block_e = jnp.minimum(jnp.sum((pad_end[None, :] <= blk_row0[:, None]).astype(I32), axis=1), N_EXPERTS - 1)
    xs = _dispatch(h2, idx_t, rank_t, pad_start, n_blocks * MOE_BLK)
    ys = _experts(xs, block_e, n_used, w_gate[l].astype(BF16), w_up[l].astype(BF16), w_down[l].astype(BF16))
    return _combine(ys, idx_t, rank_t, pad_start, gate_t.T, shared, h2, lg, lb)


def _pad_cols(w, n):
    return jnp.pad(w, ((0, 0), (0, n - w.shape[1])))


def _pad_rows(w, n):
    return jnp.pad(w, ((0, n - w.shape[0]), (0, 0)))


def _trunk(x, meta_tokens, ln_mix_g, ln_mix_b, ln_ffn_g, ln_ffn_b,
           rw_mu, rw_w_rkv, rw_w0, rw_w_l1, rw_w_l2, rw_a0, rw_a_l1, rw_a_l2,
           rw_g_l1, rw_g_l2, rw_k_k, rw_k_a, rw_r_k, rw_lnx_g, rw_lnx_b, rw_w_out,
           rw_v0, rw_v_l1, rw_v_l2, kv_w,
           da_w_q, da_lam_q1, da_lam_k1, da_lam_q2, da_lam_k2, da_subln_g, da_w_out,
           moe_router_w, moe_router_b, moe_w_gate, moe_w_up, moe_w_down,
           moe_sh_gate, moe_sh_up, moe_sh_down):
    B, S, D = x.shape
    L = S + N_META
    Lp = -(-L // SEQ_ALIGN) * SEQ_ALIGN
    Np = B * Lp
    assert D == D_MODEL and Np % ROW_TM == 0 and Lp % PROJ_TM == 0 and Lp % SCAN_TC == 0 and Lp % ATT_T == 0
    meta = jnp.broadcast_to(meta_tokens[None].astype(x.dtype), (B, N_META, D))
    h = jnp.concatenate([meta, x, jnp.zeros((B, Lp - L, D), x.dtype)], axis=1)

    pc = _perm_cols()
    gs_np, gb_np = _head_sum_mats()
    gs, gb = jnp.asarray(gs_np), jnp.asarray(gb_np)
    tri = jnp.asarray(np.triu(np.ones((ROUTE_TM, ROUTE_TM), np.float32), 1)).astype(BF16)
    row = lambda vec: vec.reshape(1, -1)
    lane = np.arange(2 * D_MODEL)
    m_key, m_head = lane // 32, lane % 16
    c128 = np.arange(LANES) % 16
    vrow = np.arange(RWKV_HEAD)

    v_first = None
    kv = None
    for l in range(DEPTH):
        if l < N_A_LAYERS:
            p = {
                "mu": _pad_rows(rw_mu[l], SUBLANES),
                "wr": rw_w_rkv[l, 0][:, pc].astype(BF16),
                "wk": rw_w_rkv[l, 1][:, pc].astype(BF16),
                "wv": rw_w_rkv[l, 2][:, pc].astype(BF16),
                "w0": row(rw_w0[l][pc]),
                "wl1": _pad_cols(rw_w_l1[l], LANES).astype(BF16),
                "wl2": _pad_rows(rw_w_l2[l][:, pc], LANES).astype(BF16),
                "a0": row(rw_a0[l][pc]),
                "al1": _pad_cols(rw_a_l1[l], LANES).astype(BF16),
                "al2": _pad_rows(rw_a_l2[l][:, pc], LANES).astype(BF16),
                "gl1": _pad_cols(rw_g_l1[l], 2 * LANES).astype(BF16),
                "gl2": _pad_rows(rw_g_l2[l][:, pc], 2 * LANES).astype(BF16),
                "kk": row(rw_k_k[l][pc]),
                "ka": row(rw_k_a[l][pc]),
                "gs": gs,
                "gb": gb,
            }
            if l > 0:
                p["v0"] = row(rw_v0[l - 1][pc])
                p["vl1"] = _pad_cols(rw_v_l1[l - 1], LANES).astype(BF16)
                p["vl2"] = _pad_rows(rw_v_l2[l - 1][:, pc], LANES).astype(BF16)
            r, w, k, v, a, b, g = _rwkv_proj(h, v_first if l > 0 else None, p)
            if l == 0:
                v_first = v
            rk_m = rw_r_k[l][m_head, m_key].reshape(1, 2 * D_MODEL)
            lg_t = rw_lnx_g[l].reshape(RWKV_HEADS, RWKV_HEAD)[c128[None, :], vrow[:, None]]
            lb_t = rw_lnx_b[l].reshape(RWKV_HEADS, RWKV_HEAD)[c128[None, :], vrow[:, None]]
            z = _wkv_scan(r, w, k, v, a, b, rk_m, lg_t, lb_t)
            h2 = _mm_res_ln(z.reshape(Np, D), g.reshape(Np, D), rw_w_out[l][pc, :].astype(BF16),
                            h.reshape(Np, D), row(ln_mix_g[l]), row(ln_mix_b[l]))
        else:
            j = l - N_A_LAYERS
            h2 = h.reshape(Np, D)
            if kv is None:
                kv = _proj(h2, kv_w.astype(BF16)).reshape(B, Lp, 2 * D)
            q_scale = DIFF_HD ** -0.5 * math.log2(math.e)
            q = _proj(h2, (da_w_q[j] * q_scale).astype(BF16)).reshape(B, Lp, D)
            lam_init = 0.8 - 0.6 * math.exp(-0.3 * l)
            lam4 = jnp.stack([da_lam_q1[j], da_lam_k1[j], da_lam_q2[j], da_lam_k2[j]])
            o = _diff_attn(q, kv, _pad_rows(lam4, SUBLANES), row(da_subln_g[j]), lam_init)
            h2 = _mm_res_ln(o.reshape(Np, D), None, da_w_out[j].astype(BF16), h2,
                            row(ln_mix_g[l]), row(ln_mix_b[l]))
        h2 = _moe_layer(h2, l, row(ln_ffn_g[l]), row(ln_ffn_b[l]), moe_router_w, moe_router_b,
                        moe_w_gate, moe_w_up, moe_w_down, moe_sh_gate, moe_sh_up, moe_sh_down, tri)
        h = h2.reshape(B, Lp, D)
    return h[:, N_META:L]


_trunk_jit = jax.jit(_trunk)


def kernel(x, meta_tokens, ln_mix_g, ln_mix_b, ln_ffn_g, ln_ffn_b, rw_mu, rw_w_rkv, rw_w0, rw_w_l1, rw_w_l2, rw_a0, rw_a_l1, rw_a_l2, rw_g_l1, rw_g_l2, rw_k_k, rw_k_a, rw_r_k, rw_lnx_g, rw_lnx_b, rw_w_out, rw_v0, rw_v_l1, rw_v_l2, kv_w, da_w_q, da_lam_q1, da_lam_k1, da_lam_q2, da_lam_k2, da_subln_g, da_w_out, moe_router_w, moe_router_b, moe_w_gate, moe_w_up, moe_w_down, moe_sh_gate, moe_sh_up, moe_sh_down):
    return _trunk_jit(x, meta_tokens, ln_mix_g, ln_mix_b, ln_ffn_g, ln_ffn_b, rw_mu, rw_w_rkv, rw_w0, rw_w_l1,
                      rw_w_l2, rw_a0, rw_a_l1, rw_a_l2, rw_g_l1, rw_g_l2, rw_k_k, rw_k_a, rw_r_k, rw_lnx_g,
                      rw_lnx_b, rw_w_out, rw_v0, rw_v_l1, rw_v_l2, kv_w, da_w_q, da_lam_q1, da_lam_k1,
                      da_lam_q2, da_lam_k2, da_subln_g, da_w_out, moe_router_w, moe_router_b, moe_w_gate,
                      moe_w_up, moe_w_down, moe_sh_gate, moe_sh_up, moe_sh_down)
```

```python
import functools
import math

import numpy as np
import jax
import jax.numpy as jnp
from jax import lax
from jax.experimental import pallas as pl
from jax.experimental.pallas import tpu as pltpu

F32 = jnp.float32
BF16 = jnp.bfloat16
I32 = jnp.int32

D_MODEL = 1024
DEPTH = 4
N_META = 16
N_A_LAYERS = DEPTH // 2
RWKV_HEAD = 64
RWKV_HEADS = D_MODEL // RWKV_HEAD
RWKV_GN_EPS = 64e-5
DIFF_HEADS = 8
DIFF_HD = D_MODEL // (2 * DIFF_HEADS)
N_EXPERTS = 64
TOP_K = 8
N_GROUPS = 8
TOPK_GROUPS = 4
EXPERT_FF = D_MODEL // 4
ROUTED_SCALE = 2.5
DN_ALPHA = (2 * DEPTH) ** 0.25
LN_EPS = 1e-5

LANES = 128
SUBLANES = 8
VMEM_LIMIT_BYTES = 56 * 1024 * 1024
DMA_QUEUES = 2
ROW_TILES = D_MODEL // LANES

SEQ_ALIGN = 256
PROJ_TM = 256
ROW_TM = 512
SCAN_TC = 128
SCAN_SUB = SUBLANES
ATT_T = 256
ROUTE_TM = 256
MOE_BLK = 512
DISP_TM = 512
COMB_TM = 256

_SIGMA = (0, 4, 1, 5, 2, 6, 3, 7)


def _cparams(sem):
    return pltpu.CompilerParams(dimension_semantics=sem, vmem_limit_bytes=VMEM_LIMIT_BYTES)


def _dot(a, b):
    return jnp.dot(a, b, preferred_element_type=F32)


def _dot_hi(a, b):
    return jnp.dot(a, b, preferred_element_type=F32, precision=lax.Precision.HIGHEST)


def _full(shape):
    nd = len(shape)
    return pl.BlockSpec(shape, lambda *_: (0,) * nd)


def _layer_norm(y, g, b):
    mu = jnp.mean(y, axis=-1, keepdims=True)
    d = y - mu
    var = jnp.mean(d * d, axis=-1, keepdims=True)
    return d * lax.rsqrt(var + LN_EPS) * g + b


def _perm_cols():
    L = np.arange(D_MODEL)
    j, p, hh = L // LANES, (L % LANES) // 16, L % 16
    i = 8 * j + np.asarray(_SIGMA)[p]
    return (hh * RWKV_HEAD + i).astype(np.int32)


def _head_sum_mats():
    L = np.arange(D_MODEL)
    c = np.arange(LANES)
    gs = (L[:, None] % 16 == c[None, :] % 16).astype(np.float32)
    gb = ((c[:, None] < 16) & (c[:, None] == L[None, :] % 16)).astype(np.float32)
    return gs, gb


def _rwkv_proj_kernel(has_vres, *refs):
    if has_vres:
        (x_ref, xp_ref, mu_ref, wr_ref, wk_ref, wv_ref, w0_ref, wl1_ref, wl2_ref, a0_ref, al1_ref,
         al2_ref, gl1_ref, gl2_ref, kk_ref, ka_ref, gs_ref, gb_ref, vf_ref, v0_ref, vl1_ref, vl2_ref,
         r_o, w_o, k_o, v_o, a_o, b_o, g_o) = refs
    else:
        (x_ref, xp_ref, mu_ref, wr_ref, wk_ref, wv_ref, w0_ref, wl1_ref, wl2_ref, a0_ref, al1_ref,
         al2_ref, gl1_ref, gl2_ref, kk_ref, ka_ref, gs_ref, gb_ref,
         r_o, w_o, k_o, v_o, a_o, b_o, g_o) = refs
    i = pl.program_id(1)
    x = x_ref[0]
    prev = xp_ref[0][SUBLANES - 1:SUBLANES, :]
    prev = jnp.where(i == 0, 0.0, prev)
    row = lax.broadcasted_iota(I32, x.shape, 0)
    xprev = jnp.where(row == 0, prev, pltpu.roll(x, 1, 0))
    dx = xprev - x

    def mix(s):
        return (x + dx * mu_ref[s:s + 1, :]).astype(BF16)

    xv = mix(2)
    r = _dot(mix(0), wr_ref[...])
    k = _dot(mix(1), wk_ref[...])
    v = _dot(xv, wv_ref[...])
    zw = w0_ref[...] + _dot(jnp.tanh(_dot(mix(3), wl1_ref[...])).astype(BF16), wl2_ref[...])
    decay = jnp.exp(-math.exp(-0.5) * jax.nn.sigmoid(zw))
    a = jax.nn.sigmoid(a0_ref[...] + _dot(_dot(mix(4), al1_ref[...]).astype(BF16), al2_ref[...]))
    g = _dot(jax.nn.sigmoid(_dot(mix(5), gl1_ref[...])).astype(BF16), gl2_ref[...])
    if has_vres:
        gate_v = jax.nn.sigmoid(v0_ref[...] + _dot(_dot(xv, vl1_ref[...]).astype(BF16), vl2_ref[...]))
        v = v + (vf_ref[0] - v) * gate_v
    kk = k * kk_ref[...]
    ss = _dot_hi(kk * kk, gs_ref[...])
    inv = 1.0 / jnp.maximum(jnp.sqrt(ss), 1e-12)
    kk = kk * _dot_hi(inv, gb_ref[...])
    k = k * (1.0 + (a - 1.0) * ka_ref[...])
    r_o[0] = r
    w_o[0] = decay
    k_o[0] = k
    v_o[0] = v
    a_o[0] = -kk
    b_o[0] = kk * a
    g_o[0] = g


def _rwkv_proj(h, vfirst, p):
    B, Lp, D = h.shape
    TM = PROJ_TM
    has_vres = vfirst is not None
    tile = pl.BlockSpec((1, TM, D), lambda b, i: (b, i, 0))
    prev8 = pl.BlockSpec((1, SUBLANES, D), lambda b, i: (b, jnp.maximum(i * (TM // SUBLANES) - 1, 0), 0))
    names = ["mu", "wr", "wk", "wv", "w0", "wl1", "wl2", "a0", "al1", "al2", "gl1", "gl2", "kk", "ka", "gs", "gb"]
    args = [h, h] + [p[n] for n in names]
    specs = [tile, prev8] + [_full(p[n].shape) for n in names]
    if has_vres:
        args += [vfirst, p["v0"], p["vl1"], p["vl2"]]
        specs += [tile, _full(p["v0"].shape), _full(p["vl1"].shape), _full(p["vl2"].shape)]
    out = jax.ShapeDtypeStruct((B, Lp, D), F32)
    return pl.pallas_call(
        functools.partial(_rwkv_proj_kernel, has_vres),
        out_shape=[out] * 7,
        grid=(B, Lp // TM),
        in_specs=specs,
        out_specs=[tile] * 7,
        compiler_params=_cparams(("parallel", "arbitrary")),
        name="rwkv_proj",
    )(*args)


def _wkv_kernel(r_ref, w_ref, k_ref, v_ref, a_ref, b_ref, rk_ref, lg_ref, lb_ref, o_ref,
                s_ref, ma_ref, mwr_ref, mw_ref, mb_ref, mk_ref, zv_ref, br_ref, kr_ref, rkk_ref, yc_ref):
    NV = RWKV_HEAD
    NCOL = 2 * D_MODEL // LANES
    NRB = NV // SUBLANES

    @pl.when(pl.program_id(0) == 0)
    def _():
        s_ref[...] = jnp.zeros_like(s_ref)

    lane = lax.broadcasted_iota(I32, (SUBLANES, LANES), 1)
    even = ((lane // 16) % 2) == 0
    grp = lane // 32

    def cs(j):
        return slice(j * LANES, (j + 1) * LANES)

    def merge(ref, t0):
        x0 = ref[0, pl.ds(t0, SUBLANES), :]
        x1 = ref[1, pl.ds(t0, SUBLANES), :]
        cols = []
        for j in range(D_MODEL // LANES):
            a0 = x0[:, cs(j)]
            a1 = x1[:, cs(j)]
            cols.append(jnp.where(even, a0, pltpu.roll(a1, 16, 1)))
            cols.append(jnp.where(even, pltpu.roll(a0, LANES - 16, 1), a1))
        return cols

    def fold(x):
        return (x + pltpu.roll(x, 32, 1)) + (pltpu.roll(x, 64, 1) + pltpu.roll(x, 96, 1))

    def colsum(xs):
        acc = xs[0]
        for x in xs[1:]:
            acc = acc + x
        return acc

    def sub_chunk(c, carry):
        t0 = pl.multiple_of(c * SCAN_SUB, SCAN_SUB)
        R = merge(r_ref, t0)
        W = merge(w_ref, t0)
        K = merge(k_ref, t0)
        V = merge(v_ref, t0)
        A = merge(a_ref, t0)
        Bm = merge(b_ref, t0)
        per_step = pl.ds(0, SCAN_SUB, stride=SUBLANES)
        for j in range(NCOL):
            ma_ref[j, per_step, :] = A[j]
            mw_ref[j, per_step, :] = W[j]
            mwr_ref[j, per_step, :] = W[j] * R[j]
            mb_ref[j, per_step, :] = Bm[j]
            mk_ref[j, per_step, :] = K[j]
        br_ref[per_step, :] = fold(colsum([Bm[j] * R[j] for j in range(NCOL)]))
        kr_ref[per_step, :] = fold(colsum([K[j] * R[j] for j in range(NCOL)]))
        rkk_ref[per_step, :] = fold(colsum([K[j] * R[j] * rk_ref[:, cs(j)] for j in range(NCOL)]))
        for j in range(NCOL):
            rolled = [V[j]] + [pltpu.roll(V[j], 32 * m, 1) for m in range(1, 4)]
            for q in range(4):
                z = rolled[(3 - q) % 4]
                for pos in range(2, -1, -1):
                    z = jnp.where(grp == pos, rolled[(pos - q) % 4], z)
                zv_ref[pl.ds(4 * j + q, SUBLANES, stride=NV), :] = z

        def tree_sum(xs):
            while len(xs) > 1:
                xs = [xs[a] + xs[a + 1] for a in range(0, len(xs), 2)]
            return xs[0]

        def step(t, carry2):
            trow = pl.ds(pl.multiple_of(t * SUBLANES, SUBLANES), 1)
            base = pl.multiple_of(t * NV, NV)
            brt = br_ref[trow, :]
            krt = kr_ref[trow, :]
            rkt = rkk_ref[trow, :]
            vcol, y = [], []
            for i in range(NRB):
                rows = slice(i * SUBLANES, (i + 1) * SUBLANES)
                srow = [s_ref[rows, cs(j)] for j in range(NCOL)]
                sa = fold(tree_sum([srow[j] * ma_ref[j, trow, :] for j in range(NCOL)]))
                yp = fold(tree_sum([srow[j] * mwr_ref[j, trow, :] for j in range(NCOL)]))
                vc = zv_ref[pl.ds(base + i * SUBLANES, SUBLANES), :]
                for j in range(NCOL):
                    s_ref[rows, cs(j)] = (srow[j] * mw_ref[j, trow, :] + sa * mb_ref[j, trow, :]
                                          + vc * mk_ref[j, trow, :])
                vcol.append(vc)
                y.append(yp + sa * brt + vc * krt)
            mean = jnp.sum(colsum(y), axis=0, keepdims=True) * (1.0 / NV)
            d = [yi - mean for yi in y]
            var = jnp.sum(colsum([di * di for di in d]), axis=0, keepdims=True) * (1.0 / NV)
            inv = lax.rsqrt(var + RWKV_GN_EPS)
            for i in range(NRB):
                rows = slice(i * SUBLANES, (i + 1) * SUBLANES)
                out = d[i] * inv * lg_ref[rows, :] + lb_ref[rows, :] + vcol[i] * rkt
                yc_ref[pl.ds(base + i * SUBLANES, SUBLANES), :] = out
            return carry2

        lax.fori_loop(0, SCAN_SUB, step, 0)

        cols = []
        for j in range(NCOL):
            acc = None
            for q in range(4):
                z = yc_ref[pl.ds(4 * j + q, SUBLANES, stride=NV), :]
                acc = z if q == 0 else jnp.where(grp == q, z, acc)
            cols.append(acc)
        for j in range(D_MODEL // LANES):
            c0, c1 = cols[2 * j], cols[2 * j + 1]
            o_ref[0, pl.ds(t0, SUBLANES), cs(j)] = jnp.where(even, c0, pltpu.roll(c1, 16, 1))
            o_ref[1, pl.ds(t0, SUBLANES), cs(j)] = jnp.where(even, pltpu.roll(c0, LANES - 16, 1), c1)
        return carry

    lax.fori_loop(0, SCAN_TC // SCAN_SUB, sub_chunk, 0)


def _wkv_scan(r, w, k, v, a, b, rk_m, lg_t, lb_t):
    B, Lp, D = r.shape
    assert B == 2, "the scan packs exactly two batch rows into the lane dimension"
    blk = pl.BlockSpec((B, SCAN_TC, D), lambda i: (0, i, 0))
    vm = lambda *s: pltpu.VMEM(s, F32)
    return pl.pallas_call(
        _wkv_kernel,
        out_shape=jax.ShapeDtypeStruct((B, Lp, D), F32),
        grid=(Lp // SCAN_TC,),
        in_specs=[blk] * 6 + [_full(rk_m.shape), _full(lg_t.shape), _full(lb_t.shape)],
        out_specs=blk,
        scratch_shapes=[vm(RWKV_HEAD, 2 * D)] + [vm(2 * D // LANES, SCAN_SUB * SUBLANES, LANES)] * 5
        + [vm(SCAN_SUB * RWKV_HEAD, LANES)] + [vm(SCAN_SUB * SUBLANES, LANES)] * 3
        + [vm(SCAN_SUB * RWKV_HEAD, LANES)],
        compiler_params=_cparams(("arbitrary",)),
        name="wkv_scan",
    )(r, w, k, v, a, b, rk_m, lg_t, lb_t)


def _mm_res_ln_kernel(has_gate, *refs):
    if has_gate:
        z_ref, g_ref, w_ref, h_ref, lg_ref, lb_ref, o_ref = refs
        z = (z_ref[...] * g_ref[...]).astype(BF16)
    else:
        z_ref, w_ref, h_ref, lg_ref, lb_ref, o_ref = refs
        z = z_ref[...].astype(BF16)
    y = DN_ALPHA * h_ref[...] + _dot(z, w_ref[...])
    o_ref[...] = _layer_norm(y, lg_ref[...], lb_ref[...])


def _mm_res_ln(z, gate, w, h, lg, lb):
    Np, D = h.shape
    TM = ROW_TM
    tile = pl.BlockSpec((TM, D), lambda i: (i, 0))
    has_gate = gate is not None
    args = [z] + ([gate] if has_gate else []) + [w, h, lg, lb]
    specs = [tile] + ([tile] if has_gate else []) + [_full(w.shape), tile, _full(lg.shape), _full(lb.shape)]
    return pl.pallas_call(
        functools.partial(_mm_res_ln_kernel, has_gate),
        out_shape=jax.ShapeDtypeStruct((Np, D), F32),
        grid=(Np // TM,),
        in_specs=specs,
        out_specs=tile,
        compiler_params=_cparams(("parallel",)),
        name="mm_res_ln",
    )(*args)


def _proj_kernel(x_ref, w_ref, o_ref):
    o_ref[...] = _dot(x_ref[...].astype(BF16), w_ref[...]).astype(o_ref.dtype)


def _proj(x, w):
    Np, D = x.shape
    Nout = w.shape[1]
    TM = ROW_TM
    return pl.pallas_call(
        _proj_kernel,
        out_shape=jax.ShapeDtypeStruct((Np, Nout), BF16),
        grid=(Np // TM,),
        in_specs=[pl.BlockSpec((TM, D), lambda i: (i, 0)), _full(w.shape)],
        out_specs=pl.BlockSpec((TM, Nout), lambda i: (i, 0)),
        compiler_params=_cparams(("parallel",)),
        name="proj",
    )(x, w)


def _attn_kernel(lam_init, lam_ref, sg_ref, q_ref, k_ref, v_ref, o_ref,
                 sa_ref, sb_ref, m1_ref, l1_ref, acc1_ref, m2_ref, l2_ref, acc2_ref):
    T = ATT_T
    qi = pl.program_id(2)
    last = pl.num_programs(2) - 1
    q = q_ref[0]
    lane = lax.broadcasted_iota(I32, q.shape, 1)
    zero = jnp.zeros_like(q)
    q1 = jnp.where(lane < DIFF_HD, q, zero)
    q2 = jnp.where(lane >= DIFF_HD, q, zero)
    m1_ref[...] = jnp.full_like(m1_ref, -1e30)
    m2_ref[...] = jnp.full_like(m2_ref, -1e30)
    l1_ref[...] = jnp.zeros_like(l1_ref)
    l2_ref[...] = jnp.zeros_like(l2_ref)
    acc1_ref[...] = jnp.zeros_like(acc1_ref)
    acc2_ref[...] = jnp.zeros_like(acc2_ref)
    nt = (((1,), (1,)), ((), ()))
    ones = jnp.ones((T, LANES), BF16)
    rel = lax.broadcasted_iota(I32, (T, T), 1) - lax.broadcasted_iota(I32, (T, T), 0)

    def chunk_start(kc):
        return pl.multiple_of(jnp.minimum(kc, last) * T, T)

    def halves(x):
        return [x[:, c * LANES:(c + 1) * LANES] for c in range(x.shape[1] // LANES)]

    def scores(g, buf):
        for half in range(2):
            kb = k_ref[0, pl.ds(chunk_start(2 * g + half), T), :]
            buf[0, :, half * T:(half + 1) * T] = lax.dot_general(q1, kb, nt, preferred_element_type=F32)
            buf[1, :, half * T:(half + 1) * T] = lax.dot_general(q2, kb, nt, preferred_element_type=F32)

    def consume(g, buf, masked):
        vext = jnp.concatenate(
            [jnp.concatenate([v_ref[0, pl.ds(chunk_start(2 * g + half), T), :], ones], axis=1) for half in range(2)],
            axis=0)
        for si, m_ref, l_ref, acc_ref in ((0, m1_ref, l1_ref, acc1_ref), (1, m2_ref, l2_ref, acc2_ref)):
            s = buf[si]
            if masked:
                s = jnp.concatenate(
                    [jnp.where(rel <= (qi - (2 * g + half)) * T, s[:, half * T:(half + 1) * T], -jnp.inf)
                     for half in range(2)], axis=1)
            parts = halves(s)
            smax = jnp.max(jnp.maximum(jnp.maximum(parts[0], parts[1]), jnp.maximum(parts[2], parts[3])),
                           axis=1, keepdims=True)
            m_old = m_ref[...]
            m_new = jnp.maximum(m_old, smax)
            alpha = jnp.exp2(m_old - m_new)
            p = jnp.exp2(s - jnp.concatenate([m_new] * len(parts), axis=1)).astype(BF16)
            d = _dot(p, vext)
            acc_ref[...] = alpha * acc_ref[...] + d[:, :LANES]
            l_ref[...] = alpha * l_ref[...] + d[:, LANES:]
            m_ref[...] = m_new

    n_free = qi // 2
    scores(0, sa_ref)

    def pair(i, carry):
        g = 2 * i
        scores(g + 1, sb_ref)
        consume(g, sa_ref, False)
        scores(g + 2, sa_ref)
        consume(g + 1, sb_ref, False)
        return carry

    n_pairs = n_free // 2
    lax.fori_loop(0, n_pairs, pair, 0)
    g_a = 2 * n_pairs
    odd = n_free % 2 == 1

    @pl.when(odd)
    def _():
        scores(g_a + 1, sb_ref)

    consume(g_a, sa_ref, True)

    @pl.when(odd)
    def _():
        consume(g_a + 1, sb_ref, True)


    lam_v = lam_ref[...]
    lam = (jnp.exp(jnp.sum(lam_v[0:1] * lam_v[1:2], axis=1, keepdims=True))
           - jnp.exp(jnp.sum(lam_v[2:3] * lam_v[3:4], axis=1, keepdims=True)) + lam_init)
    o = acc1_ref[...] / l1_ref[...] - lam * (acc2_ref[...] / l2_ref[...])
    o = o * lax.rsqrt(jnp.mean(o * o, axis=1, keepdims=True) + 1e-5) * sg_ref[...] * (1.0 - lam_init)
    o_ref[0] = o.astype(o_ref.dtype)


def _diff_attn(q, kv, lam4, subln_g, lam_init):
    B, Lp, D = q.shape
    T = ATT_T
    H = DIFF_HEADS
    HW = 2 * DIFF_HD
    vm = lambda *s: pltpu.VMEM(s, F32)
    return pl.pallas_call(
        functools.partial(_attn_kernel, lam_init),
        out_shape=jax.ShapeDtypeStruct((B, Lp, D), BF16),
        grid=(B, H, Lp // T),
        in_specs=[_full(lam4.shape), _full(subln_g.shape),
                  pl.BlockSpec((1, T, HW), lambda b, h, i: (b, i, h)),
                  pl.BlockSpec((1, Lp, HW), lambda b, h, i: (b, 0, h)),
                  pl.BlockSpec((1, Lp, HW), lambda b, h, i: (b, 0, H + h))],
        out_specs=pl.BlockSpec((1, T, HW), lambda b, h, i: (b, i, h)),
        scratch_shapes=[vm(2, T, 2 * T), vm(2, T, 2 * T),
                        vm(T, LANES), vm(T, LANES), vm(T, HW), vm(T, LANES), vm(T, LANES), vm(T, HW)],
        compiler_params=_cparams(("parallel", "parallel", "arbitrary")),
        name="diff_attn",
    )(lam4, subln_g, q, kv, kv)


def _route_kernel(x_ref, rw_ref, rb_ref, sg_ref, su_ref, sd_ref, tri_ref,
                  sh_o, idx_o, gate_o, rank_o, cnt_o, carry_ref):
    TM = ROUTE_TM
    G, EG = N_GROUPS, N_EXPERTS // N_GROUPS

    @pl.when(pl.program_id(0) == 0)
    def _():
        carry_ref[...] = jnp.zeros_like(carry_ref)

    x = x_ref[...]
    xb = x.astype(BF16)
    hmid = _dot(xb, sg_ref[...])
    hmid = hmid * jax.nn.sigmoid(hmid) * _dot(xb, su_ref[...])
    sh_o[...] = _dot(hmid.astype(BF16), sd_ref[...])

    logit = lax.dot_general(rw_ref[...], x, (((1,), (1,)), ((), ())),
                            preferred_element_type=F32, precision=lax.Precision.HIGHEST)
    s = jax.nn.sigmoid(logit)
    s3 = s.reshape(G, EG, TM)
    sel3 = (s + rb_ref[...]).reshape(G, EG, TM)
    io_j = lax.broadcasted_iota(I32, (G, EG, TM), 1)
    io_g = lax.broadcasted_iota(I32, (G, 1, TM), 0)
    neg = -jnp.inf
    m1 = jnp.max(sel3, axis=1, keepdims=True)
    i1 = jnp.min(jnp.where(sel3 == m1, io_j, EG), axis=1, keepdims=True)
    m2 = jnp.max(jnp.where(io_j == i1, neg, sel3), axis=1, keepdims=True)
    gsc = m1 + m2
    gkeep = jnp.zeros((G, 1, TM), F32)
    for _ in range(TOPK_GROUPS):
        m = jnp.max(gsc, axis=0, keepdims=True)
        gi = jnp.min(jnp.where(gsc == m, io_g, G), axis=0, keepdims=True)
        hit = io_g == gi
        gkeep = jnp.where(hit, 1.0, gkeep)
        gsc = jnp.where(hit, neg, gsc)
    cur = jnp.where(gkeep > 0.0, sel3, neg)
    io_e = io_g * EG + io_j
    hits, idxs, ws = [], [], []
    for _ in range(TOP_K):
        m = jnp.max(jnp.max(cur, axis=1, keepdims=True), axis=0, keepdims=True)
        ei = jnp.min(jnp.min(jnp.where(cur == m, io_e, N_EXPERTS), axis=1, keepdims=True), axis=0, keepdims=True)
        hit = io_e == ei
        ws.append(jnp.sum(jnp.sum(jnp.where(hit, s3, 0.0), axis=1, keepdims=True), axis=0, keepdims=True))
        cur = jnp.where(hit, neg, cur)
        hits.append(hit)
        idxs.append(ei)
    wsum = ws[0]
    for wv in ws[1:]:
        wsum = wsum + wv
    scale = ROUTED_SCALE / wsum
    onehot = jnp.zeros((G, EG, TM), F32)
    for hit in hits:
        onehot = jnp.where(hit, 1.0, onehot)
    oh2 = onehot.reshape(N_EXPERTS, TM)
    rank_full = (_dot(oh2.astype(BF16), tri_ref[...]) + carry_ref[:, :1]).reshape(G, EG, TM)
    for kk in range(TOP_K):
        rk = jnp.sum(jnp.sum(jnp.where(hits[kk], rank_full, 0.0), axis=1, keepdims=True), axis=0, keepdims=True)
        idx_o[kk:kk + 1, :] = idxs[kk].reshape(1, TM)
        gate_o[kk:kk + 1, :] = (ws[kk] * scale).reshape(1, TM)
        rank_o[kk:kk + 1, :] = rk.reshape(1, TM).astype(I32)
    carry_ref[...] = carry_ref[...] + jnp.sum(oh2, axis=1, keepdims=True)
    cnt_o[...] = carry_ref[...]


def _route_shared(x, rw_t, rb, sg, su, sd, tri):
    Np, D = x.shape
    TM = ROUTE_TM
    tok = pl.BlockSpec((TOP_K, TM), lambda i: (0, i))
    return pl.pallas_call(
        _route_kernel,
        out_shape=[jax.ShapeDtypeStruct((Np, D), F32),
                   jax.ShapeDtypeStruct((TOP_K, Np), I32),
                   jax.ShapeDtypeStruct((TOP_K, Np), F32),
                   jax.ShapeDtypeStruct((TOP_K, Np), I32),
                   jax.ShapeDtypeStruct((N_EXPERTS, LANES), F32)],
        grid=(Np // TM,),
        in_specs=[pl.BlockSpec((TM, D), lambda i: (i, 0)), _full(rw_t.shape), _full(rb.shape),
                  _full(sg.shape), _full(su.shape), _full(sd.shape), _full(tri.shape)],
        out_specs=[pl.BlockSpec((TM, D), lambda i: (i, 0)), tok, tok, tok, _full((N_EXPERTS, LANES))],
        scratch_shapes=[pltpu.VMEM((N_EXPERTS, LANES), F32)],
        compiler_params=_cparams(("arbitrary",)),
        name="moe_route",
    )(x, rw_t, rb, sg, su, sd, tri)


def _tile_copy(src_ref, src_row, dst_ref, dst_row, sem):
    src = src_ref.at[pl.ds(pl.multiple_of(src_row * ROW_TILES, ROW_TILES), ROW_TILES)]
    dst = dst_ref.at[pl.ds(pl.multiple_of(dst_row * ROW_TILES, ROW_TILES), ROW_TILES)]
    return pltpu.make_async_copy(src, dst, sem)


def _rows_to_tiles(x, tiles_ref, n_rows):
    for s in range(ROW_TILES):
        tiles_ref[pl.ds(s, n_rows, stride=ROW_TILES), :] = x[:, s * LANES:(s + 1) * LANES]


def _tiles_to_rows(tiles_ref, n_rows):
    return [tiles_ref[pl.ds(s, n_rows, stride=ROW_TILES), :] for s in range(ROW_TILES)]


def _dispatch_kernel(ps_ref, idx_ref, rank_ref, x_ref, xs_hbm, stage_ref, sem):
    _rows_to_tiles(x_ref[...], stage_ref, DISP_TM)

    def issue(n, c):
        for kk in range(TOP_K):
            dst = ps_ref[idx_ref[kk, n]] + rank_ref[kk, n]
            _tile_copy(stage_ref, n, xs_hbm, dst, sem).start(priority=kk % DMA_QUEUES)
        return c

    lax.fori_loop(0, DISP_TM, issue, 0)

    def drain(n, c):
        for kk in range(TOP_K):
            _tile_copy(stage_ref, n, xs_hbm, 0, sem).wait()
        return c

    lax.fori_loop(0, DISP_TM, drain, 0)


def _dispatch(x, idx_t, rank_t, pad_start, n_rows):
    Np, D = x.shape
    tok = pl.BlockSpec((TOP_K, DISP_TM), lambda i, ps: (0, i), memory_space=pltpu.SMEM)
    return pl.pallas_call(
        _dispatch_kernel,
        out_shape=jax.ShapeDtypeStruct((n_rows * ROW_TILES, LANES), F32),
        grid_spec=pltpu.PrefetchScalarGridSpec(
            num_scalar_prefetch=1,
            grid=(Np // DISP_TM,),
            in_specs=[tok, tok, pl.BlockSpec((DISP_TM, D), lambda i, ps: (i, 0))],
            out_specs=pl.BlockSpec(memory_space=pl.ANY),
            scratch_shapes=[pltpu.VMEM((DISP_TM * ROW_TILES, LANES), F32), pltpu.SemaphoreType.DMA(())],
        ),
        compiler_params=_cparams(("arbitrary",)),
        name="moe_dispatch",
    )(pad_start, idx_t, rank_t, x)


def _expert_kernel(be_ref, nu_ref, xs_ref, wg_ref, wu_ref, wd_ref, ys_ref):
    @pl.when(pl.program_id(0) < nu_ref[0])
    def _():
        x = jnp.concatenate(_tiles_to_rows(xs_ref, MOE_BLK), axis=1).astype(BF16)
        g = _dot(x, wg_ref[0])
        hmid = g * jax.nn.sigmoid(g) * _dot(x, wu_ref[0])
        _rows_to_tiles(_dot(hmid.astype(BF16), wd_ref[0]), ys_ref, MOE_BLK)


def _experts(xs, block_e, n_used, wg, wu, wd):
    D, F = wg.shape[1], wg.shape[2]
    nb = xs.shape[0] // (MOE_BLK * ROW_TILES)
    tiles = pl.BlockSpec((MOE_BLK * ROW_TILES, LANES), lambda i, be, nu: (jnp.minimum(i, nu[0] - 1), 0))
    wspec = lambda s: pl.BlockSpec((1,) + s, lambda i, be, nu: (be[jnp.minimum(i, nu[0] - 1)], 0, 0))
    return pl.pallas_call(
        _expert_kernel,
        out_shape=jax.ShapeDtypeStruct(xs.shape, F32),
        grid_spec=pltpu.PrefetchScalarGridSpec(
            num_scalar_prefetch=2,
            grid=(nb,),
            in_specs=[tiles, wspec((D, F)), wspec((D, F)), wspec((F, D))],
            out_specs=tiles,
        ),
        compiler_params=_cparams(("arbitrary",)),
        name="moe_experts",
    )(block_e, n_used, xs, wg, wu, wd)


def _combine_kernel(ps_ref, idx_ref, rank_ref, gate_ref, sh_ref, h_ref, lg_ref, lb_ref, ys_hbm, o_ref,
                    buf, sem):
    def issue(n, c):
        for kk in range(TOP_K):
            src = ps_ref[idx_ref[kk, n]] + rank_ref[kk, n]
            _tile_copy(ys_hbm, src, buf.at[kk], n, sem).start(priority=kk % DMA_QUEUES)
        return c

    lax.fori_loop(0, COMB_TM, issue, 0)

    def drain(n, c):
        for kk in range(TOP_K):
            _tile_copy(ys_hbm, 0, buf.at[kk], n, sem).wait()
        return c

    lax.fori_loop(0, COMB_TM, drain, 0)
    groups = []
    for s in range(ROW_TILES):
        acc = None
        for kk in range(TOP_K):
            part = gate_ref[kk] * buf[kk, pl.ds(s, COMB_TM, stride=ROW_TILES), :]
            acc = part if kk == 0 else acc + part
        groups.append(acc)
    ffn = sh_ref[...] + jnp.concatenate(groups, axis=1)
    o_ref[...] = _layer_norm(DN_ALPHA * h_ref[...] + ffn, lg_ref[...], lb_ref[...])


def _combine(ys, idx_t, rank_t, pad_start, gate, shared, h, lg, lb):
    Np, D = h.shape
    TM = COMB_TM
    tok = pl.BlockSpec((TOP_K, TM), lambda i, ps: (0, i), memory_space=pltpu.SMEM)
    tile = pl.BlockSpec((TM, D), lambda i, ps: (i, 0))
    vec = pl.BlockSpec((1, D), lambda i, ps: (0, 0))
    return pl.pallas_call(
        _combine_kernel,
        out_shape=jax.ShapeDtypeStruct((Np, D), F32),
        grid_spec=pltpu.PrefetchScalarGridSpec(
            num_scalar_prefetch=1,
            grid=(Np // TM,),
            in_specs=[tok, tok, pl.BlockSpec((TOP_K, TM, LANES), lambda i, ps: (0, i, 0)), tile, tile, vec, vec,
                      pl.BlockSpec(memory_space=pl.ANY)],
            out_specs=tile,
            scratch_shapes=[pltpu.VMEM((TOP_K, TM * ROW_TILES, LANES), F32), pltpu.SemaphoreType.DMA(())],
        ),
        compiler_params=_cparams(("arbitrary",)),
        name="moe_combine",
    )(pad_start, idx_t, rank_t, gate, shared, h, lg, lb, ys)


def _moe_layer(h2, l, lg, lb, router_w, router_b, w_gate, w_up, w_down, sh_gate, sh_up, sh_down, tri):
    Np, D = h2.shape
    shared, idx_t, gate_t, rank_t, cnt = _route_shared(
        h2, router_w[l].T, router_b[l].reshape(N_EXPERTS, 1),
        sh_gate[l].astype(BF16), sh_up[l].astype(BF16), sh_down[l].astype(BF16), tri)
    counts = cnt[:, 0].astype(I32)
    padded = (counts + MOE_BLK - 1) // MOE_BLK * MOE_BLK
    pad_end = jnp.cumsum(padded)
    pad_start = (pad_end - padded).astype(I32)
    n_blocks = Np * TOP_K // MOE_BLK + N_EXPERTS
    n_used = (pad_end[-1:] // MOE_BLK).astype(I32)
    blk_row0 = jnp.arange(n_blocks, dtype=I32) * MOE_BLK
    block_e = jnp.minimum(jnp.sum((pad_end[None, :] <= blk_row0[:, None]).astype(I32), axis=1), N_EXPERTS - 1)
    xs = _dispatch(h2, idx_t, rank_t, pad_start, n_blocks * MOE_BLK)
    ys = _experts(xs, block_e, n_used, w_gate[l].astype(BF16), w_up[l].astype(BF16), w_down[l].astype(BF16))
    gate_splat = jnp.broadcast_to(gate_t[:, :, None], (TOP_K, Np, LANES))
    return _combine(ys, idx_t, rank_t, pad_start, gate_splat, shared, h2, lg, lb)


def _pad_cols(w, n):
    return jnp.pad(w, ((0, 0), (0, n - w.shape[1])))


def _pad_rows(w, n):
    return jnp.pad(w, ((0, n - w.shape[0]), (0, 0)))


def _trunk(x, meta_tokens, ln_mix_g, ln_mix_b, ln_ffn_g, ln_ffn_b,
           rw_mu, rw_w_rkv, rw_w0, rw_w_l1, rw_w_l2, rw_a0, rw_a_l1, rw_a_l2,
           rw_g_l1, rw_g_l2, rw_k_k, rw_k_a, rw_r_k, rw_lnx_g, rw_lnx_b, rw_w_out,
           rw_v0, rw_v_l1, rw_v_l2, kv_w,
           da_w_q, da_lam_q1, da_lam_k1, da_lam_q2, da_lam_k2, da_subln_g, da_w_out,
           moe_router_w, moe_router_b, moe_w_gate, moe_w_up, moe_w_down,
           moe_sh_gate, moe_sh_up, moe_sh_down):
    B, S, D = x.shape
    L = S + N_META
    Lp = -(-L // SEQ_ALIGN) * SEQ_ALIGN
    Np = B * Lp
    assert D == D_MODEL and Np % ROW_TM == 0 and Lp % PROJ_TM == 0 and Lp % SCAN_TC == 0 and Lp % ATT_T == 0
    meta = jnp.broadcast_to(meta_tokens[None].astype(x.dtype), (B, N_META, D))
    h = jnp.concatenate([meta, x, jnp.zeros((B, Lp - L, D), x.dtype)], axis=1)

    pc = _perm_cols()
    gs_np, gb_np = _head_sum_mats()
    gs, gb = jnp.asarray(gs_np), jnp.asarray(gb_np)
    tri = jnp.asarray(np.triu(np.ones((ROUTE_TM, ROUTE_TM), np.float32), 1)).astype(BF16)
    row = lambda vec: vec.reshape(1, -1)
    lane = np.arange(2 * D_MODEL)
    m_key, m_head = lane // 32, lane % 16
    c128 = np.arange(LANES) % 16
    vrow = np.arange(RWKV_HEAD)

    v_first = None
    kv = None
    for l in range(DEPTH):
        if l < N_A_LAYERS:
            p = {
                "mu": _pad_rows(rw_mu[l], SUBLANES),
                "wr": rw_w_rkv[l, 0][:, pc].astype(BF16),
                "wk": rw_w_rkv[l, 1][:, pc].astype(BF16),
                "wv": rw_w_rkv[l, 2][:, pc].astype(BF16),
                "w0": row(rw_w0[l][pc]),
                "wl1": _pad_cols(rw_w_l1[l], LANES).astype(BF16),
                "wl2": _pad_rows(rw_w_l2[l][:, pc], LANES).astype(BF16),
                "a0": row(rw_a0[l][pc]),
                "al1": _pad_cols(rw_a_l1[l], LANES).astype(BF16),
                "al2": _pad_rows(rw_a_l2[l][:, pc], LANES).astype(BF16),
                "gl1": _pad_cols(rw_g_l1[l], 2 * LANES).astype(BF16),
                "gl2": _pad_rows(rw_g_l2[l][:, pc], 2 * LANES).astype(BF16),
                "kk": row(rw_k_k[l][pc]),
                "ka": row(rw_k_a[l][pc]),
                "gs": gs,
                "gb": gb,
            }
            if l > 0:
                p["v0"] = row(rw_v0[l - 1][pc])
                p["vl1"] = _pad_cols(rw_v_l1[l - 1], LANES).astype(BF16)
                p["vl2"] = _pad_rows(rw_v_l2[l - 1][:, pc], LANES).astype(BF16)
            r, w, k, v, a, b, g = _rwkv_proj(h, v_first if l > 0 else None, p)
            if l == 0:
                v_first = v
            rk_m = rw_r_k[l][m_head, m_key].reshape(1, 2 * D_MODEL)
            lg_t = rw_lnx_g[l].reshape(RWKV_HEADS, RWKV_HEAD)[c128[None, :], vrow[:, None]]
            lb_t = rw_lnx_b[l].reshape(RWKV_HEADS, RWKV_HEAD)[c128[None, :], vrow[:, None]]
            z = _wkv_scan(r, w, k, v, a, b, rk_m, lg_t, lb_t)
            h2 = _mm_res_ln(z.reshape(Np, D), g.reshape(Np, D), rw_w_out[l][pc, :].astype(BF16),
                            h.reshape(Np, D), row(ln_mix_g[l]), row(ln_mix_b[l]))
        else:
            j = l - N_A_LAYERS
            h2 = h.reshape(Np, D)
            if kv is None:
                kv = _proj(h2, kv_w.astype(BF16)).reshape(B, Lp, 2 * D)
            q_scale = DIFF_HD ** -0.5 * math.log2(math.e)
            q = _proj(h2, (da_w_q[j] * q_scale).astype(BF16)).reshape(B, Lp, D)
            lam_init = 0.8 - 0.6 * math.exp(-0.3 * l)
            lam4 = jnp.stack([da_lam_q1[j], da_lam_k1[j], da_lam_q2[j], da_lam_k2[j]])
            o = _diff_attn(q, kv, _pad_rows(lam4, SUBLANES), row(da_subln_g[j]), lam_init)
            h2 = _mm_res_ln(o.reshape(Np, D), None, da_w_out[j].astype(BF16), h2,
                            row(ln_mix_g[l]), row(ln_mix_b[l]))
        h2 = _moe_layer(h2, l, row(ln_ffn_g[l]), row(ln_ffn_b[l]), moe_router_w, moe_router_b,
                        moe_w_gate, moe_w_up, moe_w_down, moe_sh_gate, moe_sh_up, moe_sh_down, tri)
        h = h2.reshape(B, Lp, D)
    return h[:, N_META:L]


_trunk_jit = jax.jit(_trunk)


def kernel(x, meta_tokens, ln_mix_g, ln_mix_b, ln_ffn_g, ln_ffn_b, rw_mu, rw_w_rkv, rw_w0, rw_w_l1, rw_w_l2, rw_a0, rw_a_l1, rw_a_l2, rw_g_l1, rw_g_l2, rw_k_k, rw_k_a, rw_r_k, rw_lnx_g, rw_lnx_b, rw_w_out, rw_v0, rw_v_l1, rw_v_l2, kv_w, da_w_q, da_lam_q1, da_lam_k1, da_lam_q2, da_lam_k2, da_subln_g, da_w_out, moe_router_w, moe_router_b, moe_w_gate, moe_w_up, moe_w_down, moe_sh_gate, moe_sh_up, moe_sh_down):
    return _trunk_jit(x, meta_tokens, ln_mix_g, ln_mix_b, ln_ffn_g, ln_ffn_b, rw_mu, rw_w_rkv, rw_w0, rw_w_l1,
                      rw_w_l2, rw_a0, rw_a_l1, rw_a_l2, rw_g_l1, rw_g_l2, rw_k_k, rw_k_a, rw_r_k, rw_lnx_g,
                      rw_lnx_b, rw_w_out, rw_v0, rw_v_l1, rw_v_l2, kv_w, da_w_q, da_lam_q1, da_lam_k1,
                      da_lam_q2, da_lam_k2, da_subln_g, da_w_out, moe_router_w, moe_router_b, moe_w_gate,
                      moe_w_up, moe_w_down, moe_sh_gate, moe_sh_up, moe_sh_down)
```

```python
import functools
import math

import numpy as np
import jax
import jax.numpy as jnp
from jax import lax
from jax.experimental import pallas as pl
from jax.experimental.pallas import tpu as pltpu

F32 = jnp.float32
BF16 = jnp.bfloat16
I32 = jnp.int32

D_MODEL = 1024
DEPTH = 4
N_META = 16
N_A_LAYERS = DEPTH // 2
RWKV_HEAD = 64
RWKV_HEADS = D_MODEL // RWKV_HEAD
RWKV_GN_EPS = 64e-5
DIFF_HEADS = 8
DIFF_HD = D_MODEL // (2 * DIFF_HEADS)
N_EXPERTS = 64
TOP_K = 8
N_GROUPS = 8
TOPK_GROUPS = 4
EXPERT_FF = D_MODEL // 4
ROUTED_SCALE = 2.5
DN_ALPHA = (2 * DEPTH) ** 0.25
LN_EPS = 1e-5

LANES = 128
SUBLANES = 8
VMEM_LIMIT_BYTES = 56 * 1024 * 1024
DMA_QUEUES = 2
ROW_TILES = D_MODEL // LANES

SEQ_ALIGN = 256
PROJ_TM = 256
ROW_TM = 512
SCAN_TC = 128
SCAN_SUB = SUBLANES
ATT_T = 256
ROUTE_TM = 256
MOE_BLK = 512
DISP_TM = 512
COMB_TM = 256

_SIGMA = (0, 4, 1, 5, 2, 6, 3, 7)


def _cparams(sem):
    return pltpu.CompilerParams(dimension_semantics=sem, vmem_limit_bytes=VMEM_LIMIT_BYTES)


def _dot(a, b):
    return jnp.dot(a, b, preferred_element_type=F32)


def _dot_hi(a, b):
    return jnp.dot(a, b, preferred_element_type=F32, precision=lax.Precision.HIGHEST)


def _full(shape):
    nd = len(shape)
    return pl.BlockSpec(shape, lambda *_: (0,) * nd)


def _layer_norm(y, g, b):
    mu = jnp.mean(y, axis=-1, keepdims=True)
    d = y - mu
    var = jnp.mean(d * d, axis=-1, keepdims=True)
    return d * lax.rsqrt(var + LN_EPS) * g + b


def _perm_cols():
    L = np.arange(D_MODEL)
    j, p, hh = L // LANES, (L % LANES) // 16, L % 16
    i = 8 * j + np.asarray(_SIGMA)[p]
    return (hh * RWKV_HEAD + i).astype(np.int32)


def _head_sum_mats():
    L = np.arange(D_MODEL)
    c = np.arange(LANES)
    gs = (L[:, None] % 16 == c[None, :] % 16).astype(np.float32)
    gb = ((c[:, None] < 16) & (c[:, None] == L[None, :] % 16)).astype(np.float32)
    return gs, gb


def _rwkv_proj_kernel(has_vres, *refs):
    if has_vres:
        (x_ref, xp_ref, mu_ref, wr_ref, wk_ref, wv_ref, w0_ref, wl1_ref, wl2_ref, a0_ref, al1_ref,
         al2_ref, gl1_ref, gl2_ref, kk_ref, ka_ref, gs_ref, gb_ref, vf_ref, v0_ref, vl1_ref, vl2_ref,
         r_o, w_o, k_o, v_o, a_o, b_o, g_o) = refs
    else:
        (x_ref, xp_ref, mu_ref, wr_ref, wk_ref, wv_ref, w0_ref, wl1_ref, wl2_ref, a0_ref, al1_ref,
         al2_ref, gl1_ref, gl2_ref, kk_ref, ka_ref, gs_ref, gb_ref,
         r_o, w_o, k_o, v_o, a_o, b_o, g_o) = refs
    i = pl.program_id(1)
    x = x_ref[0]
    prev = xp_ref[0][SUBLANES - 1:SUBLANES, :]
    prev = jnp.where(i == 0, 0.0, prev)
    row = lax.broadcasted_iota(I32, x.shape, 0)
    xprev = jnp.where(row == 0, prev, pltpu.roll(x, 1, 0))
    dx = xprev - x

    def mix(s):
        return (x + dx * mu_ref[s:s + 1, :]).astype(BF16)

    xv = mix(2)
    r = _dot(mix(0), wr_ref[...])
    k = _dot(mix(1), wk_ref[...])
    v = _dot(xv, wv_ref[...])
    zw = w0_ref[...] + _dot(jnp.tanh(_dot(mix(3), wl1_ref[...])).astype(BF16), wl2_ref[...])
    decay = jnp.exp(-math.exp(-0.5) * jax.nn.sigmoid(zw))
    a = jax.nn.sigmoid(a0_ref[...] + _dot(_dot(mix(4), al1_ref[...]).astype(BF16), al2_ref[...]))
    g = _dot(jax.nn.sigmoid(_dot(mix(5), gl1_ref[...])).astype(BF16), gl2_ref[...])
    if has_vres:
        gate_v = jax.nn.sigmoid(v0_ref[...] + _dot(_dot(xv, vl1_ref[...]).astype(BF16), vl2_ref[...]))
        v = v + (vf_ref[0] - v) * gate_v
    kk = k * kk_ref[...]
    ss = _dot_hi(kk * kk, gs_ref[...])
    inv = 1.0 / jnp.maximum(jnp.sqrt(ss), 1e-12)
    kk = kk * _dot_hi(inv, gb_ref[...])
    k = k * (1.0 + (a - 1.0) * ka_ref[...])
    r_o[0] = r
    w_o[0] = decay
    k_o[0] = k
    v_o[0] = v
    a_o[0] = -kk
    b_o[0] = kk * a
    g_o[0] = g


def _rwkv_proj(h, vfirst, p):
    B, Lp, D = h.shape
    TM = PROJ_TM
    has_vres = vfirst is not None
    tile = pl.BlockSpec((1, TM, D), lambda b, i: (b, i, 0))
    prev8 = pl.BlockSpec((1, SUBLANES, D), lambda b, i: (b, jnp.maximum(i * (TM // SUBLANES) - 1, 0), 0))
    names = ["mu", "wr", "wk", "wv", "w0", "wl1", "wl2", "a0", "al1", "al2", "gl1", "gl2", "kk", "ka", "gs", "gb"]
    args = [h, h] + [p[n] for n in names]
    specs = [tile, prev8] + [_full(p[n].shape) for n in names]
    if has_vres:
        args += [vfirst, p["v0"], p["vl1"], p["vl2"]]
        specs += [tile, _full(p["v0"].shape), _full(p["vl1"].shape), _full(p["vl2"].shape)]
    out = jax.ShapeDtypeStruct((B, Lp, D), F32)
    return pl.pallas_call(
        functools.partial(_rwkv_proj_kernel, has_vres),
        out_shape=[out] * 7,
        grid=(B, Lp // TM),
        in_specs=specs,
        out_specs=[tile] * 7,
        compiler_params=_cparams(("parallel", "arbitrary")),
        name="rwkv_proj",
    )(*args)


def _wkv_kernel(r_ref, w_ref, k_ref, v_ref, a_ref, b_ref, rk_ref, lg_ref, lb_ref, o_ref,
                s_ref, ma_ref, mwr_ref, mw_ref, mb_ref, mk_ref, zv_ref, br_ref, kr_ref, rkk_ref, yc_ref):
    NV = RWKV_HEAD
    NCOL = 2 * D_MODEL // LANES
    NRB = NV // SUBLANES

    @pl.when(pl.program_id(0) == 0)
    def _():
        s_ref[...] = jnp.zeros_like(s_ref)

    lane = lax.broadcasted_iota(I32, (SUBLANES, LANES), 1)
    even = ((lane // 16) % 2) == 0
    grp = lane // 32

    def cs(j):
        return slice(j * LANES, (j + 1) * LANES)

    def merge(ref, t0):
        x0 = ref[0, pl.ds(t0, SUBLANES), :]
        x1 = ref[1, pl.ds(t0, SUBLANES), :]
        cols = []
        for j in range(D_MODEL // LANES):
            a0 = x0[:, cs(j)]
            a1 = x1[:, cs(j)]
            cols.append(jnp.where(even, a0, pltpu.roll(a1, 16, 1)))
            cols.append(jnp.where(even, pltpu.roll(a0, LANES - 16, 1), a1))
        return cols

    def fold(x):
        return (x + pltpu.roll(x, 32, 1)) + (pltpu.roll(x, 64, 1) + pltpu.roll(x, 96, 1))

    def colsum(xs):
        acc = xs[0]
        for x in xs[1:]:
            acc = acc + x
        return acc

    def sub_chunk(c, carry):
        t0 = pl.multiple_of(c * SCAN_SUB, SCAN_SUB)
        R = merge(r_ref, t0)
        W = merge(w_ref, t0)
        K = merge(k_ref, t0)
        V = merge(v_ref, t0)
        A = merge(a_ref, t0)
        Bm = merge(b_ref, t0)
        for j in range(NCOL):
            ma_ref[j] = A[j]
            mw_ref[j] = W[j]
            mwr_ref[j] = W[j] * R[j]
            mb_ref[j] = Bm[j]
            mk_ref[j] = K[j]
        br_ref[...] = fold(colsum([Bm[j] * R[j] for j in range(NCOL)]))
        kr_ref[...] = fold(colsum([K[j] * R[j] for j in range(NCOL)]))
        rkk_ref[...] = fold(colsum([K[j] * R[j] * rk_ref[:, cs(j)] for j in range(NCOL)]))
        for j in range(NCOL):
            rolled = [V[j]] + [pltpu.roll(V[j], 32 * m, 1) for m in range(1, 4)]
            for q in range(4):
                z = rolled[(3 - q) % 4]
                for pos in range(2, -1, -1):
                    z = jnp.where(grp == pos, rolled[(pos - q) % 4], z)
                zv_ref[pl.ds(4 * j + q, SUBLANES, stride=NV), :] = z

        def pair_sum(xs):
            return colsum(xs[0::2]) + colsum(xs[1::2])

        def step(t):
            trow = slice(t, t + 1)
            base = t * NV
            brt = br_ref[trow, :]
            krt = kr_ref[trow, :]
            rkt = rkk_ref[trow, :]
            vcol, y = [], []
            for i in range(NRB):
                rows = slice(i * SUBLANES, (i + 1) * SUBLANES)
                srow = [s_ref[rows, cs(j)] for j in range(NCOL)]
                sa = fold(pair_sum([srow[j] * ma_ref[j, trow, :] for j in range(NCOL)]))
                yp = fold(pair_sum([srow[j] * mwr_ref[j, trow, :] for j in range(NCOL)]))
                vc = zv_ref[pl.ds(base + i * SUBLANES, SUBLANES), :]
                for j in range(NCOL):
                    s_ref[rows, cs(j)] = (srow[j] * mw_ref[j, trow, :] + sa * mb_ref[j, trow, :]
                                          + vc * mk_ref[j, trow, :])
                vcol.append(vc)
                y.append(yp + sa * brt + vc * krt)
            mean = jnp.sum(colsum(y), axis=0, keepdims=True) * (1.0 / NV)
            d = [yi - mean for yi in y]
            var = jnp.sum(colsum([di * di for di in d]), axis=0, keepdims=True) * (1.0 / NV)
            inv = lax.rsqrt(var + RWKV_GN_EPS)
            for i in range(NRB):
                rows = slice(i * SUBLANES, (i + 1) * SUBLANES)
                out = d[i] * inv * lg_ref[rows, :] + lb_ref[rows, :] + vcol[i] * rkt
                yc_ref[pl.ds(base + i * SUBLANES, SUBLANES), :] = out

        for t in range(SCAN_SUB):
            step(t)

        cols = []
        for j in range(NCOL):
            acc = None
            for q in range(4):
                z = yc_ref[pl.ds(4 * j + q, SUBLANES, stride=NV), :]
                acc = z if q == 0 else jnp.where(grp == q, z, acc)
            cols.append(acc)
        for j in range(D_MODEL // LANES):
            c0, c1 = cols[2 * j], cols[2 * j + 1]
            o_ref[0, pl.ds(t0, SUBLANES), cs(j)] = jnp.where(even, c0, pltpu.roll(c1, 16, 1))
            o_ref[1, pl.ds(t0, SUBLANES), cs(j)] = jnp.where(even, pltpu.roll(c0, LANES - 16, 1), c1)
        return carry

    lax.fori_loop(0, SCAN_TC // SCAN_SUB, sub_chunk, 0)


def _wkv_scan(r, w, k, v, a, b, rk_m, lg_t, lb_t):
    B, Lp, D = r.shape
    assert B == 2, "the scan packs exactly two batch rows into the lane dimension"
    blk = pl.BlockSpec((B, SCAN_TC, D), lambda i: (0, i, 0))
    vm = lambda *s: pltpu.VMEM(s, F32)
    return pl.pallas_call(
        _wkv_kernel,
        out_shape=jax.ShapeDtypeStruct((B, Lp, D), F32),
        grid=(Lp // SCAN_TC,),
        in_specs=[blk] * 6 + [_full(rk_m.shape), _full(lg_t.shape), _full(lb_t.shape)],
        out_specs=blk,
        scratch_shapes=[vm(RWKV_HEAD, 2 * D)] + [vm(2 * D // LANES, SCAN_SUB, LANES)] * 5
        + [vm(SCAN_SUB * RWKV_HEAD, LANES)] + [vm(SCAN_SUB, LANES)] * 3
        + [vm(SCAN_SUB * RWKV_HEAD, LANES)],
        compiler_params=_cparams(("arbitrary",)),
        name="wkv_scan",
    )(r, w, k, v, a, b, rk_m, lg_t, lb_t)


def _mm_res_ln_kernel(has_gate, *refs):
    if has_gate:
        z_ref, g_ref, w_ref, h_ref, lg_ref, lb_ref, o_ref = refs
        z = (z_ref[...] * g_ref[...]).astype(BF16)
    else:
        z_ref, w_ref, h_ref, lg_ref, lb_ref, o_ref = refs
        z = z_ref[...].astype(BF16)
    y = DN_ALPHA * h_ref[...] + _dot(z, w_ref[...])
    o_ref[...] = _layer_norm(y, lg_ref[...], lb_ref[...])


def _mm_res_ln(z, gate, w, h, lg, lb):
    Np, D = h.shape
    TM = ROW_TM
    tile = pl.BlockSpec((TM, D), lambda i: (i, 0))
    has_gate = gate is not None
    args = [z] + ([gate] if has_gate else []) + [w, h, lg, lb]
    specs = [tile] + ([tile] if has_gate else []) + [_full(w.shape), tile, _full(lg.shape), _full(lb.shape)]
    return pl.pallas_call(
        functools.partial(_mm_res_ln_kernel, has_gate),
        out_shape=jax.ShapeDtypeStruct((Np, D), F32),
        grid=(Np // TM,),
        in_specs=specs,
        out_specs=tile,
        compiler_params=_cparams(("parallel",)),
        name="mm_res_ln",
    )(*args)


def _proj_kernel(x_ref, w_ref, o_ref):
    o_ref[...] = _dot(x_ref[...].astype(BF16), w_ref[...]).astype(o_ref.dtype)


def _proj(x, w):
    Np, D = x.shape
    Nout = w.shape[1]
    TM = ROW_TM
    return pl.pallas_call(
        _proj_kernel,
        out_shape=jax.ShapeDtypeStruct((Np, Nout), BF16),
        grid=(Np // TM,),
        in_specs=[pl.BlockSpec((TM, D), lambda i: (i, 0)), _full(w.shape)],
        out_specs=pl.BlockSpec((TM, Nout), lambda i: (i, 0)),
        compiler_params=_cparams(("parallel",)),
        name="proj",
    )(x, w)


def _attn_kernel(lam_init, lam_ref, sg_ref, q_ref, k_ref, v_ref, o_ref,
                 sa_ref, sb_ref, m1_ref, l1_ref, acc1_ref, m2_ref, l2_ref, acc2_ref):
    T = ATT_T
    qi = pl.program_id(2)
    last = pl.num_programs(2) - 1
    q = q_ref[0]
    lane = lax.broadcasted_iota(I32, q.shape, 1)
    zero = jnp.zeros_like(q)
    q1 = jnp.where(lane < DIFF_HD, q, zero)
    q2 = jnp.where(lane >= DIFF_HD, q, zero)
    m1_ref[...] = jnp.full_like(m1_ref, -1e30)
    m2_ref[...] = jnp.full_like(m2_ref, -1e30)
    l1_ref[...] = jnp.zeros_like(l1_ref)
    l2_ref[...] = jnp.zeros_like(l2_ref)
    acc1_ref[...] = jnp.zeros_like(acc1_ref)
    acc2_ref[...] = jnp.zeros_like(acc2_ref)
    nt = (((1,), (1,)), ((), ()))
    ones = jnp.ones((T, LANES), BF16)
    rel = lax.broadcasted_iota(I32, (T, T), 1) - lax.broadcasted_iota(I32, (T, T), 0)

    def chunk_start(kc):
        return pl.multiple_of(jnp.minimum(kc, last) * T, T)

    def halves(x):
        return [x[:, c * LANES:(c + 1) * LANES] for c in range(x.shape[1] // LANES)]

    def scores(g, buf):
        for half in range(2):
            kb = k_ref[0, pl.ds(chunk_start(2 * g + half), T), :]
            buf[0, :, half * T:(half + 1) * T] = lax.dot_general(q1, kb, nt, preferred_element_type=F32)
            buf[1, :, half * T:(half + 1) * T] = lax.dot_general(q2, kb, nt, preferred_element_type=F32)

    def consume(g, buf, masked):
        vext = jnp.concatenate(
            [jnp.concatenate([v_ref[0, pl.ds(chunk_start(2 * g + half), T), :], ones], axis=1) for half in range(2)],
            axis=0)
        for si, m_ref, l_ref, acc_ref in ((0, m1_ref, l1_ref, acc1_ref), (1, m2_ref, l2_ref, acc2_ref)):
            s = buf[si]
            if masked:
                s = jnp.concatenate(
                    [jnp.where(rel <= (qi - (2 * g + half)) * T, s[:, half * T:(half + 1) * T], -jnp.inf)
                     for half in range(2)], axis=1)
            parts = halves(s)
            smax = jnp.max(jnp.maximum(jnp.maximum(parts[0], parts[1]), jnp.maximum(parts[2], parts[3])),
                           axis=1, keepdims=True)
            m_old = m_ref[...]
            m_new = jnp.maximum(m_old, smax)
            alpha = jnp.exp2(m_old - m_new)
            p = jnp.exp2(s - jnp.concatenate([m_new] * len(parts), axis=1)).astype(BF16)
            d = _dot(p, vext)
            acc_ref[...] = alpha * acc_ref[...] + d[:, :LANES]
            l_ref[...] = alpha * l_ref[...] + d[:, LANES:]
            m_ref[...] = m_new

    n_free = qi // 2
    scores(0, sa_ref)

    def pair(i, carry):
        g = 2 * i
        scores(g + 1, sb_ref)
        consume(g, sa_ref, False)
        scores(g + 2, sa_ref)
        consume(g + 1, sb_ref, False)
        return carry

    n_pairs = n_free // 2
    lax.fori_loop(0, n_pairs, pair, 0)
    g_a = 2 * n_pairs
    odd = n_free % 2 == 1

    @pl.when(odd)
    def _():
        scores(g_a + 1, sb_ref)

    consume(g_a, sa_ref, True)

    @pl.when(odd)
    def _():
        consume(g_a + 1, sb_ref, True)


    lam_v = lam_ref[...]
    lam = (jnp.exp(jnp.sum(lam_v[0:1] * lam_v[1:2], axis=1, keepdims=True))
           - jnp.exp(jnp.sum(lam_v[2:3] * lam_v[3:4], axis=1, keepdims=True)) + lam_init)
    o = acc1_ref[...] / l1_ref[...] - lam * (acc2_ref[...] / l2_ref[...])
    o = o * lax.rsqrt(jnp.mean(o * o, axis=1, keepdims=True) + 1e-5) * sg_ref[...] * (1.0 - lam_init)
    o_ref[0] = o.astype(o_ref.dtype)


def _diff_attn(q, kv, lam4, subln_g, lam_init):
    B, Lp, D = q.shape
    T = ATT_T
    H = DIFF_HEADS
    HW = 2 * DIFF_HD
    vm = lambda *s: pltpu.VMEM(s, F32)
    return pl.pallas_call(
        functools.partial(_attn_kernel, lam_init),
        out_shape=jax.ShapeDtypeStruct((B, Lp, D), BF16),
        grid=(B, H, Lp // T),
        in_specs=[_full(lam4.shape), _full(subln_g.shape),
                  pl.BlockSpec((1, T, HW), lambda b, h, i: (b, i, h)),
                  pl.BlockSpec((1, Lp, HW), lambda b, h, i: (b, 0, h)),
                  pl.BlockSpec((1, Lp, HW), lambda b, h, i: (b, 0, H + h))],
        out_specs=pl.BlockSpec((1, T, HW), lambda b, h, i: (b, i, h)),
        scratch_shapes=[vm(2, T, 2 * T), vm(2, T, 2 * T),
                        vm(T, LANES), vm(T, LANES), vm(T, HW), vm(T, LANES), vm(T, LANES), vm(T, HW)],
        compiler_params=_cparams(("parallel", "parallel", "arbitrary")),
        name="diff_attn",
    )(lam4, subln_g, q, kv, kv)


def _route_kernel(x_ref, rw_ref, rb_ref, sg_ref, su_ref, sd_ref, tri_ref,
                  sh_o, idx_o, gate_o, rank_o, cnt_o, carry_ref):
    TM = ROUTE_TM
    G, EG = N_GROUPS, N_EXPERTS // N_GROUPS

    @pl.when(pl.program_id(0) == 0)
    def _():
        carry_ref[...] = jnp.zeros_like(carry_ref)

    x = x_ref[...]
    xb = x.astype(BF16)
    hmid = _dot(xb, sg_ref[...])
    hmid = hmid * jax.nn.sigmoid(hmid) * _dot(xb, su_ref[...])
    sh_o[...] = _dot(hmid.astype(BF16), sd_ref[...])

    logit = lax.dot_general(rw_ref[...], x, (((1,), (1,)), ((), ())),
                            preferred_element_type=F32, precision=lax.Precision.HIGHEST)
    s = jax.nn.sigmoid(logit)
    s3 = s.reshape(G, EG, TM)
    sel3 = (s + rb_ref[...]).reshape(G, EG, TM)
    io_j = lax.broadcasted_iota(I32, (G, EG, TM), 1)
    io_g = lax.broadcasted_iota(I32, (G, 1, TM), 0)
    neg = -jnp.inf
    m1 = jnp.max(sel3, axis=1, keepdims=True)
    i1 = jnp.min(jnp.where(sel3 == m1, io_j, EG), axis=1, keepdims=True)
    m2 = jnp.max(jnp.where(io_j == i1, neg, sel3), axis=1, keepdims=True)
    gsc = m1 + m2
    gkeep = jnp.zeros((G, 1, TM), F32)
    for _ in range(TOPK_GROUPS):
        m = jnp.max(gsc, axis=0, keepdims=True)
        gi = jnp.min(jnp.where(gsc == m, io_g, G), axis=0, keepdims=True)
        hit = io_g == gi
        gkeep = jnp.where(hit, 1.0, gkeep)
        gsc = jnp.where(hit, neg, gsc)
    cur = jnp.where(gkeep > 0.0, sel3, neg)
    io_e = io_g * EG + io_j
    hits, idxs, ws = [], [], []
    for _ in range(TOP_K):
        m = jnp.max(jnp.max(cur, axis=1, keepdims=True), axis=0, keepdims=True)
        ei = jnp.min(jnp.min(jnp.where(cur == m, io_e, N_EXPERTS), axis=1, keepdims=True), axis=0, keepdims=True)
        hit = io_e == ei
        ws.append(jnp.sum(jnp.sum(jnp.where(hit, s3, 0.0), axis=1, keepdims=True), axis=0, keepdims=True))
        cur = jnp.where(hit, neg, cur)
        hits.append(hit)
        idxs.append(ei)
    wsum = ws[0]
    for wv in ws[1:]:
        wsum = wsum + wv
    scale = ROUTED_SCALE / wsum
    onehot = jnp.zeros((G, EG, TM), F32)
    for hit in hits:
        onehot = jnp.where(hit, 1.0, onehot)
    oh2 = onehot.reshape(N_EXPERTS, TM)
    rank_full = (_dot(oh2.astype(BF16), tri_ref[...]) + carry_ref[:, :1]).reshape(G, EG, TM)
    for kk in range(TOP_K):
        rk = jnp.sum(jnp.sum(jnp.where(hits[kk], rank_full, 0.0), axis=1, keepdims=True), axis=0, keepdims=True)
        idx_o[kk:kk + 1, :] = idxs[kk].reshape(1, TM)
        gate_o[kk:kk + 1, :] = (ws[kk] * scale).reshape(1, TM)
        rank_o[kk:kk + 1, :] = rk.reshape(1, TM).astype(I32)
    carry_ref[...] = carry_ref[...] + jnp.sum(oh2, axis=1, keepdims=True)
    cnt_o[...] = carry_ref[...]


def _route_shared(x, rw_t, rb, sg, su, sd, tri):
    Np, D = x.shape
    TM = ROUTE_TM
    tok = pl.BlockSpec((TOP_K, TM), lambda i: (0, i))
    return pl.pallas_call(
        _route_kernel,
        out_shape=[jax.ShapeDtypeStruct((Np, D), F32),
                   jax.ShapeDtypeStruct((TOP_K, Np), I32),
                   jax.ShapeDtypeStruct((TOP_K, Np), F32),
                   jax.ShapeDtypeStruct((TOP_K, Np), I32),
                   jax.ShapeDtypeStruct((N_EXPERTS, LANES), F32)],
        grid=(Np // TM,),
        in_specs=[pl.BlockSpec((TM, D), lambda i: (i, 0)), _full(rw_t.shape), _full(rb.shape),
                  _full(sg.shape), _full(su.shape), _full(sd.shape), _full(tri.shape)],
        out_specs=[pl.BlockSpec((TM, D), lambda i: (i, 0)), tok, tok, tok, _full((N_EXPERTS, LANES))],
        scratch_shapes=[pltpu.VMEM((N_EXPERTS, LANES), F32)],
        compiler_params=_cparams(("arbitrary",)),
        name="moe_route",
    )(x, rw_t, rb, sg, su, sd, tri)


def _tile_copy(src_ref, src_row, dst_ref, dst_row, sem):
    src = src_ref.at[pl.ds(pl.multiple_of(src_row * ROW_TILES, ROW_TILES), ROW_TILES)]
    dst = dst_ref.at[pl.ds(pl.multiple_of(dst_row * ROW_TILES, ROW_TILES), ROW_TILES)]
    return pltpu.make_async_copy(src, dst, sem)


def _rows_to_tiles(x, tiles_ref, n_rows):
    for s in range(ROW_TILES):
        tiles_ref[pl.ds(s, n_rows, stride=ROW_TILES), :] = x[:, s * LANES:(s + 1) * LANES]


def _tiles_to_rows(tiles_ref, n_rows):
    return [tiles_ref[pl.ds(s, n_rows, stride=ROW_TILES), :] for s in range(ROW_TILES)]


def _dispatch_kernel(ps_ref, idx_ref, rank_ref, x_ref, xs_hbm, stage_ref, sem):
    _rows_to_tiles(x_ref[...], stage_ref, DISP_TM)

    def issue(n, c):
        for kk in range(TOP_K):
            dst = ps_ref[idx_ref[kk, n]] + rank_ref[kk, n]
            _tile_copy(stage_ref, n, xs_hbm, dst, sem).start(priority=kk % DMA_QUEUES)
        return c

    lax.fori_loop(0, DISP_TM, issue, 0)

    def drain(n, c):
        for kk in range(TOP_K):
            _tile_copy(stage_ref, n, xs_hbm, 0, sem).wait()
        return c

    lax.fori_loop(0, DISP_TM, drain, 0)


def _dispatch(x, idx_t, rank_t, pad_start, n_rows):
    Np, D = x.shape
    tok = pl.BlockSpec((TOP_K, DISP_TM), lambda i, ps: (0, i), memory_space=pltpu.SMEM)
    return pl.pallas_call(
        _dispatch_kernel,
        out_shape=jax.ShapeDtypeStruct((n_rows * ROW_TILES, LANES), F32),
        grid_spec=pltpu.PrefetchScalarGridSpec(
            num_scalar_prefetch=1,
            grid=(Np // DISP_TM,),
            in_specs=[tok, tok, pl.BlockSpec((DISP_TM, D), lambda i, ps: (i, 0))],
            out_specs=pl.BlockSpec(memory_space=pl.ANY),
            scratch_shapes=[pltpu.VMEM((DISP_TM * ROW_TILES, LANES), F32), pltpu.SemaphoreType.DMA(())],
        ),
        compiler_params=_cparams(("arbitrary",)),
        name="moe_dispatch",
    )(pad_start, idx_t, rank_t, x)


def _expert_kernel(be_ref, nu_ref, xs_ref, wg_ref, wu_ref, wd_ref, ys_ref):
    @pl.when(pl.program_id(0) < nu_ref[0])
    def _():
        x = jnp.concatenate(_tiles_to_rows(xs_ref, MOE_BLK), axis=1).astype(BF16)
        g = _dot(x, wg_ref[0])
        hmid = g * jax.nn.sigmoid(g) * _dot(x, wu_ref[0])
        _rows_to_tiles(_dot(hmid.astype(BF16), wd_ref[0]), ys_ref, MOE_BLK)


def _experts(xs, block_e, n_used, wg, wu, wd):
    D, F = wg.shape[1], wg.shape[2]
    nb = xs.shape[0] // (MOE_BLK * ROW_TILES)
    tiles = pl.BlockSpec((MOE_BLK * ROW_TILES, LANES), lambda i, be, nu: (jnp.minimum(i, nu[0] - 1), 0))
    wspec = lambda s: pl.BlockSpec((1,) + s, lambda i, be, nu: (be[jnp.minimum(i, nu[0] - 1)], 0, 0))
    return pl.pallas_call(
        _expert_kernel,
        out_shape=jax.ShapeDtypeStruct(xs.shape, F32),
        grid_spec=pltpu.PrefetchScalarGridSpec(
            num_scalar_prefetch=2,
            grid=(nb,),
            in_specs=[tiles, wspec((D, F)), wspec((D, F)), wspec((F, D))],
            out_specs=tiles,
        ),
        compiler_params=_cparams(("arbitrary",)),
        name="moe_experts",
    )(block_e, n_used, xs, wg, wu, wd)


def _combine_kernel(ps_ref, idx_ref, rank_ref, gate_ref, sh_ref, h_ref, lg_ref, lb_ref, ys_hbm, o_ref,
                    buf, sem):
    def issue(n, c):
        for kk in range(TOP_K):
            src = ps_ref[idx_ref[kk, n]] + rank_ref[kk, n]
            _tile_copy(ys_hbm, src, buf.at[kk], n, sem).start(priority=kk % DMA_QUEUES)
        return c

    lax.fori_loop(0, COMB_TM, issue, 0)

    def drain(n, c):
        for kk in range(TOP_K):
            _tile_copy(ys_hbm, 0, buf.at[kk], n, sem).wait()
        return c

    lax.fori_loop(0, COMB_TM, drain, 0)
    groups = []
    for s in range(ROW_TILES):
        acc = None
        for kk in range(TOP_K):
            part = gate_ref[kk] * buf[kk, pl.ds(s, COMB_TM, stride=ROW_TILES), :]
            acc = part if kk == 0 else acc + part
        groups.append(acc)
    ffn = sh_ref[...] + jnp.concatenate(groups, axis=1)
    o_ref[...] = _layer_norm(DN_ALPHA * h_ref[...] + ffn, lg_ref[...], lb_ref[...])


def _combine(ys, idx_t, rank_t, pad_start, gate, shared, h, lg, lb):
    Np, D = h.shape
    TM = COMB_TM
    tok = pl.BlockSpec((TOP_K, TM), lambda i, ps: (0, i), memory_space=pltpu.SMEM)
    tile = pl.BlockSpec((TM, D), lambda i, ps: (i, 0))
    vec = pl.BlockSpec((1, D), lambda i, ps: (0, 0))
    return pl.pallas_call(
        _combine_kernel,
        out_shape=jax.ShapeDtypeStruct((Np, D), F32),
        grid_spec=pltpu.PrefetchScalarGridSpec(
            num_scalar_prefetch=1,
            grid=(Np // TM,),
            in_specs=[tok, tok, pl.BlockSpec((TOP_K, TM, LANES), lambda i, ps: (0, i, 0)), tile, tile, vec, vec,
                      pl.BlockSpec(memory_space=pl.ANY)],
            out_specs=tile,
            scratch_shapes=[pltpu.VMEM((TOP_K, TM * ROW_TILES, LANES), F32), pltpu.SemaphoreType.DMA(())],
        ),
        compiler_params=_cparams(("arbitrary",)),
        name="moe_combine",
    )(pad_start, idx_t, rank_t, gate, shared, h, lg, lb, ys)


def _moe_layer(h2, l, lg, lb, router_w, router_b, w_gate, w_up, w_down, sh_gate, sh_up, sh_down, tri):
    Np, D = h2.shape
    shared, idx_t, gate_t, rank_t, cnt = _route_shared(
        h2, router_w[l].T, router_b[l].reshape(N_EXPERTS, 1),
        sh_gate[l].astype(BF16), sh_up[l].astype(BF16), sh_down[l].astype(BF16), tri)
    counts = cnt[:, 0].astype(I32)
    padded = (counts + MOE_BLK - 1) // MOE_BLK * MOE_BLK
    pad_end = jnp.cumsum(padded)
    pad_start = (pad_end - padded).astype(I32)
    n_blocks = Np * TOP_K // MOE_BLK + N_EXPERTS
    n_used = (pad_end[-1:] // MOE_BLK).astype(I32)
    blk_row0 = jnp.arange(n_blocks, dtype=I32) * MOE_BLK
    block_e = jnp.minimum(jnp.sum((pad_end[None, :] <= blk_row0[:, None]).astype(I32), axis=1), N_EXPERTS - 1)
    xs = _dispatch(h2, idx_t, rank_t, pad_start, n_blocks * MOE_BLK)
    ys = _experts(xs, block_e, n_used, w_gate[l].astype(BF16), w_up[l].astype(BF16), w_down[l].astype(BF16))
    gate_splat = jnp.broadcast_to(gate_t[:, :, None], (TOP_K, Np, LANES))
    return _combine(ys, idx_t, rank_t, pad_start, gate_splat, shared, h2, lg, lb)


def _pad_cols(w, n):
    return jnp.pad(w, ((0, 0), (0, n - w.shape[1])))


def _pad_rows(w, n):
    return jnp.pad(w, ((0, n - w.shape[0]), (0, 0)))


def _trunk(x, meta_tokens, ln_mix_g, ln_mix_b, ln_ffn_g, ln_ffn_b,
           rw_mu, rw_w_rkv, rw_w0, rw_w_l1, rw_w_l2, rw_a0, rw_a_l1, rw_a_l2,
           rw_g_l1, rw_g_l2, rw_k_k, rw_k_a, rw_r_k, rw_lnx_g, rw_lnx_b, rw_w_out,
           rw_v0, rw_v_l1, rw_v_l2, kv_w,
           da_w_q, da_lam_q1, da_lam_k1, da_lam_q2, da_lam_k2, da_subln_g, da_w_out,
           moe_router_w, moe_router_b, moe_w_gate, moe_w_up, moe_w_down,
           moe_sh_gate, moe_sh_up, moe_sh_down):
    B, S, D = x.shape
    L = S + N_META
    Lp = -(-L // SEQ_ALIGN) * SEQ_ALIGN
    Np = B * Lp
    assert D == D_MODEL and Np % ROW_TM == 0 and Lp % PROJ_TM == 0 and Lp % SCAN_TC == 0 and Lp % ATT_T == 0
    meta = jnp.broadcast_to(meta_tokens[None].astype(x.dtype), (B, N_META, D))
    h = jnp.concatenate([meta, x, jnp.zeros((B, Lp - L, D), x.dtype)], axis=1)

    pc = _perm_cols()
    gs_np, gb_np = _head_sum_mats()
    gs, gb = jnp.asarray(gs_np), jnp.asarray(gb_np)
    tri = jnp.asarray(np.triu(np.ones((ROUTE_TM, ROUTE_TM), np.float32), 1)).astype(BF16)
    row = lambda vec: vec.reshape(1, -1)
    lane = np.arange(2 * D_MODEL)
    m_key, m_head = lane // 32, lane % 16
    c128 = np.arange(LANES) % 16
    vrow = np.arange(RWKV_HEAD)

    v_first = None
    kv = None
    for l in range(DEPTH):
        if l < N_A_LAYERS:
            p = {
                "mu": _pad_rows(rw_mu[l], SUBLANES),
                "wr": rw_w_rkv[l, 0][:, pc].astype(BF16),
                "wk": rw_w_rkv[l, 1][:, pc].astype(BF16),
                "wv": rw_w_rkv[l, 2][:, pc].astype(BF16),
                "w0": row(rw_w0[l][pc]),
                "wl1": _pad_cols(rw_w_l1[l], LANES).astype(BF16),
                "wl2": _pad_rows(rw_w_l2[l][:, pc], LANES).astype(BF16),
                "a0": row(rw_a0[l][pc]),
                "al1": _pad_cols(rw_a_l1[l], LANES).astype(BF16),
                "al2": _pad_rows(rw_a_l2[l][:, pc], LANES).astype(BF16),
                "gl1": _pad_cols(rw_g_l1[l], 2 * LANES).astype(BF16),
                "gl2": _pad_rows(rw_g_l2[l][:, pc], 2 * LANES).astype(BF16),
                "kk": row(rw_k_k[l][pc]),
                "ka": row(rw_k_a[l][pc]),
                "gs": gs,
                "gb": gb,
            }
            if l > 0:
                p["v0"] = row(rw_v0[l - 1][pc])
                p["vl1"] = _pad_cols(rw_v_l1[l - 1], LANES).astype(BF16)
                p["vl2"] = _pad_rows(rw_v_l2[l - 1][:, pc], LANES).astype(BF16)
            r, w, k, v, a, b, g = _rwkv_proj(h, v_first if l > 0 else None, p)
            if l == 0:
                v_first = v
            rk_m = rw_r_k[l][m_head, m_key].reshape(1, 2 * D_MODEL)
            lg_t = rw_lnx_g[l].reshape(RWKV_HEADS, RWKV_HEAD)[c128[None, :], vrow[:, None]]
            lb_t = rw_lnx_b[l].reshape(RWKV_HEADS, RWKV_HEAD)[c128[None, :], vrow[:, None]]
            z = _wkv_scan(r, w, k, v, a, b, rk_m, lg_t, lb_t)
            h2 = _mm_res_ln(z.reshape(Np, D), g.reshape(Np, D), rw_w_out[l][pc, :].astype(BF16),
                            h.reshape(Np, D), row(ln_mix_g[l]), row(ln_mix_b[l]))
        else:
            j = l - N_A_LAYERS
            h2 = h.reshape(Np, D)
            if kv is None:
                kv = _proj(h2, kv_w.astype(BF16)).reshape(B, Lp, 2 * D)
            q_scale = DIFF_HD ** -0.5 * math.log2(math.e)
            q = _proj(h2, (da_w_q[j] * q_scale).astype(BF16)).reshape(B, Lp, D)
            lam_init = 0.8 - 0.6 * math.exp(-0.3 * l)
            lam4 = jnp.stack([da_lam_q1[j], da_lam_k1[j], da_lam_q2[j], da_lam_k2[j]])
            o = _diff_attn(q, kv, _pad_rows(lam4, SUBLANES), row(da_subln_g[j]), lam_init)
            h2 = _mm_res_ln(o.reshape(Np, D), None, da_w_out[j].astype(BF16), h2,
                            row(ln_mix_g[l]), row(ln_mix_b[l]))
        h2 = _moe_layer(h2, l, row(ln_ffn_g[l]), row(ln_ffn_b[l]), moe_router_w, moe_router_b,
                        moe_w_gate, moe_w_up, moe_w_down, moe_sh_gate, moe_sh_up, moe_sh_down, tri)
        h = h2.reshape(B, Lp, D)
    return h[:, N_META:L]


_trunk_jit = jax.jit(_trunk)


def kernel(x, meta_tokens, ln_mix_g, ln_mix_b, ln_ffn_g, ln_ffn_b, rw_mu, rw_w_rkv, rw_w0, rw_w_l1, rw_w_l2, rw_a0, rw_a_l1, rw_a_l2, rw_g_l1, rw_g_l2, rw_k_k, rw_k_a, rw_r_k, rw_lnx_g, rw_lnx_b, rw_w_out, rw_v0, rw_v_l1, rw_v_l2, kv_w, da_w_q, da_lam_q1, da_lam_k1, da_lam_q2, da_lam_k2, da_subln_g, da_w_out, moe_router_w, moe_router_b, moe_w_gate, moe_w_up, moe_w_down, moe_sh_gate, moe_sh_up, moe_sh_down):
    return _trunk_jit(x, meta_tokens, ln_mix_g, ln_mix_b, ln_ffn_g, ln_ffn_b, rw_mu, rw_w_rkv, rw_w0, rw_w_l1,
                      rw_w_l2, rw_a0, rw_a_l1, rw_a_l2, rw_g_l1, rw_g_l2, rw_k_k, rw_k_a, rw_r_k, rw_lnx_g,
                      rw_lnx_b, rw_w_out, rw_v0, rw_v_l1, rw_v_l2, kv_w, da_w_q, da_lam_q1, da_lam_k1,
                      da_lam_q2, da_lam_k2, da_subln_g, da_w_out, moe_router_w, moe_router_b, moe_w_gate,
                      moe_w_up, moe_w_down, moe_sh_gate, moe_sh_up, moe_sh_down)
```

```python
import functools
import math

import numpy as np
import jax
import jax.numpy as jnp
from jax import lax
from jax.experimental import pallas as pl
from jax.experimental.pallas import tpu as pltpu

F32 = jnp.float32
BF16 = jnp.bfloat16
I32 = jnp.int32

D_MODEL = 1024
DEPTH = 4
N_META = 16
N_A_LAYERS = DEPTH // 2
RWKV_HEAD = 64
RWKV_HEADS = D_MODEL // RWKV_HEAD
RWKV_GN_EPS = 64e-5
DIFF_HEADS = 8
DIFF_HD = D_MODEL // (2 * DIFF_HEADS)
N_EXPERTS = 64
TOP_K = 8
N_GROUPS = 8
TOPK_GROUPS = 4
EXPERT_FF = D_MODEL // 4
ROUTED_SCALE = 2.5
DN_ALPHA = (2 * DEPTH) ** 0.25
LN_EPS = 1e-5

LANES = 128
SUBLANES = 8
VMEM_LIMIT_BYTES = 56 * 1024 * 1024
DMA_QUEUES = 2
ROW_TILES = D_MODEL // LANES

SEQ_ALIGN = 256
PROJ_TM = 256
ROW_TM = 512
SCAN_TC = 128
SCAN_SUB = SUBLANES
ATT_T = 256
ROUTE_TM = 256
MOE_BLK = 512
DISP_TM = 512
COMB_TM = 256

_SIGMA = (0, 4, 1, 5, 2, 6, 3, 7)


def _cparams(sem):
    return pltpu.CompilerParams(dimension_semantics=sem, vmem_limit_bytes=VMEM_LIMIT_BYTES)


def _dot(a, b):
    return jnp.dot(a, b, preferred_element_type=F32)


def _dot_hi(a, b):
    return jnp.dot(a, b, preferred_element_type=F32, precision=lax.Precision.HIGHEST)


def _full(shape):
    nd = len(shape)
    return pl.BlockSpec(shape, lambda *_: (0,) * nd)


def _layer_norm(y, g, b):
    mu = jnp.mean(y, axis=-1, keepdims=True)
    d = y - mu
    var = jnp.mean(d * d, axis=-1, keepdims=True)
    return d * lax.rsqrt(var + LN_EPS) * g + b


def _perm_cols():
    L = np.arange(D_MODEL)
    j, p, hh = L // LANES, (L % LANES) // 16, L % 16
    i = 8 * j + np.asarray(_SIGMA)[p]
    return (hh * RWKV_HEAD + i).astype(np.int32)


def _head_sum_mats():
    L = np.arange(D_MODEL)
    c = np.arange(LANES)
    gs = (L[:, None] % 16 == c[None, :] % 16).astype(np.float32)
    gb = ((c[:, None] < 16) & (c[:, None] == L[None, :] % 16)).astype(np.float32)
    return gs, gb


def _rwkv_proj_kernel(has_vres, *refs):
    if has_vres:
        (x_ref, xp_ref, mu_ref, wr_ref, wk_ref, wv_ref, w0_ref, wl1_ref, wl2_ref, a0_ref, al1_ref,
         al2_ref, gl1_ref, gl2_ref, kk_ref, ka_ref, gs_ref, gb_ref, vf_ref, v0_ref, vl1_ref, vl2_ref,
         r_o, w_o, k_o, v_o, a_o, b_o, g_o) = refs
    else:
        (x_ref, xp_ref, mu_ref, wr_ref, wk_ref, wv_ref, w0_ref, wl1_ref, wl2_ref, a0_ref, al1_ref,
         al2_ref, gl1_ref, gl2_ref, kk_ref, ka_ref, gs_ref, gb_ref,
         r_o, w_o, k_o, v_o, a_o, b_o, g_o) = refs
    i = pl.program_id(1)
    x = x_ref[0]
    prev = xp_ref[0][SUBLANES - 1:SUBLANES, :]
    prev = jnp.where(i == 0, 0.0, prev)
    row = lax.broadcasted_iota(I32, x.shape, 0)
    xprev = jnp.where(row == 0, prev, pltpu.roll(x, 1, 0))
    dx = xprev - x

    def mix(s):
        return (x + dx * mu_ref[s:s + 1, :]).astype(BF16)

    xv = mix(2)
    r = _dot(mix(0), wr_ref[...])
    k = _dot(mix(1), wk_ref[...])
    v = _dot(xv, wv_ref[...])
    zw = w0_ref[...] + _dot(jnp.tanh(_dot(mix(3), wl1_ref[...])).astype(BF16), wl2_ref[...])
    decay = jnp.exp(-math.exp(-0.5) * jax.nn.sigmoid(zw))
    a = jax.nn.sigmoid(a0_ref[...] + _dot(_dot(mix(4), al1_ref[...]).astype(BF16), al2_ref[...]))
    g = _dot(jax.nn.sigmoid(_dot(mix(5), gl1_ref[...])).astype(BF16), gl2_ref[...])
    if has_vres:
        gate_v = jax.nn.sigmoid(v0_ref[...] + _dot(_dot(xv, vl1_ref[...]).astype(BF16), vl2_ref[...]))
        v = v + (vf_ref[0] - v) * gate_v
    kk = k * kk_ref[...]
    ss = _dot_hi(kk * kk, gs_ref[...])
    inv = 1.0 / jnp.maximum(jnp.sqrt(ss), 1e-12)
    kk = kk * _dot_hi(inv, gb_ref[...])
    k = k * (1.0 + (a - 1.0) * ka_ref[...])
    r_o[0] = r
    w_o[0] = decay
    k_o[0] = k
    v_o[0] = v
    a_o[0] = -kk
    b_o[0] = kk * a
    g_o[0] = g


def _rwkv_proj(h, vfirst, p):
    B, Lp, D = h.shape
    TM = PROJ_TM
    has_vres = vfirst is not None
    tile = pl.BlockSpec((1, TM, D), lambda b, i: (b, i, 0))
    prev8 = pl.BlockSpec((1, SUBLANES, D), lambda b, i: (b, jnp.maximum(i * (TM // SUBLANES) - 1, 0), 0))
    names = ["mu", "wr", "wk", "wv", "w0", "wl1", "wl2", "a0", "al1", "al2", "gl1", "gl2", "kk", "ka", "gs", "gb"]
    args = [h, h] + [p[n] for n in names]
    specs = [tile, prev8] + [_full(p[n].shape) for n in names]
    if has_vres:
        args += [vfirst, p["v0"], p["vl1"], p["vl2"]]
        specs += [tile, _full(p["v0"].shape), _full(p["vl1"].shape), _full(p["vl2"].shape)]
    out = jax.ShapeDtypeStruct((B, Lp, D), F32)
    return pl.pallas_call(
        functools.partial(_rwkv_proj_kernel, has_vres),
        out_shape=[out] * 7,
        grid=(B, Lp // TM),
        in_specs=specs,
        out_specs=[tile] * 7,
        compiler_params=_cparams(("parallel", "arbitrary")),
        name="rwkv_proj",
    )(*args)


def _wkv_kernel(r_ref, w_ref, k_ref, v_ref, a_ref, b_ref, rk_ref, lg_ref, lb_ref, o_ref,
                s_ref, ma_ref, mwr_ref, mw_ref, mb_ref, mk_ref, zv_ref, br_ref, kr_ref, rkk_ref, yc_ref):
    NV = RWKV_HEAD
    NCOL = 2 * D_MODEL // LANES
    NRB = NV // SUBLANES

    @pl.when(pl.program_id(0) == 0)
    def _():
        s_ref[...] = jnp.zeros_like(s_ref)

    lane = lax.broadcasted_iota(I32, (SUBLANES, LANES), 1)
    even = ((lane // 16) % 2) == 0
    grp = lane // 32

    def cs(j):
        return slice(j * LANES, (j + 1) * LANES)

    def merge(ref, t0):
        x0 = ref[0, pl.ds(t0, SUBLANES), :]
        x1 = ref[1, pl.ds(t0, SUBLANES), :]
        cols = []
        for j in range(D_MODEL // LANES):
            a0 = x0[:, cs(j)]
            a1 = x1[:, cs(j)]
            cols.append(jnp.where(even, a0, pltpu.roll(a1, 16, 1)))
            cols.append(jnp.where(even, pltpu.roll(a0, LANES - 16, 1), a1))
        return cols

    def fold(x):
        return (x + pltpu.roll(x, 32, 1)) + (pltpu.roll(x, 64, 1) + pltpu.roll(x, 96, 1))

    def colsum(xs):
        acc = xs[0]
        for x in xs[1:]:
            acc = acc + x
        return acc

    def sub_chunk(c, carry):
        t0 = pl.multiple_of(c * SCAN_SUB, SCAN_SUB)
        R = merge(r_ref, t0)
        W = merge(w_ref, t0)
        K = merge(k_ref, t0)
        V = merge(v_ref, t0)
        A = merge(a_ref, t0)
        Bm = merge(b_ref, t0)
        for j in range(NCOL):
            ma_ref[j] = A[j]
            mw_ref[j] = W[j]
            mwr_ref[j] = W[j] * R[j]
            mb_ref[j] = Bm[j]
            mk_ref[j] = K[j]
        br_ref[...] = fold(colsum([Bm[j] * R[j] for j in range(NCOL)]))
        kr_ref[...] = fold(colsum([K[j] * R[j] for j in range(NCOL)]))
        rkk_ref[...] = fold(colsum([K[j] * R[j] * rk_ref[:, cs(j)] for j in range(NCOL)]))
        for j in range(NCOL):
            rolled = [V[j]] + [pltpu.roll(V[j], 32 * m, 1) for m in range(1, 4)]
            for q in range(4):
                z = rolled[(3 - q) % 4]
                for pos in range(2, -1, -1):
                    z = jnp.where(grp == pos, rolled[(pos - q) % 4], z)
                zv_ref[pl.ds(4 * j + q, SUBLANES, stride=NV), :] = z

        def pair_sum(xs):
            return colsum(xs[0::2]) + colsum(xs[1::2])

        def step(t):
            trow = slice(t, t + 1)
            base = t * NV
            brt = br_ref[trow, :]
            krt = kr_ref[trow, :]
            rkt = rkk_ref[trow, :]
            vcol, y = [], []
            for i in range(NRB):
                rows = slice(i * SUBLANES, (i + 1) * SUBLANES)
                srow = [s_ref[rows, cs(j)] for j in range(NCOL)]
                sa = fold(pair_sum([srow[j] * ma_ref[j, trow, :] for j in range(NCOL)]))
                yp = fold(pair_sum([srow[j] * mwr_ref[j, trow, :] for j in range(NCOL)]))
                vc = zv_ref[pl.ds(base + i * SUBLANES, SUBLANES), :]
                for j in range(NCOL):
                    s_ref[rows, cs(j)] = (srow[j] * mw_ref[j, trow, :] + sa * mb_ref[j, trow, :]
                                          + vc * mk_ref[j, trow, :])
                vcol.append(vc)
                y.append(yp + sa * brt + vc * krt)
            mean = jnp.sum(colsum(y), axis=0, keepdims=True) * (1.0 / NV)
            d = [yi - mean for yi in y]
            var = jnp.sum(colsum([di * di for di in d]), axis=0, keepdims=True) * (1.0 / NV)
            inv = lax.rsqrt(var + RWKV_GN_EPS)
            for i in range(NRB):
                rows = slice(i * SUBLANES, (i + 1) * SUBLANES)
                out = d[i] * inv * lg_ref[rows, :] + lb_ref[rows, :] + vcol[i] * rkt
                yc_ref[pl.ds(base + i * SUBLANES, SUBLANES), :] = out

        for t in range(SCAN_SUB):
            step(t)

        cols = []
        for j in range(NCOL):
            acc = None
            for q in range(4):
                z = yc_ref[pl.ds(4 * j + q, SUBLANES, stride=NV), :]
                acc = z if q == 0 else jnp.where(grp == q, z, acc)
            cols.append(acc)
        for j in range(D_MODEL // LANES):
            c0, c1 = cols[2 * j], cols[2 * j + 1]
            o_ref[0, pl.ds(t0, SUBLANES), cs(j)] = jnp.where(even, c0, pltpu.roll(c1, 16, 1))
            o_ref[1, pl.ds(t0, SUBLANES), cs(j)] = jnp.where(even, pltpu.roll(c0, LANES - 16, 1), c1)
        return carry

    lax.fori_loop(0, SCAN_TC // SCAN_SUB, sub_chunk, 0)


def _wkv_scan(r, w, k, v, a, b, rk_m, lg_t, lb_t):
    B, Lp, D = r.shape
    assert B == 2, "the scan packs exactly two batch rows into the lane dimension"
    blk = pl.BlockSpec((B, SCAN_TC, D), lambda i: (0, i, 0))
    vm = lambda *s: pltpu.VMEM(s, F32)
    return pl.pallas_call(
        _wkv_kernel,
        out_shape=jax.ShapeDtypeStruct((B, Lp, D), F32),
        grid=(Lp // SCAN_TC,),
        in_specs=[blk] * 6 + [_full(rk_m.shape), _full(lg_t.shape), _full(lb_t.shape)],
        out_specs=blk,
        scratch_shapes=[vm(RWKV_HEAD, 2 * D)] + [vm(2 * D // LANES, SCAN_SUB, LANES)] * 5
        + [vm(SCAN_SUB * RWKV_HEAD, LANES)] + [vm(SCAN_SUB, LANES)] * 3
        + [vm(SCAN_SUB * RWKV_HEAD, LANES)],
        compiler_params=_cparams(("arbitrary",)),
        name="wkv_scan",
    )(r, w, k, v, a, b, rk_m, lg_t, lb_t)


def _mm_res_ln_kernel(has_gate, *refs):
    if has_gate:
        z_ref, g_ref, w_ref, h_ref, lg_ref, lb_ref, o_ref = refs
        z = (z_ref[...] * g_ref[...]).astype(BF16)
    else:
        z_ref, w_ref, h_ref, lg_ref, lb_ref, o_ref = refs
        z = z_ref[...].astype(BF16)
    y = DN_ALPHA * h_ref[...] + _dot(z, w_ref[...])
    o_ref[...] = _layer_norm(y, lg_ref[...], lb_ref[...])


def _mm_res_ln(z, gate, w, h, lg, lb):
    Np, D = h.shape
    TM = ROW_TM
    tile = pl.BlockSpec((TM, D), lambda i: (i, 0))
    has_gate = gate is not None
    args = [z] + ([gate] if has_gate else []) + [w, h, lg, lb]
    specs = [tile] + ([tile] if has_gate else []) + [_full(w.shape), tile, _full(lg.shape), _full(lb.shape)]
    return pl.pallas_call(
        functools.partial(_mm_res_ln_kernel, has_gate),
        out_shape=jax.ShapeDtypeStruct((Np, D), F32),
        grid=(Np // TM,),
        in_specs=specs,
        out_specs=tile,
        compiler_params=_cparams(("parallel",)),
        name="mm_res_ln",
    )(*args)


def _proj_kernel(x_ref, w_ref, o_ref):
    o_ref[...] = _dot(x_ref[...].astype(BF16), w_ref[...]).astype(o_ref.dtype)


def _proj(x, w):
    Np, D = x.shape
    Nout = w.shape[1]
    TM = ROW_TM
    return pl.pallas_call(
        _proj_kernel,
        out_shape=jax.ShapeDtypeStruct((Np, Nout), BF16),
        grid=(Np // TM,),
        in_specs=[pl.BlockSpec((TM, D), lambda i: (i, 0)), _full(w.shape)],
        out_specs=pl.BlockSpec((TM, Nout), lambda i: (i, 0)),
        compiler_params=_cparams(("parallel",)),
        name="proj",
    )(x, w)


def _attn_kernel(lam_init, lam_ref, sg_ref, q_ref, k_ref, v_ref, o_ref,
                 sa_ref, sb_ref, m1_ref, l1_ref, acc1_ref, m2_ref, l2_ref, acc2_ref):
    T = ATT_T
    qi = pl.program_id(2)
    last = pl.num_programs(2) - 1
    q = q_ref[0]
    lane = lax.broadcasted_iota(I32, q.shape, 1)
    zero = jnp.zeros_like(q)
    q1 = jnp.where(lane < DIFF_HD, q, zero)
    q2 = jnp.where(lane >= DIFF_HD, q, zero)
    m1_ref[...] = jnp.full_like(m1_ref, -1e30)
    m2_ref[...] = jnp.full_like(m2_ref, -1e30)
    l1_ref[...] = jnp.zeros_like(l1_ref)
    l2_ref[...] = jnp.zeros_like(l2_ref)
    acc1_ref[...] = jnp.zeros_like(acc1_ref)
    acc2_ref[...] = jnp.zeros_like(acc2_ref)
    nt = (((1,), (1,)), ((), ()))
    ones = jnp.ones((T, LANES), BF16)
    rel = lax.broadcasted_iota(I32, (T, T), 1) - lax.broadcasted_iota(I32, (T, T), 0)

    def chunk_start(kc):
        return pl.multiple_of(jnp.minimum(kc, last) * T, T)

    def halves(x):
        return [x[:, c * LANES:(c + 1) * LANES] for c in range(x.shape[1] // LANES)]

    def scores(g, buf):
        for half in range(2):
            kb = k_ref[0, pl.ds(chunk_start(2 * g + half), T), :]
            buf[0, :, half * T:(half + 1) * T] = lax.dot_general(q1, kb, nt, preferred_element_type=F32)
            buf[1, :, half * T:(half + 1) * T] = lax.dot_general(q2, kb, nt, preferred_element_type=F32)

    def consume(g, buf, masked):
        vext = jnp.concatenate(
            [jnp.concatenate([v_ref[0, pl.ds(chunk_start(2 * g + half), T), :], ones], axis=1) for half in range(2)],
            axis=0)
        for si, m_ref, l_ref, acc_ref in ((0, m1_ref, l1_ref, acc1_ref), (1, m2_ref, l2_ref, acc2_ref)):
            s = buf[si]
            if masked:
                s = jnp.concatenate(
                    [jnp.where(rel <= (qi - (2 * g + half)) * T, s[:, half * T:(half + 1) * T], -jnp.inf)
                     for half in range(2)], axis=1)
            parts = halves(s)
            smax = jnp.max(jnp.maximum(jnp.maximum(parts[0], parts[1]), jnp.maximum(parts[2], parts[3])),
                           axis=1, keepdims=True)
            m_old = m_ref[...]
            m_new = jnp.maximum(m_old, smax)
            alpha = jnp.exp2(m_old - m_new)
            p = jnp.exp2(s - jnp.concatenate([m_new] * len(parts), axis=1)).astype(BF16)
            d = _dot(p, vext)
            acc_ref[...] = alpha * acc_ref[...] + d[:, :LANES]
            l_ref[...] = alpha * l_ref[...] + d[:, LANES:]
            m_ref[...] = m_new

    n_free = qi // 2
    scores(0, sa_ref)

    def pair(i, carry):
        g = 2 * i
        scores(g + 1, sb_ref)
        consume(g, sa_ref, False)
        scores(g + 2, sa_ref)
        consume(g + 1, sb_ref, False)
        return carry

    n_pairs = n_free // 2
    lax.fori_loop(0, n_pairs, pair, 0)
    g_a = 2 * n_pairs
    odd = n_free % 2 == 1

    @pl.when(odd)
    def _():
        scores(g_a + 1, sb_ref)

    consume(g_a, sa_ref, True)

    @pl.when(odd)
    def _():
        consume(g_a + 1, sb_ref, True)


    lam_v = lam_ref[...]
    lam = (jnp.exp(jnp.sum(lam_v[0:1] * lam_v[1:2], axis=1, keepdims=True))
           - jnp.exp(jnp.sum(lam_v[2:3] * lam_v[3:4], axis=1, keepdims=True)) + lam_init)
    o = acc1_ref[...] / l1_ref[...] - lam * (acc2_ref[...] / l2_ref[...])
    o = o * lax.rsqrt(jnp.mean(o * o, axis=1, keepdims=True) + 1e-5) * sg_ref[...] * (1.0 - lam_init)
    o_ref[0] = o.astype(o_ref.dtype)


def _diff_attn(q, kv, lam4, subln_g, lam_init):
    B, Lp, D = q.shape
    T = ATT_T
    H = DIFF_HEADS
    HW = 2 * DIFF_HD
    vm = lambda *s: pltpu.VMEM(s, F32)
    return pl.pallas_call(
        functools.partial(_attn_kernel, lam_init),
        out_shape=jax.ShapeDtypeStruct((B, Lp, D), BF16),
        grid=(B, H, Lp // T),
        in_specs=[_full(lam4.shape), _full(subln_g.shape),
                  pl.BlockSpec((1, T, HW), lambda b, h, i: (b, i, h)),
                  pl.BlockSpec((1, Lp, HW), lambda b, h, i: (b, 0, h)),
                  pl.BlockSpec((1, Lp, HW), lambda b, h, i: (b, 0, H + h))],
        out_specs=pl.BlockSpec((1, T, HW), lambda b, h, i: (b, i, h)),
        scratch_shapes=[vm(2, T, 2 * T), vm(2, T, 2 * T),
                        vm(T, LANES), vm(T, LANES), vm(T, HW), vm(T, LANES), vm(T, LANES), vm(T, HW)],
        compiler_params=_cparams(("parallel", "parallel", "arbitrary")),
        name="diff_attn",
    )(lam4, subln_g, q, kv, kv)


def _route_kernel(x_ref, rw_ref, rb_ref, sg_ref, su_ref, sd_ref, tri_ref,
                  sh_o, idx_o, gate_o, rank_o, cnt_o, carry_ref):
    TM = ROUTE_TM
    G, EG = N_GROUPS, N_EXPERTS // N_GROUPS

    @pl.when(pl.program_id(0) == 0)
    def _():
        carry_ref[...] = jnp.zeros_like(carry_ref)

    x = x_ref[...]
    xb = x.astype(BF16)
    hmid = _dot(xb, sg_ref[...])
    hmid = hmid * jax.nn.sigmoid(hmid) * _dot(xb, su_ref[...])
    sh_o[...] = _dot(hmid.astype(BF16), sd_ref[...])

    logit = lax.dot_general(rw_ref[...], x, (((1,), (1,)), ((), ())),
                            preferred_element_type=F32, precision=lax.Precision.HIGHEST)
    s = jax.nn.sigmoid(logit)
    s3 = s.reshape(G, EG, TM)
    sel3 = (s + rb_ref[...]).reshape(G, EG, TM)
    io_j = lax.broadcasted_iota(I32, (G, EG, TM), 1)
    io_g = lax.broadcasted_iota(I32, (G, 1, TM), 0)
    neg = -jnp.inf
    m1 = jnp.max(sel3, axis=1, keepdims=True)
    i1 = jnp.min(jnp.where(sel3 == m1, io_j, EG), axis=1, keepdims=True)
    m2 = jnp.max(jnp.where(io_j == i1, neg, sel3), axis=1, keepdims=True)
    gsc = m1 + m2
    gkeep = jnp.zeros((G, 1, TM), F32)
    for _ in range(TOPK_GROUPS):
        m = jnp.max(gsc, axis=0, keepdims=True)
        gi = jnp.min(jnp.where(gsc == m, io_g, G), axis=0, keepdims=True)
        hit = io_g == gi
        gkeep = jnp.where(hit, 1.0, gkeep)
        gsc = jnp.where(hit, neg, gsc)
    cur = jnp.where(gkeep > 0.0, sel3, neg)
    io_e = io_g * EG + io_j
    hits, idxs, ws = [], [], []
    for _ in range(TOP_K):
        m = jnp.max(jnp.max(cur, axis=1, keepdims=True), axis=0, keepdims=True)
        ei = jnp.min(jnp.min(jnp.where(cur == m, io_e, N_EXPERTS), axis=1, keepdims=True), axis=0, keepdims=True)
        hit = io_e == ei
        ws.append(jnp.sum(jnp.sum(jnp.where(hit, s3, 0.0), axis=1, keepdims=True), axis=0, keepdims=True))
        cur = jnp.where(hit, neg, cur)
        hits.append(hit)
        idxs.append(ei)
    wsum = ws[0]
    for wv in ws[1:]:
        wsum = wsum + wv
    scale = ROUTED_SCALE / wsum
    onehot = jnp.zeros((G, EG, TM), F32)
    for hit in hits:
        onehot = jnp.where(hit, 1.0, onehot)
    oh2 = onehot.reshape(N_EXPERTS, TM)
    rank_full = (_dot(oh2.astype(BF16), tri_ref[...]) + carry_ref[:, :1]).reshape(G, EG, TM)
    for kk in range(TOP_K):
        rk = jnp.sum(jnp.sum(jnp.where(hits[kk], rank_full, 0.0), axis=1, keepdims=True), axis=0, keepdims=True)
        idx_o[kk:kk + 1, :] = idxs[kk].reshape(1, TM)
        gate_o[kk:kk + 1, :] = (ws[kk] * scale).reshape(1, TM)
        rank_o[kk:kk + 1, :] = rk.reshape(1, TM).astype(I32)
    carry_ref[...] = carry_ref[...] + jnp.sum(oh2, axis=1, keepdims=True)
    cnt_o[...] = carry_ref[...]


def _route_shared(x, rw_t, rb, sg, su, sd, tri):
    Np, D = x.shape
    TM = ROUTE_TM
    tok = pl.BlockSpec((TOP_K, TM), lambda i: (0, i))
    return pl.pallas_call(
        _route_kernel,
        out_shape=[jax.ShapeDtypeStruct((Np, D), F32),
                   jax.ShapeDtypeStruct((TOP_K, Np), I32),
                   jax.ShapeDtypeStruct((TOP_K, Np), F32),
                   jax.ShapeDtypeStruct((TOP_K, Np), I32),
                   jax.ShapeDtypeStruct((N_EXPERTS, LANES), F32)],
        grid=(Np // TM,),
        in_specs=[pl.BlockSpec((TM, D), lambda i: (i, 0)), _full(rw_t.shape), _full(rb.shape),
                  _full(sg.shape), _full(su.shape), _full(sd.shape), _full(tri.shape)],
        out_specs=[pl.BlockSpec((TM, D), lambda i: (i, 0)), tok, tok, tok, _full((N_EXPERTS, LANES))],
        scratch_shapes=[pltpu.VMEM((N_EXPERTS, LANES), F32)],
        compiler_params=_cparams(("arbitrary",)),
        name="moe_route",
    )(x, rw_t, rb, sg, su, sd, tri)


def _tile_copy(src_ref, src_row, dst_ref, dst_row, sem):
    src = src_ref.at[pl.ds(pl.multiple_of(src_row * ROW_TILES, ROW_TILES), ROW_TILES)]
    dst = dst_ref.at[pl.ds(pl.multiple_of(dst_row * ROW_TILES, ROW_TILES), ROW_TILES)]
    return pltpu.make_async_copy(src, dst, sem)


def _rows_to_tiles(x, tiles_ref, n_rows):
    for s in range(ROW_TILES):
        tiles_ref[pl.ds(s, n_rows, stride=ROW_TILES), :] = x[:, s * LANES:(s + 1) * LANES]


def _tiles_to_rows(tiles_ref, n_rows):
    return [tiles_ref[pl.ds(s, n_rows, stride=ROW_TILES), :] for s in range(ROW_TILES)]


def _dispatch_kernel(ps_ref, idx_ref, rank_ref, x_ref, xs_hbm, stage_ref, sem):
    _rows_to_tiles(x_ref[...], stage_ref, DISP_TM)

    def issue(n, c):
        for kk in range(TOP_K):
            dst = ps_ref[idx_ref[kk, n]] + rank_ref[kk, n]
            _tile_copy(stage_ref, n, xs_hbm, dst, sem).start(priority=kk % DMA_QUEUES)
        return c

    lax.fori_loop(0, DISP_TM, issue, 0)

    def drain(n, c):
        for kk in range(TOP_K):
            _tile_copy(stage_ref, n, xs_hbm, 0, sem).wait()
        return c

    lax.fori_loop(0, DISP_TM, drain, 0)


def _dispatch(x, idx_t, rank_t, pad_start, n_rows):
    Np, D = x.shape
    tok = pl.BlockSpec((TOP_K, DISP_TM), lambda i, ps: (0, i), memory_space=pltpu.SMEM)
    return pl.pallas_call(
        _dispatch_kernel,
        out_shape=jax.ShapeDtypeStruct((n_rows * ROW_TILES, LANES), F32),
        grid_spec=pltpu.PrefetchScalarGridSpec(
            num_scalar_prefetch=1,
            grid=(Np // DISP_TM,),
            in_specs=[tok, tok, pl.BlockSpec((DISP_TM, D), lambda i, ps: (i, 0))],
            out_specs=pl.BlockSpec(memory_space=pl.ANY),
            scratch_shapes=[pltpu.VMEM((DISP_TM * ROW_TILES, LANES), F32), pltpu.SemaphoreType.DMA(())],
        ),
        compiler_params=_cparams(("arbitrary",)),
        name="moe_dispatch",
    )(pad_start, idx_t, rank_t, x)


def _expert_kernel(be_ref, nu_ref, xs_ref, wg_ref, wu_ref, wd_ref, ys_ref):
    @pl.when(pl.program_id(0) < nu_ref[0])
    def _():
        x = jnp.concatenate(_tiles_to_rows(xs_ref, MOE_BLK), axis=1).astype(BF16)
        g = _dot(x, wg_ref[0])
        hmid = g * jax.nn.sigmoid(g) * _dot(x, wu_ref[0])
        _rows_to_tiles(_dot(hmid.astype(BF16), wd_ref[0]), ys_ref, MOE_BLK)


def _experts(xs, block_e, n_used, wg, wu, wd):
    D, F = wg.shape[1], wg.shape[2]
    nb = xs.shape[0] // (MOE_BLK * ROW_TILES)
    tiles = pl.BlockSpec((MOE_BLK * ROW_TILES, LANES), lambda i, be, nu: (jnp.minimum(i, nu[0] - 1), 0))
    wspec = lambda s: pl.BlockSpec((1,) + s, lambda i, be, nu: (be[jnp.minimum(i, nu[0] - 1)], 0, 0))
    return pl.pallas_call(
        _expert_kernel,
        out_shape=jax.ShapeDtypeStruct(xs.shape, F32),
        grid_spec=pltpu.PrefetchScalarGridSpec(
            num_scalar_prefetch=2,
            grid=(nb,),
            in_specs=[tiles, wspec((D, F)), wspec((D, F)), wspec((F, D))],
            out_specs=tiles,
        ),
        compiler_params=_cparams(("arbitrary",)),
        name="moe_experts",
    )(block_e, n_used, xs, wg, wu, wd)


def _combine_kernel(ps_ref, idx_ref, rank_ref, idx_nx_ref, rank_nx_ref, gate_ref, sh_ref, h_ref, lg_ref, lb_ref,
                    ys_hbm, o_ref, buf, sems):
    step = pl.program_id(0)
    slot = step % 2

    def issue(idx, rank, dst_slot):
        def body(n, c):
            for kk in range(TOP_K):
                src = ps_ref[idx[kk, n]] + rank[kk, n]
                _tile_copy(ys_hbm, src, buf.at[dst_slot, kk], n, sems.at[dst_slot]).start(priority=kk % DMA_QUEUES)
            return c

        lax.fori_loop(0, COMB_TM, body, 0)

    @pl.when(step == 0)
    def _():
        issue(idx_ref, rank_ref, 0)

    @pl.when(step + 1 < pl.num_programs(0))
    def _():
        issue(idx_nx_ref, rank_nx_ref, 1 - slot)

    def drain(n, c):
        for kk in range(TOP_K):
            _tile_copy(ys_hbm, 0, buf.at[slot, kk], n, sems.at[slot]).wait()
        return c

    lax.fori_loop(0, COMB_TM, drain, 0)
    groups = []
    for s in range(ROW_TILES):
        acc = None
        for kk in range(TOP_K):
            part = gate_ref[kk] * buf[slot, kk, pl.ds(s, COMB_TM, stride=ROW_TILES), :]
            acc = part if kk == 0 else acc + part
        groups.append(acc)
    ffn = sh_ref[...] + jnp.concatenate(groups, axis=1)
    o_ref[...] = _layer_norm(DN_ALPHA * h_ref[...] + ffn, lg_ref[...], lb_ref[...])


def _combine(ys, idx_t, rank_t, pad_start, gate, shared, h, lg, lb):
    Np, D = h.shape
    TM = COMB_TM
    n_steps = Np // TM
    tok = pl.BlockSpec((TOP_K, TM), lambda i, ps: (0, i), memory_space=pltpu.SMEM)
    tok_next = pl.BlockSpec((TOP_K, TM), lambda i, ps: (0, jnp.minimum(i + 1, n_steps - 1)), memory_space=pltpu.SMEM)
    tile = pl.BlockSpec((TM, D), lambda i, ps: (i, 0))
    vec = pl.BlockSpec((1, D), lambda i, ps: (0, 0))
    return pl.pallas_call(
        _combine_kernel,
        out_shape=jax.ShapeDtypeStruct((Np, D), F32),
        grid_spec=pltpu.PrefetchScalarGridSpec(
            num_scalar_prefetch=1,
            grid=(n_steps,),
            in_specs=[tok, tok, tok_next, tok_next, pl.BlockSpec((TOP_K, TM, LANES), lambda i, ps: (0, i, 0)),
                      tile, tile, vec, vec, pl.BlockSpec(memory_space=pl.ANY)],
            out_specs=tile,
            scratch_shapes=[pltpu.VMEM((2, TOP_K, TM * ROW_TILES, LANES), F32), pltpu.SemaphoreType.DMA((2,))],
        ),
        compiler_params=_cparams(("arbitrary",)),
        name="moe_combine",
    )(pad_start, idx_t, rank_t, idx_t, rank_t, gate, shared, h, lg, lb, ys)


def _moe_layer(h2, l, lg, lb, router_w, router_b, w_gate, w_up, w_down, sh_gate, sh_up, sh_down, tri):
    Np, D = h2.shape
    shared, idx_t, gate_t, rank_t, cnt = _route_shared(
        h2, router_w[l].T, router_b[l].reshape(N_EXPERTS, 1),
        sh_gate[l].astype(BF16), sh_up[l].astype(BF16), sh_down[l].astype(BF16), tri)
    counts = cnt[:, 0].astype(I32)
    padded = (counts + MOE_BLK - 1) // MOE_BLK * MOE_BLK
    pad_end = jnp.cumsum(padded)
    pad_start = (pad_end - padded).astype(I32)
    n_blocks = Np * TOP_K // MOE_BLK + N_EXPERTS
    n_used = (pad_end[-1:] // MOE_BLK).astype(I32)
    blk_row0 = jnp.arange(n_blocks, dtype=I32) * MOE_BLK
    block_e = jnp.minimum(jnp.sum((pad_end[None, :] <= blk_row0[:, None]).astype(I32), axis=1), N_EXPERTS - 1)
    xs = _dispatch(h2, idx_t, rank_t, pad_start, n_blocks * MOE_BLK)
    ys = _experts(xs, block_e, n_used, w_gate[l].astype(BF16), w_up[l].astype(BF16), w_down[l].astype(BF16))
    gate_splat = jnp.broadcast_to(gate_t[:, :, None], (TOP_K, Np, LANES))
    return _combine(ys, idx_t, rank_t, pad_start, gate_splat, shared, h2, lg, lb)


def _pad_cols(w, n):
    return jnp.pad(w, ((0, 0), (0, n - w.shape[1])))


def _pad_rows(w, n):
    return jnp.pad(w, ((0, n - w.shape[0]), (0, 0)))


def _trunk(x, meta_tokens, ln_mix_g, ln_mix_b, ln_ffn_g, ln_ffn_b,
           rw_mu, rw_w_rkv, rw_w0, rw_w_l1, rw_w_l2, rw_a0, rw_a_l1, rw_a_l2,
           rw_g_l1, rw_g_l2, rw_k_k, rw_k_a, rw_r_k, rw_lnx_g, rw_lnx_b, rw_w_out,
           rw_v0, rw_v_l1, rw_v_l2, kv_w,
           da_w_q, da_lam_q1, da_lam_k1, da_lam_q2, da_lam_k2, da_subln_g, da_w_out,
           moe_router_w, moe_router_b, moe_w_gate, moe_w_up, moe_w_down,
           moe_sh_gate, moe_sh_up, moe_sh_down):
    B, S, D = x.shape
    L = S + N_META
    Lp = -(-L // SEQ_ALIGN) * SEQ_ALIGN
    Np = B * Lp
    assert D == D_MODEL and Np % ROW_TM == 0 and Lp % PROJ_TM == 0 and Lp % SCAN_TC == 0 and Lp % ATT_T == 0
    meta = jnp.broadcast_to(meta_tokens[None].astype(x.dtype), (B, N_META, D))
    h = jnp.concatenate([meta, x, jnp.zeros((B, Lp - L, D), x.dtype)], axis=1)

    pc = _perm_cols()
    gs_np, gb_np = _head_sum_mats()
    gs, gb = jnp.asarray(gs_np), jnp.asarray(gb_np)
    tri = jnp.asarray(np.triu(np.ones((ROUTE_TM, ROUTE_TM), np.float32), 1)).astype(BF16)
    row = lambda vec: vec.reshape(1, -1)
    lane = np.arange(2 * D_MODEL)
    m_key, m_head = lane // 32, lane % 16
    c128 = np.arange(LANES) % 16
    vrow = np.arange(RWKV_HEAD)

    v_first = None
    kv = None
    for l in range(DEPTH):
        if l < N_A_LAYERS:
            p = {
                "mu": _pad_rows(rw_mu[l], SUBLANES),
                "wr": rw_w_rkv[l, 0][:, pc].astype(BF16),
                "wk": rw_w_rkv[l, 1][:, pc].astype(BF16),
                "wv": rw_w_rkv[l, 2][:, pc].astype(BF16),
                "w0": row(rw_w0[l][pc]),
                "wl1": _pad_cols(rw_w_l1[l], LANES).astype(BF16),
                "wl2": _pad_rows(rw_w_l2[l][:, pc], LANES).astype(BF16),
                "a0": row(rw_a0[l][pc]),
                "al1": _pad_cols(rw_a_l1[l], LANES).astype(BF16),
                "al2": _pad_rows(rw_a_l2[l][:, pc], LANES).astype(BF16),
                "gl1": _pad_cols(rw_g_l1[l], 2 * LANES).astype(BF16),
                "gl2": _pad_rows(rw_g_l2[l][:, pc], 2 * LANES).astype(BF16),
                "kk": row(rw_k_k[l][pc]),
                "ka": row(rw_k_a[l][pc]),
                "gs": gs,
                "gb": gb,
            }
            if l > 0:
                p["v0"] = row(rw_v0[l - 1][pc])
                p["vl1"] = _pad_cols(rw_v_l1[l - 1], LANES).astype(BF16)
                p["vl2"] = _pad_rows(rw_v_l2[l - 1][:, pc], LANES).astype(BF16)
            r, w, k, v, a, b, g = _rwkv_proj(h, v_first if l > 0 else None, p)
            if l == 0:
                v_first = v
            rk_m = rw_r_k[l][m_head, m_key].reshape(1, 2 * D_MODEL)
            lg_t = rw_lnx_g[l].reshape(RWKV_HEADS, RWKV_HEAD)[c128[None, :], vrow[:, None]]
            lb_t = rw_lnx_b[l].reshape(RWKV_HEADS, RWKV_HEAD)[c128[None, :], vrow[:, None]]
            z = _wkv_scan(r, w, k, v, a, b, rk_m, lg_t, lb_t)
            h2 = _mm_res_ln(z.reshape(Np, D), g.reshape(Np, D), rw_w_out[l][pc, :].astype(BF16),
                            h.reshape(Np, D), row(ln_mix_g[l]), row(ln_mix_b[l]))
        else:
            j = l - N_A_LAYERS
            h2 = h.reshape(Np, D)
            if kv is None:
                kv = _proj(h2, kv_w.astype(BF16)).reshape(B, Lp, 2 * D)
            q_scale = DIFF_HD ** -0.5 * math.log2(math.e)
            q = _proj(h2, (da_w_q[j] * q_scale).astype(BF16)).reshape(B, Lp, D)
            lam_init = 0.8 - 0.6 * math.exp(-0.3 * l)
            lam4 = jnp.stack([da_lam_q1[j], da_lam_k1[j], da_lam_q2[j], da_lam_k2[j]])
            o = _diff_attn(q, kv, _pad_rows(lam4, SUBLANES), row(da_subln_g[j]), lam_init)
            h2 = _mm_res_ln(o.reshape(Np, D), None, da_w_out[j].astype(BF16), h2,
                            row(ln_mix_g[l]), row(ln_mix_b[l]))
        h2 = _moe_layer(h2, l, row(ln_ffn_g[l]), row(ln_ffn_b[l]), moe_router_w, moe_router_b,
                        moe_w_gate, moe_w_up, moe_w_down, moe_sh_gate, moe_sh_up, moe_sh_down, tri)
        h = h2.reshape(B, Lp, D)
    return h[:, N_META:L]


_trunk_jit = jax.jit(_trunk)


def kernel(x, meta_tokens, ln_mix_g, ln_mix_b, ln_ffn_g, ln_ffn_b, rw_mu, rw_w_rkv, rw_w0, rw_w_l1, rw_w_l2, rw_a0, rw_a_l1, rw_a_l2, rw_g_l1, rw_g_l2, rw_k_k, rw_k_a, rw_r_k, rw_lnx_g, rw_lnx_b, rw_w_out, rw_v0, rw_v_l1, rw_v_l2, kv_w, da_w_q, da_lam_q1, da_lam_k1, da_lam_q2, da_lam_k2, da_subln_g, da_w_out, moe_router_w, moe_router_b, moe_w_gate, moe_w_up, moe_w_down, moe_sh_gate, moe_sh_up, moe_sh_down):
    return _trunk_jit(x, meta_tokens, ln_mix_g, ln_mix_b, ln_ffn_g, ln_ffn_b, rw_mu, rw_w_rkv, rw_w0, rw_w_l1,
                      rw_w_l2, rw_a0, rw_a_l1, rw_a_l2, rw_g_l1, rw_g_l2, rw_k_k, rw_k_a, rw_r_k, rw_lnx_g,
                      rw_lnx_b, rw_w_out, rw_v0, rw_v_l1, rw_v_l2, kv_w, da_w_q, da_lam_q1, da_lam_k1,
                      da_lam_q2, da_lam_k2, da_subln_g, da_w_out, moe_router_w, moe_router_b, moe_w_gate,
                      moe_w_up, moe_w_down, moe_sh_gate, moe_sh_up, moe_sh_down)
```

```python
import functools
import math

import numpy as np
import jax
import jax.numpy as jnp
from jax import lax
from jax.experimental import pallas as pl
from jax.experimental.pallas import tpu as pltpu

F32 = jnp.float32
BF16 = jnp.bfloat16
I32 = jnp.int32

D_MODEL = 1024
DEPTH = 4
N_META = 16
N_A_LAYERS = DEPTH // 2
RWKV_HEAD = 64
RWKV_HEADS = D_MODEL // RWKV_HEAD
RWKV_GN_EPS = 64e-5
DIFF_HEADS = 8
DIFF_HD = D_MODEL // (2 * DIFF_HEADS)
N_EXPERTS = 64
TOP_K = 8
N_GROUPS = 8
TOPK_GROUPS = 4
EXPERT_FF = D_MODEL // 4
ROUTED_SCALE = 2.5
DN_ALPHA = (2 * DEPTH) ** 0.25
LN_EPS = 1e-5

LANES = 128
SUBLANES = 8
VMEM_LIMIT_BYTES = 56 * 1024 * 1024
DMA_QUEUES = 2
ROW_TILES = D_MODEL // LANES

SEQ_ALIGN = 256
PROJ_TM = 256
ROW_TM = 512
SCAN_TC = 128
SCAN_SUB = SUBLANES
ATT_T = 256
ROUTE_TM = 256
MOE_BLK = 512
DISP_TM = 512
COMB_TM = 256

_SIGMA = (0, 4, 1, 5, 2, 6, 3, 7)


def _cparams(sem):
    return pltpu.CompilerParams(dimension_semantics=sem, vmem_limit_bytes=VMEM_LIMIT_BYTES)


def _dot(a, b):
    return jnp.dot(a, b, preferred_element_type=F32)


def _dot_hi(a, b):
    return jnp.dot(a, b, preferred_element_type=F32, precision=lax.Precision.HIGHEST)


def _full(shape):
    nd = len(shape)
    return pl.BlockSpec(shape, lambda *_: (0,) * nd)


def _layer_norm(y, g, b):
    mu = jnp.mean(y, axis=-1, keepdims=True)
    d = y - mu
    var = jnp.mean(d * d, axis=-1, keepdims=True)
    return d * lax.rsqrt(var + LN_EPS) * g + b


def _perm_cols():
    L = np.arange(D_MODEL)
    j, p, hh = L // LANES, (L % LANES) // 16, L % 16
    i = 8 * j + np.asarray(_SIGMA)[p]
    return (hh * RWKV_HEAD + i).astype(np.int32)


def _head_sum_mats():
    L = np.arange(D_MODEL)
    c = np.arange(LANES)
    gs = (L[:, None] % 16 == c[None, :] % 16).astype(np.float32)
    gb = ((c[:, None] < 16) & (c[:, None] == L[None, :] % 16)).astype(np.float32)
    return gs, gb


def _rwkv_proj_kernel(has_vres, *refs):
    if has_vres:
        (x_ref, xp_ref, mu_ref, wr_ref, wk_ref, wv_ref, w0_ref, wl1_ref, wl2_ref, a0_ref, al1_ref,
         al2_ref, gl1_ref, gl2_ref, kk_ref, ka_ref, gs_ref, gb_ref, vf_ref, v0_ref, vl1_ref, vl2_ref,
         r_o, w_o, k_o, v_o, a_o, b_o, g_o) = refs
    else:
        (x_ref, xp_ref, mu_ref, wr_ref, wk_ref, wv_ref, w0_ref, wl1_ref, wl2_ref, a0_ref, al1_ref,
         al2_ref, gl1_ref, gl2_ref, kk_ref, ka_ref, gs_ref, gb_ref,
         r_o, w_o, k_o, v_o, a_o, b_o, g_o) = refs
    i = pl.program_id(1)
    x = x_ref[0]
    prev = xp_ref[0][SUBLANES - 1:SUBLANES, :]
    prev = jnp.where(i == 0, 0.0, prev)
    row = lax.broadcasted_iota(I32, x.shape, 0)
    xprev = jnp.where(row == 0, prev, pltpu.roll(x, 1, 0))
    dx = xprev - x

    def mix(s):
        return (x + dx * mu_ref[s:s + 1, :]).astype(BF16)

    xv = mix(2)
    r = _dot(mix(0), wr_ref[...])
    k = _dot(mix(1), wk_ref[...])
    v = _dot(xv, wv_ref[...])
    zw = w0_ref[...] + _dot(jnp.tanh(_dot(mix(3), wl1_ref[...])).astype(BF16), wl2_ref[...])
    decay = jnp.exp(-math.exp(-0.5) * jax.nn.sigmoid(zw))
    a = jax.nn.sigmoid(a0_ref[...] + _dot(_dot(mix(4), al1_ref[...]).astype(BF16), al2_ref[...]))
    g = _dot(jax.nn.sigmoid(_dot(mix(5), gl1_ref[...])).astype(BF16), gl2_ref[...])
    if has_vres:
        gate_v = jax.nn.sigmoid(v0_ref[...] + _dot(_dot(xv, vl1_ref[...]).astype(BF16), vl2_ref[...]))
        v = v + (vf_ref[0] - v) * gate_v
    kk = k * kk_ref[...]
    ss = _dot_hi(kk * kk, gs_ref[...])
    inv = 1.0 / jnp.maximum(jnp.sqrt(ss), 1e-12)
    kk = kk * _dot_hi(inv, gb_ref[...])
    k = k * (1.0 + (a - 1.0) * ka_ref[...])
    r_o[0] = r
    w_o[0] = decay
    k_o[0] = k
    v_o[0] = v
    a_o[0] = -kk
    b_o[0] = kk * a
    g_o[0] = g


def _rwkv_proj(h, vfirst, p):
    B, Lp, D = h.shape
    TM = PROJ_TM
    has_vres = vfirst is not None
    tile = pl.BlockSpec((1, TM, D), lambda b, i: (b, i, 0))
    prev8 = pl.BlockSpec((1, SUBLANES, D), lambda b, i: (b, jnp.maximum(i * (TM // SUBLANES) - 1, 0), 0))
    names = ["mu", "wr", "wk", "wv", "w0", "wl1", "wl2", "a0", "al1", "al2", "gl1", "gl2", "kk", "ka", "gs", "gb"]
    args = [h, h] + [p[n] for n in names]
    specs = [tile, prev8] + [_full(p[n].shape) for n in names]
    if has_vres:
        args += [vfirst, p["v0"], p["vl1"], p["vl2"]]
        specs += [tile, _full(p["v0"].shape), _full(p["vl1"].shape), _full(p["vl2"].shape)]
    out = jax.ShapeDtypeStruct((B, Lp, D), F32)
    return pl.pallas_call(
        functools.partial(_rwkv_proj_kernel, has_vres),
        out_shape=[out] * 7,
        grid=(B, Lp // TM),
        in_specs=specs,
        out_specs=[tile] * 7,
        compiler_params=_cparams(("parallel", "arbitrary")),
        name="rwkv_proj",
    )(*args)


def _wkv_kernel(r_ref, w_ref, k_ref, v_ref, a_ref, b_ref, rk_ref, lg_ref, lb_ref, o_ref,
                s_ref, ma_ref, mwr_ref, mw_ref, mb_ref, mk_ref, zv_ref, br_ref, kr_ref, rkk_ref, yc_ref):
    NV = RWKV_HEAD
    NCOL = 2 * D_MODEL // LANES
    NRB = NV // SUBLANES

    @pl.when(pl.program_id(0) == 0)
    def _():
        s_ref[...] = jnp.zeros_like(s_ref)

    lane = lax.broadcasted_iota(I32, (SUBLANES, LANES), 1)
    even = ((lane // 16) % 2) == 0
    grp = lane // 32

    def cs(j):
        return slice(j * LANES, (j + 1) * LANES)

    def merge(ref, t0):
        x0 = ref[0, pl.ds(t0, SUBLANES), :]
        x1 = ref[1, pl.ds(t0, SUBLANES), :]
        cols = []
        for j in range(D_MODEL // LANES):
            a0 = x0[:, cs(j)]
            a1 = x1[:, cs(j)]
            cols.append(jnp.where(even, a0, pltpu.roll(a1, 16, 1)))
            cols.append(jnp.where(even, pltpu.roll(a0, LANES - 16, 1), a1))
        return cols

    def fold(x):
        return (x + pltpu.roll(x, 32, 1)) + (pltpu.roll(x, 64, 1) + pltpu.roll(x, 96, 1))

    def colsum(xs):
        acc = xs[0]
        for x in xs[1:]:
            acc = acc + x
        return acc

    def sub_chunk(c, carry):
        t0 = pl.multiple_of(c * SCAN_SUB, SCAN_SUB)
        R = merge(r_ref, t0)
        W = merge(w_ref, t0)
        K = merge(k_ref, t0)
        V = merge(v_ref, t0)
        A = merge(a_ref, t0)
        Bm = merge(b_ref, t0)
        trow_id = lax.broadcasted_iota(I32, (SUBLANES, LANES), 0)
        for j in range(NCOL):
            p = W[j]
            for sh in (1, 2, 4):
                p = p * jnp.where(trow_id >= sh, pltpu.roll(p, sh, 0), 1.0)
            p_prev = jnp.where(trow_id >= 1, pltpu.roll(p, 1, 0), 1.0)
            p_inv = 1.0 / p
            ma_ref[j] = A[j] * p_prev
            mw_ref[j] = p
            mwr_ref[j] = R[j] * p
            mb_ref[j] = Bm[j] * p_inv
            mk_ref[j] = K[j] * p_inv
        br_ref[...] = fold(colsum([Bm[j] * R[j] for j in range(NCOL)]))
        kr_ref[...] = fold(colsum([K[j] * R[j] for j in range(NCOL)]))
        rkk_ref[...] = fold(colsum([K[j] * R[j] * rk_ref[:, cs(j)] for j in range(NCOL)]))
        for j in range(NCOL):
            rolled = [V[j]] + [pltpu.roll(V[j], 32 * m, 1) for m in range(1, 4)]
            for q in range(4):
                z = rolled[(3 - q) % 4]
                for pos in range(2, -1, -1):
                    z = jnp.where(grp == pos, rolled[(pos - q) % 4], z)
                zv_ref[pl.ds(4 * j + q, SUBLANES, stride=NV), :] = z

        def pair_sum(xs):
            return colsum(xs[0::2]) + colsum(xs[1::2])

        def step(t):
            trow = slice(t, t + 1)
            base = t * NV
            brt = br_ref[trow, :]
            krt = kr_ref[trow, :]
            rkt = rkk_ref[trow, :]
            vcol, y = [], []
            for i in range(NRB):
                rows = slice(i * SUBLANES, (i + 1) * SUBLANES)
                vc = zv_ref[pl.ds(base + i * SUBLANES, SUBLANES), :]
                pa, py = [], []
                for j in range(NCOL):
                    sj = s_ref[rows, cs(j)]
                    pa.append(sj * ma_ref[j, trow, :])
                    py.append(sj * mwr_ref[j, trow, :])
                    s_ref[rows, cs(j)] = sj + vc * mk_ref[j, trow, :]
                sa = fold(pair_sum(pa))
                yp = fold(pair_sum(py))
                for j in range(NCOL):
                    s_ref[rows, cs(j)] = s_ref[rows, cs(j)] + sa * mb_ref[j, trow, :]
                vcol.append(vc)
                y.append(yp + sa * brt + vc * krt)
            mean = jnp.sum(colsum(y), axis=0, keepdims=True) * (1.0 / NV)
            d = [yi - mean for yi in y]
            var = jnp.sum(colsum([di * di for di in d]), axis=0, keepdims=True) * (1.0 / NV)
            inv = lax.rsqrt(var + RWKV_GN_EPS)
            for i in range(NRB):
                rows = slice(i * SUBLANES, (i + 1) * SUBLANES)
                out = d[i] * inv * lg_ref[rows, :] + lb_ref[rows, :] + vcol[i] * rkt
                yc_ref[pl.ds(base + i * SUBLANES, SUBLANES), :] = out

        for t in range(SCAN_SUB):
            step(t)
        last = slice(SCAN_SUB - 1, SCAN_SUB)
        for j in range(NCOL):
            s_ref[:, cs(j)] = s_ref[:, cs(j)] * mw_ref[j, last, :]

        cols = []
        for j in range(NCOL):
            acc = None
            for q in range(4):
                z = yc_ref[pl.ds(4 * j + q, SUBLANES, stride=NV), :]
                acc = z if q == 0 else jnp.where(grp == q, z, acc)
            cols.append(acc)
        for j in range(D_MODEL // LANES):
            c0, c1 = cols[2 * j], cols[2 * j + 1]
            o_ref[0, pl.ds(t0, SUBLANES), cs(j)] = jnp.where(even, c0, pltpu.roll(c1, 16, 1))
            o_ref[1, pl.ds(t0, SUBLANES), cs(j)] = jnp.where(even, pltpu.roll(c0, LANES - 16, 1), c1)
        return carry

    lax.fori_loop(0, SCAN_TC // SCAN_SUB, sub_chunk, 0)


def _wkv_scan(r, w, k, v, a, b, rk_m, lg_t, lb_t):
    B, Lp, D = r.shape
    assert B == 2, "the scan packs exactly two batch rows into the lane dimension"
    blk = pl.BlockSpec((B, SCAN_TC, D), lambda i: (0, i, 0))
    vm = lambda *s: pltpu.VMEM(s, F32)
    return pl.pallas_call(
        _wkv_kernel,
        out_shape=jax.ShapeDtypeStruct((B, Lp, D), F32),
        grid=(Lp // SCAN_TC,),
        in_specs=[blk] * 6 + [_full(rk_m.shape), _full(lg_t.shape), _full(lb_t.shape)],
        out_specs=blk,
        scratch_shapes=[vm(RWKV_HEAD, 2 * D)] + [vm(2 * D // LANES, SCAN_SUB, LANES)] * 5
        + [vm(SCAN_SUB * RWKV_HEAD, LANES)] + [vm(SCAN_SUB, LANES)] * 3
        + [vm(SCAN_SUB * RWKV_HEAD, LANES)],
        compiler_params=_cparams(("arbitrary",)),
        name="wkv_scan",
    )(r, w, k, v, a, b, rk_m, lg_t, lb_t)


def _mm_res_ln_kernel(has_gate, *refs):
    if has_gate:
        z_ref, g_ref, w_ref, h_ref, lg_ref, lb_ref, o_ref = refs
        z = (z_ref[...] * g_ref[...]).astype(BF16)
    else:
        z_ref, w_ref, h_ref, lg_ref, lb_ref, o_ref = refs
        z = z_ref[...].astype(BF16)
    y = DN_ALPHA * h_ref[...] + _dot(z, w_ref[...])
    o_ref[...] = _layer_norm(y, lg_ref[...], lb_ref[...])


def _mm_res_ln(z, gate, w, h, lg, lb):
    Np, D = h.shape
    TM = ROW_TM
    tile = pl.BlockSpec((TM, D), lambda i: (i, 0))
    has_gate = gate is not None
    args = [z] + ([gate] if has_gate else []) + [w, h, lg, lb]
    specs = [tile] + ([tile] if has_gate else []) + [_full(w.shape), tile, _full(lg.shape), _full(lb.shape)]
    return pl.pallas_call(
        functools.partial(_mm_res_ln_kernel, has_gate),
        out_shape=jax.ShapeDtypeStruct((Np, D), F32),
        grid=(Np // TM,),
        in_specs=specs,
        out_specs=tile,
        compiler_params=_cparams(("parallel",)),
        name="mm_res_ln",
    )(*args)


def _proj_kernel(x_ref, w_ref, o_ref):
    o_ref[...] = _dot(x_ref[...].astype(BF16), w_ref[...]).astype(o_ref.dtype)


def _proj(x, w):
    Np, D = x.shape
    Nout = w.shape[1]
    TM = ROW_TM
    return pl.pallas_call(
        _proj_kernel,
        out_shape=jax.ShapeDtypeStruct((Np, Nout), BF16),
        grid=(Np // TM,),
        in_specs=[pl.BlockSpec((TM, D), lambda i: (i, 0)), _full(w.shape)],
        out_specs=pl.BlockSpec((TM, Nout), lambda i: (i, 0)),
        compiler_params=_cparams(("parallel",)),
        name="proj",
    )(x, w)


def _attn_kernel(lam_init, lam_ref, sg_ref, q_ref, k_ref, v_ref, o_ref,
                 sa_ref, sb_ref, m1_ref, l1_ref, acc1_ref, m2_ref, l2_ref, acc2_ref):
    T = ATT_T
    qi = pl.program_id(2)
    last = pl.num_programs(2) - 1
    q = q_ref[0]
    lane = lax.broadcasted_iota(I32, q.shape, 1)
    zero = jnp.zeros_like(q)
    q12 = jnp.concatenate([jnp.where(lane < DIFF_HD, q, zero), jnp.where(lane >= DIFF_HD, q, zero)], axis=0)
    m1_ref[...] = jnp.full_like(m1_ref, -1e30)
    m2_ref[...] = jnp.full_like(m2_ref, -1e30)
    l1_ref[...] = jnp.zeros_like(l1_ref)
    l2_ref[...] = jnp.zeros_like(l2_ref)
    acc1_ref[...] = jnp.zeros_like(acc1_ref)
    acc2_ref[...] = jnp.zeros_like(acc2_ref)
    nt = (((1,), (1,)), ((), ()))
    ones = jnp.ones((T, LANES), BF16)
    rel = lax.broadcasted_iota(I32, (T, T), 1) - lax.broadcasted_iota(I32, (T, T), 0)

    def chunk_start(kc):
        return pl.multiple_of(jnp.minimum(kc, last) * T, T)

    def halves(x):
        return [x[:, c * LANES:(c + 1) * LANES] for c in range(x.shape[1] // LANES)]

    def scores(g, buf):
        for half in range(2):
            kb = k_ref[0, pl.ds(chunk_start(2 * g + half), T), :]
            s12 = lax.dot_general(q12, kb, nt, preferred_element_type=F32)
            buf[0, :, half * T:(half + 1) * T] = s12[:T]
            buf[1, :, half * T:(half + 1) * T] = s12[T:]

    def consume(g, buf, masked):
        vext = jnp.concatenate(
            [jnp.concatenate([v_ref[0, pl.ds(chunk_start(2 * g + half), T), :], ones], axis=1) for half in range(2)],
            axis=0)
        for si, m_ref, l_ref, acc_ref in ((0, m1_ref, l1_ref, acc1_ref), (1, m2_ref, l2_ref, acc2_ref)):
            s = buf[si]
            if masked:
                s = jnp.concatenate(
                    [jnp.where(rel <= (qi - (2 * g + half)) * T, s[:, half * T:(half + 1) * T], -jnp.inf)
                     for half in range(2)], axis=1)
            parts = halves(s)
            smax = jnp.max(jnp.maximum(jnp.maximum(parts[0], parts[1]), jnp.maximum(parts[2], parts[3])),
                           axis=1, keepdims=True)
            m_old = m_ref[...]
            m_new = jnp.maximum(m_old, smax)
            alpha = jnp.exp2(m_old - m_new)
            p = jnp.exp2(s - jnp.concatenate([m_new] * len(parts), axis=1)).astype(BF16)
            d = _dot(p, vext)
            acc_ref[...] = alpha * acc_ref[...] + d[:, :LANES]
            l_ref[...] = alpha * l_ref[...] + d[:, LANES:]
            m_ref[...] = m_new

    n_free = qi // 2
    scores(0, sa_ref)

    def pair(i, carry):
        g = 2 * i
        scores(g + 1, sb_ref)
        consume(g, sa_ref, False)
        scores(g + 2, sa_ref)
        consume(g + 1, sb_ref, False)
        return carry

    n_pairs = n_free // 2
    lax.fori_loop(0, n_pairs, pair, 0)
    g_a = 2 * n_pairs
    odd = n_free % 2 == 1

    @pl.when(odd)
    def _():
        scores(g_a + 1, sb_ref)

    consume(g_a, sa_ref, True)

    @pl.when(odd)
    def _():
        consume(g_a + 1, sb_ref, True)


    lam_v = lam_ref[...]
    lam = (jnp.exp(jnp.sum(lam_v[0:1] * lam_v[1:2], axis=1, keepdims=True))
           - jnp.exp(jnp.sum(lam_v[2:3] * lam_v[3:4], axis=1, keepdims=True)) + lam_init)
    o = acc1_ref[...] / l1_ref[...] - lam * (acc2_ref[...] / l2_ref[...])
    o = o * lax.rsqrt(jnp.mean(o * o, axis=1, keepdims=True) + 1e-5) * sg_ref[...] * (1.0 - lam_init)
    o_ref[0] = o.astype(o_ref.dtype)


def _diff_attn(q, kv, lam4, subln_g, lam_init):
    B, Lp, D = q.shape
    T = ATT_T
    H = DIFF_HEADS
    HW = 2 * DIFF_HD
    vm = lambda *s: pltpu.VMEM(s, F32)
    return pl.pallas_call(
        functools.partial(_attn_kernel, lam_init),
        out_shape=jax.ShapeDtypeStruct((B, Lp, D), BF16),
        grid=(B, H, Lp // T),
        in_specs=[_full(lam4.shape), _full(subln_g.shape),
                  pl.BlockSpec((1, T, HW), lambda b, h, i: (b, i, h)),
                  pl.BlockSpec((1, Lp, HW), lambda b, h, i: (b, 0, h)),
                  pl.BlockSpec((1, Lp, HW), lambda b, h, i: (b, 0, H + h))],
        out_specs=pl.BlockSpec((1, T, HW), lambda b, h, i: (b, i, h)),
        scratch_shapes=[vm(2, T, 2 * T), vm(2, T, 2 * T),
                        vm(T, LANES), vm(T, LANES), vm(T, HW), vm(T, LANES), vm(T, LANES), vm(T, HW)],
        compiler_params=_cparams(("parallel", "parallel", "arbitrary")),
        name="diff_attn",
    )(lam4, subln_g, q, kv, kv)


def _route_kernel(x_ref, rw_ref, rb_ref, sg_ref, su_ref, sd_ref, tri_ref,
                  sh_o, idx_o, gate_o, rank_o, cnt_o, carry_ref):
    TM = ROUTE_TM
    G, EG = N_GROUPS, N_EXPERTS // N_GROUPS

    @pl.when(pl.program_id(0) == 0)
    def _():
        carry_ref[...] = jnp.zeros_like(carry_ref)

    x = x_ref[...]
    xb = x.astype(BF16)
    hmid = _dot(xb, sg_ref[...])
    hmid = hmid * jax.nn.sigmoid(hmid) * _dot(xb, su_ref[...])
    sh_o[...] = _dot(hmid.astype(BF16), sd_ref[...])

    logit = lax.dot_general(rw_ref[...], x, (((1,), (1,)), ((), ())),
                            preferred_element_type=F32, precision=lax.Precision.HIGHEST)
    s = jax.nn.sigmoid(logit)
    s3 = s.reshape(G, EG, TM)
    sel3 = (s + rb_ref[...]).reshape(G, EG, TM)
    io_j = lax.broadcasted_iota(I32, (G, EG, TM), 1)
    io_g = lax.broadcasted_iota(I32, (G, 1, TM), 0)
    neg = -jnp.inf
    m1 = jnp.max(sel3, axis=1, keepdims=True)
    i1 = jnp.min(jnp.where(sel3 == m1, io_j, EG), axis=1, keepdims=True)
    m2 = jnp.max(jnp.where(io_j == i1, neg, sel3), axis=1, keepdims=True)
    gsc = m1 + m2
    gkeep = jnp.zeros((G, 1, TM), F32)
    for _ in range(TOPK_GROUPS):
        m = jnp.max(gsc, axis=0, keepdims=True)
        gi = jnp.min(jnp.where(gsc == m, io_g, G), axis=0, keepdims=True)
        hit = io_g == gi
        gkeep = jnp.where(hit, 1.0, gkeep)
        gsc = jnp.where(hit, neg, gsc)
    cur = jnp.where(gkeep > 0.0, sel3, neg)
    io_e = io_g * EG + io_j
    hits, idxs, ws = [], [], []
    for _ in range(TOP_K):
        m = jnp.max(jnp.max(cur, axis=1, keepdims=True), axis=0, keepdims=True)
        ei = jnp.min(jnp.min(jnp.where(cur == m, io_e, N_EXPERTS), axis=1, keepdims=True), axis=0, keepdims=True)
        hit = io_e == ei
        ws.append(jnp.sum(jnp.sum(jnp.where(hit, s3, 0.0), axis=1, keepdims=True), axis=0, keepdims=True))
        cur = jnp.where(hit, neg, cur)
        hits.append(hit)
        idxs.append(ei)
    wsum = ws[0]
    for wv in ws[1:]:
        wsum = wsum + wv
    scale = ROUTED_SCALE / wsum
    onehot = jnp.zeros((G, EG, TM), F32)
    for hit in hits:
        onehot = jnp.where(hit, 1.0, onehot)
    oh2 = onehot.reshape(N_EXPERTS, TM)
    rank_full = (_dot(oh2.astype(BF16), tri_ref[...]) + carry_ref[:, :1]).reshape(G, EG, TM)
    for kk in range(TOP_K):
        rk = jnp.sum(jnp.sum(jnp.where(hits[kk], rank_full, 0.0), axis=1, keepdims=True), axis=0, keepdims=True)
        idx_o[kk:kk + 1, :] = idxs[kk].reshape(1, TM)
        gate_o[kk:kk + 1, :] = (ws[kk] * scale).reshape(1, TM)
        rank_o[kk:kk + 1, :] = rk.reshape(1, TM).astype(I32)
    carry_ref[...] = carry_ref[...] + jnp.sum(oh2, axis=1, keepdims=True)
    cnt_o[...] = carry_ref[...]


def _route_shared(x, rw_t, rb, sg, su, sd, tri):
    Np, D = x.shape
    TM = ROUTE_TM
    tok = pl.BlockSpec((TOP_K, TM), lambda i: (0, i))
    return pl.pallas_call(
        _route_kernel,
        out_shape=[jax.ShapeDtypeStruct((Np, D), F32),
                   jax.ShapeDtypeStruct((TOP_K, Np), I32),
                   jax.ShapeDtypeStruct((TOP_K, Np), F32),
                   jax.ShapeDtypeStruct((TOP_K, Np), I32),
                   jax.ShapeDtypeStruct((N_EXPERTS, LANES), F32)],
        grid=(Np // TM,),
        in_specs=[pl.BlockSpec((TM, D), lambda i: (i, 0)), _full(rw_t.shape), _full(rb.shape),
                  _full(sg.shape), _full(su.shape), _full(sd.shape), _full(tri.shape)],
        out_specs=[pl.BlockSpec((TM, D), lambda i: (i, 0)), tok, tok, tok, _full((N_EXPERTS, LANES))],
        scratch_shapes=[pltpu.VMEM((N_EXPERTS, LANES), F32)],
        compiler_params=_cparams(("arbitrary",)),
        name="moe_route",
    )(x, rw_t, rb, sg, su, sd, tri)


def _tile_copy(src_ref, src_row, dst_ref, dst_row, sem):
    src = src_ref.at[pl.ds(pl.multiple_of(src_row * ROW_TILES, ROW_TILES), ROW_TILES)]
    dst = dst_ref.at[pl.ds(pl.multiple_of(dst_row * ROW_TILES, ROW_TILES), ROW_TILES)]
    return pltpu.make_async_copy(src, dst, sem)


def _rows_to_tiles(x, tiles_ref, n_rows):
    for s in range(ROW_TILES):
        tiles_ref[pl.ds(s, n_rows, stride=ROW_TILES), :] = x[:, s * LANES:(s + 1) * LANES]


def _tiles_to_rows(tiles_ref, n_rows):
    return [tiles_ref[pl.ds(s, n_rows, stride=ROW_TILES), :] for s in range(ROW_TILES)]


def _dispatch_kernel(ps_ref, idx_ref, rank_ref, x_ref, xs_hbm, stage_ref, sem):
    _rows_to_tiles(x_ref[...], stage_ref, DISP_TM)

    def issue(n, c):
        for kk in range(TOP_K):
            dst = ps_ref[idx_ref[kk, n]] + rank_ref[kk, n]
            _tile_copy(stage_ref, n, xs_hbm, dst, sem).start(priority=kk % DMA_QUEUES)
        return c

    lax.fori_loop(0, DISP_TM, issue, 0)

    def drain(n, c):
        for kk in range(TOP_K):
            _tile_copy(stage_ref, n, xs_hbm, 0, sem).wait()
        return c

    lax.fori_loop(0, DISP_TM, drain, 0)


def _dispatch(x, idx_t, rank_t, pad_start, n_rows):
    Np, D = x.shape
    tok = pl.BlockSpec((TOP_K, DISP_TM), lambda i, ps: (0, i), memory_space=pltpu.SMEM)
    return pl.pallas_call(
        _dispatch_kernel,
        out_shape=jax.ShapeDtypeStruct((n_rows * ROW_TILES, LANES), F32),
        grid_spec=pltpu.PrefetchScalarGridSpec(
            num_scalar_prefetch=1,
            grid=(Np // DISP_TM,),
            in_specs=[tok, tok, pl.BlockSpec((DISP_TM, D), lambda i, ps: (i, 0))],
            out_specs=pl.BlockSpec(memory_space=pl.ANY),
            scratch_shapes=[pltpu.VMEM((DISP_TM * ROW_TILES, LANES), F32), pltpu.SemaphoreType.DMA(())],
        ),
        compiler_params=_cparams(("arbitrary",)),
        name="moe_dispatch",
    )(pad_start, idx_t, rank_t, x)


def _expert_kernel(be_ref, nu_ref, xs_ref, wg_ref, wu_ref, wd_ref, ys_ref):
    @pl.when(pl.program_id(0) < nu_ref[0])
    def _():
        x = jnp.concatenate(_tiles_to_rows(xs_ref, MOE_BLK), axis=1).astype(BF16)
        g = _dot(x, wg_ref[0])
        hmid = g * jax.nn.sigmoid(g) * _dot(x, wu_ref[0])
        _rows_to_tiles(_dot(hmid.astype(BF16), wd_ref[0]), ys_ref, MOE_BLK)


def _experts(xs, block_e, n_used, wg, wu, wd):
    D, F = wg.shape[1], wg.shape[2]
    nb = xs.shape[0] // (MOE_BLK * ROW_TILES)
    tiles = pl.BlockSpec((MOE_BLK * ROW_TILES, LANES), lambda i, be, nu: (jnp.minimum(i, nu[0] - 1), 0))
    wspec = lambda s: pl.BlockSpec((1,) + s, lambda i, be, nu: (be[jnp.minimum(i, nu[0] - 1)], 0, 0))
    return pl.pallas_call(
        _expert_kernel,
        out_shape=jax.ShapeDtypeStruct(xs.shape, F32),
        grid_spec=pltpu.PrefetchScalarGridSpec(
            num_scalar_prefetch=2,
            grid=(nb,),
            in_specs=[tiles, wspec((D, F)), wspec((D, F)), wspec((F, D))],
            out_specs=tiles,
        ),
        compiler_params=_cparams(("arbitrary",)),
        name="moe_experts",
    )(block_e, n_used, xs, wg, wu, wd)


def _combine_kernel(ps_ref, idx_ref, rank_ref, gate_ref, sh_ref, h_ref, lg_ref, lb_ref, ys_hbm, o_ref,
                    buf, sem):
    def issue(n, c):
        for kk in range(TOP_K):
            src = ps_ref[idx_ref[kk, n]] + rank_ref[kk, n]
            _tile_copy(ys_hbm, src, buf.at[kk], n, sem).start(priority=kk % DMA_QUEUES)
        return c

    lax.fori_loop(0, COMB_TM, issue, 0)

    def drain(n, c):
        for kk in range(TOP_K):
            _tile_copy(ys_hbm, 0, buf.at[kk], n, sem).wait()
        return c

    lax.fori_loop(0, COMB_TM, drain, 0)
    groups = []
    for s in range(ROW_TILES):
        acc = None
        for kk in range(TOP_K):
            part = gate_ref[kk] * buf[kk, pl.ds(s, COMB_TM, stride=ROW_TILES), :]
            acc = part if kk == 0 else acc + part
        groups.append(acc)
    ffn = sh_ref[...] + jnp.concatenate(groups, axis=1)
    o_ref[...] = _layer_norm(DN_ALPHA * h_ref[...] + ffn, lg_ref[...], lb_ref[...])


def _combine(ys, idx_t, rank_t, pad_start, gate, shared, h, lg, lb):
    Np, D = h.shape
    TM = COMB_TM
    tok = pl.BlockSpec((TOP_K, TM), lambda i, ps: (0, i), memory_space=pltpu.SMEM)
    tile = pl.BlockSpec((TM, D), lambda i, ps: (i, 0))
    vec = pl.BlockSpec((1, D), lambda i, ps: (0, 0))
    return pl.pallas_call(
        _combine_kernel,
        out_shape=jax.ShapeDtypeStruct((Np, D), F32),
        grid_spec=pltpu.PrefetchScalarGridSpec(
            num_scalar_prefetch=1,
            grid=(Np // TM,),
            in_specs=[tok, tok, pl.BlockSpec((TOP_K, TM, LANES), lambda i, ps: (0, i, 0)), tile, tile, vec, vec,
                      pl.BlockSpec(memory_space=pl.ANY)],
            out_specs=tile,
            scratch_shapes=[pltpu.VMEM((TOP_K, TM * ROW_TILES, LANES), F32), pltpu.SemaphoreType.DMA(())],
        ),
        compiler_params=_cparams(("arbitrary",)),
        name="moe_combine",
    )(pad_start, idx_t, rank_t, gate, shared, h, lg, lb, ys)


def _moe_layer(h2, l, lg, lb, router_w, router_b, w_gate, w_up, w_down, sh_gate, sh_up, sh_down, tri):
    Np, D = h2.shape
    shared, idx_t, gate_t, rank_t, cnt = _route_shared(
        h2, router_w[l].T, router_b[l].reshape(N_EXPERTS, 1),
        sh_gate[l].astype(BF16), sh_up[l].astype(BF16), sh_down[l].astype(BF16), tri)
    counts = cnt[:, 0].astype(I32)
    padded = (counts + MOE_BLK - 1) // MOE_BLK * MOE_BLK
    pad_end = jnp.cumsum(padded)
    pad_start = (pad_end - padded).astype(I32)
    n_blocks = Np * TOP_K // MOE_BLK + N_EXPERTS
    n_used = (pad_end[-1:] // MOE_BLK).astype(I32)
    blk_row0 = jnp.arange(n_blocks, dtype=I32) * MOE_BLK
    block_e = jnp.minimum(jnp.sum((pad_end[None, :] <= blk_row0[:, None]).astype(I32), axis=1), N_EXPERTS - 1)
    xs = _dispatch(h2, idx_t, rank_t, pad_start, n_blocks * MOE_BLK)
    ys = _experts(xs, block_e, n_used, w_gate[l].astype(BF16), w_up[l].astype(BF16), w_down[l].astype(BF16))
    gate_splat = jnp.broadcast_to(gate_t[:, :, None], (TOP_K, Np, LANES))
    return _combine(ys, idx_t, rank_t, pad_start, gate_splat, shared, h2, lg, lb)


def _pad_cols(w, n):
    return jnp.pad(w, ((0, 0), (0, n - w.shape[1])))


def _pad_rows(w, n):
    return jnp.pad(w, ((0, n - w.shape[0]), (0, 0)))


def _trunk(x, meta_tokens, ln_mix_g, ln_mix_b, ln_ffn_g, ln_ffn_b,
           rw_mu, rw_w_rkv, rw_w0, rw_w_l1, rw_w_l2, rw_a0, rw_a_l1, rw_a_l2,
           rw_g_l1, rw_g_l2, rw_k_k, rw_k_a, rw_r_k, rw_lnx_g, rw_lnx_b, rw_w_out,
           rw_v0, rw_v_l1, rw_v_l2, kv_w,
           da_w_q, da_lam_q1, da_lam_k1, da_lam_q2, da_lam_k2, da_subln_g, da_w_out,
           moe_router_w, moe_router_b, moe_w_gate, moe_w_up, moe_w_down,
           moe_sh_gate, moe_sh_up, moe_sh_down):
    B, S, D = x.shape
    L = S + N_META
    Lp = -(-L // SEQ_ALIGN) * SEQ_ALIGN
    Np = B * Lp
    assert D == D_MODEL and Np % ROW_TM == 0 and Lp % PROJ_TM == 0 and Lp % SCAN_TC == 0 and Lp % ATT_T == 0
    meta = jnp.broadcast_to(meta_tokens[None].astype(x.dtype), (B, N_META, D))
    h = jnp.concatenate([meta, x, jnp.zeros((B, Lp - L, D), x.dtype)], axis=1)

    pc = _perm_cols()
    gs_np, gb_np = _head_sum_mats()
    gs, gb = jnp.asarray(gs_np), jnp.asarray(gb_np)
    tri = jnp.asarray(np.triu(np.ones((ROUTE_TM, ROUTE_TM), np.float32), 1)).astype(BF16)
    row = lambda vec: vec.reshape(1, -1)
    lane = np.arange(2 * D_MODEL)
    m_key, m_head = lane // 32, lane % 16
    c128 = np.arange(LANES) % 16
    vrow = np.arange(RWKV_HEAD)

    v_first = None
    kv = None
    for l in range(DEPTH):
        if l < N_A_LAYERS:
            p = {
                "mu": _pad_rows(rw_mu[l], SUBLANES),
                "wr": rw_w_rkv[l, 0][:, pc].astype(BF16),
                "wk": rw_w_rkv[l, 1][:, pc].astype(BF16),
                "wv": rw_w_rkv[l, 2][:, pc].astype(BF16),
                "w0": row(rw_w0[l][pc]),
                "wl1": _pad_cols(rw_w_l1[l], LANES).astype(BF16),
                "wl2": _pad_rows(rw_w_l2[l][:, pc], LANES).astype(BF16),
                "a0": row(rw_a0[l][pc]),
                "al1": _pad_cols(rw_a_l1[l], LANES).astype(BF16),
                "al2": _pad_rows(rw_a_l2[l][:, pc], LANES).astype(BF16),
                "gl1": _pad_cols(rw_g_l1[l], 2 * LANES).astype(BF16),
                "gl2": _pad_rows(rw_g_l2[l][:, pc], 2 * LANES).astype(BF16),
                "kk": row(rw_k_k[l][pc]),
                "ka": row(rw_k_a[l][pc]),
                "gs": gs,
                "gb": gb,
            }
            if l > 0:
                p["v0"] = row(rw_v0[l - 1][pc])
                p["vl1"] = _pad_cols(rw_v_l1[l - 1], LANES).astype(BF16)
                p["vl2"] = _pad_rows(rw_v_l2[l - 1][:, pc], LANES).astype(BF16)
            r, w, k, v, a, b, g = _rwkv_proj(h, v_first if l > 0 else None, p)
            if l == 0:
                v_first = v
            rk_m = rw_r_k[l][m_head, m_key].reshape(1, 2 * D_MODEL)
            lg_t = rw_lnx_g[l].reshape(RWKV_HEADS, RWKV_HEAD)[c128[None, :], vrow[:, None]]
            lb_t = rw_lnx_b[l].reshape(RWKV_HEADS, RWKV_HEAD)[c128[None, :], vrow[:, None]]
            z = _wkv_scan(r, w, k, v, a, b, rk_m, lg_t, lb_t)
            h2 = _mm_res_ln(z.reshape(Np, D), g.reshape(Np, D), rw_w_out[l][pc, :].astype(BF16),
                            h.reshape(Np, D), row(ln_mix_g[l]), row(ln_mix_b[l]))
        else:
            j = l - N_A_LAYERS
            h2 = h.reshape(Np, D)
            if kv is None:
                kv = _proj(h2, kv_w.astype(BF16)).reshape(B, Lp, 2 * D)
            q_scale = DIFF_HD ** -0.5 * math.log2(math.e)
            q = _proj(h2, (da_w_q[j] * q_scale).astype(BF16)).reshape(B, Lp, D)
            lam_init = 0.8 - 0.6 * math.exp(-0.3 * l)
            lam4 = jnp.stack([da_lam_q1[j], da_lam_k1[j], da_lam_q2[j], da_lam_k2[j]])
            o = _diff_attn(q, kv, _pad_rows(lam4, SUBLANES), row(da_subln_g[j]), lam_init)
            h2 = _mm_res_ln(o.reshape(Np, D), None, da_w_out[j].astype(BF16), h2,
                            row(ln_mix_g[l]), row(ln_mix_b[l]))
        h2 = _moe_layer(h2, l, row(ln_ffn_g[l]), row(ln_ffn_b[l]), moe_router_w, moe_router_b,
                        moe_w_gate, moe_w_up, moe_w_down, moe_sh_gate, moe_sh_up, moe_sh_down, tri)
        h = h2.reshape(B, Lp, D)
    return h[:, N_META:L]


_trunk_jit = jax.jit(_trunk)


def kernel(x, meta_tokens, ln_mix_g, ln_mix_b, ln_ffn_g, ln_ffn_b, rw_mu, rw_w_rkv, rw_w0, rw_w_l1, rw_w_l2, rw_a0, rw_a_l1, rw_a_l2, rw_g_l1, rw_g_l2, rw_k_k, rw_k_a, rw_r_k, rw_lnx_g, rw_lnx_b, rw_w_out, rw_v0, rw_v_l1, rw_v_l2, kv_w, da_w_q, da_lam_q1, da_lam_k1, da_lam_q2, da_lam_k2, da_subln_g, da_w_out, moe_router_w, moe_router_b, moe_w_gate, moe_w_up, moe_w_down, moe_sh_gate, moe_sh_up, moe_sh_down):
    return _trunk_jit(x, meta_tokens, ln_mix_g, ln_mix_b, ln_ffn_g, ln_ffn_b, rw_mu, rw_w_rkv, rw_w0, rw_w_l1,
                      rw_w_l2, rw_a0, rw_a_l1, rw_a_l2, rw_g_l1, rw_g_l2, rw_k_k, rw_k_a, rw_r_k, rw_lnx_g,
                      rw_lnx_b, rw_w_out, rw_v0, rw_v_l1, rw_v_l2, kv_w, da_w_q, da_lam_q1, da_lam_k1,
                      da_lam_q2, da_lam_k2, da_subln_g, da_w_out, moe_router_w, moe_router_b, moe_w_gate,
                      moe_w_up, moe_w_down, moe_sh_gate, moe_sh_up, moe_sh_down)
```

```python
import functools
import math

import numpy as np
import jax
import jax.numpy as jnp
from jax import lax
from jax.experimental import pallas as pl
from jax.experimental.pallas import tpu as pltpu

F32 = jnp.float32
BF16 = jnp.bfloat16
I32 = jnp.int32

D_MODEL = 1024
DEPTH = 4
N_META = 16
N_A_LAYERS = DEPTH // 2
RWKV_HEAD = 64
RWKV_HEADS = D_MODEL // RWKV_HEAD
RWKV_GN_EPS = 64e-5
DIFF_HEADS = 8
DIFF_HD = D_MODEL // (2 * DIFF_HEADS)
N_EXPERTS = 64
TOP_K = 8
N_GROUPS = 8
TOPK_GROUPS = 4
EXPERT_FF = D_MODEL // 4
ROUTED_SCALE = 2.5
DN_ALPHA = (2 * DEPTH) ** 0.25
LN_EPS = 1e-5

LANES = 128
SUBLANES = 8
VMEM_LIMIT_BYTES = 56 * 1024 * 1024
DMA_QUEUES = 2
ROW_TILES = D_MODEL // LANES

SEQ_ALIGN = 256
PROJ_TM = 256
ROW_TM = 512
SCAN_TC = 128
SCAN_SUB = SUBLANES
ATT_T = 256
ROUTE_TM = 256
MOE_BLK = 512
DISP_TM = 512
COMB_TM = 256

_SIGMA = (0, 4, 1, 5, 2, 6, 3, 7)


def _cparams(sem):
    return pltpu.CompilerParams(dimension_semantics=sem, vmem_limit_bytes=VMEM_LIMIT_BYTES)


def _dot(a, b):
    return jnp.dot(a, b, preferred_element_type=F32)


def _full(shape):
    nd = len(shape)
    return pl.BlockSpec(shape, lambda *_: (0,) * nd)


def _layer_norm(y, g, b):
    mu = jnp.mean(y, axis=-1, keepdims=True)
    d = y - mu
    var = jnp.mean(d * d, axis=-1, keepdims=True)
    return d * lax.rsqrt(var + LN_EPS) * g + b


def _perm_cols():
    L = np.arange(D_MODEL)
    j, p, hh = L // LANES, (L % LANES) // 16, L % 16
    i = 8 * j + np.asarray(_SIGMA)[p]
    return (hh * RWKV_HEAD + i).astype(np.int32)


def _rwkv_proj_kernel(has_vres, *refs):
    if has_vres:
        (x_ref, xp_ref, mu_ref, wr_ref, wk_ref, wv_ref, w0_ref, wl1_ref, wl2_ref, a0_ref, al1_ref,
         al2_ref, gl1_ref, gl2_ref, kk_ref, ka_ref, vf_ref, v0_ref, vl1_ref, vl2_ref,
         r_o, w_o, k_o, v_o, a_o, b_o, g_o) = refs
    else:
        (x_ref, xp_ref, mu_ref, wr_ref, wk_ref, wv_ref, w0_ref, wl1_ref, wl2_ref, a0_ref, al1_ref,
         al2_ref, gl1_ref, gl2_ref, kk_ref, ka_ref,
         r_o, w_o, k_o, v_o, a_o, b_o, g_o) = refs
    i = pl.program_id(1)
    x = x_ref[0]
    prev = xp_ref[0][SUBLANES - 1:SUBLANES, :]
    prev = jnp.where(i == 0, 0.0, prev)
    row = lax.broadcasted_iota(I32, x.shape, 0)
    xprev = jnp.where(row == 0, prev, pltpu.roll(x, 1, 0))
    dx = xprev - x

    def mix(s):
        return (x + dx * mu_ref[s:s + 1, :]).astype(BF16)

    xv = mix(2)
    r = _dot(mix(0), wr_ref[...])
    k = _dot(mix(1), wk_ref[...])
    v = _dot(xv, wv_ref[...])
    zw = w0_ref[...] + _dot(jnp.tanh(_dot(mix(3), wl1_ref[...])).astype(BF16), wl2_ref[...])
    decay = jnp.exp(-math.exp(-0.5) * jax.nn.sigmoid(zw))
    a = jax.nn.sigmoid(a0_ref[...] + _dot(_dot(mix(4), al1_ref[...]).astype(BF16), al2_ref[...]))
    g = _dot(jax.nn.sigmoid(_dot(mix(5), gl1_ref[...])).astype(BF16), gl2_ref[...])
    if has_vres:
        gate_v = jax.nn.sigmoid(v0_ref[...] + _dot(_dot(xv, vl1_ref[...]).astype(BF16), vl2_ref[...]))
        v = v + (vf_ref[0] - v) * gate_v
    kk = k * kk_ref[...]
    sq = kk * kk
    ss = functools.reduce(jnp.add, [sq[:, j * LANES:(j + 1) * LANES] for j in range(D_MODEL // LANES)])
    for shift in (16, 32, 64):
        ss = ss + pltpu.roll(ss, shift, 1)
    inv = 1.0 / jnp.maximum(jnp.sqrt(ss), 1e-12)
    kk = kk * jnp.concatenate([inv] * (D_MODEL // LANES), axis=1)
    k = k * (1.0 + (a - 1.0) * ka_ref[...])
    r_o[0] = r
    w_o[0] = decay
    k_o[0] = k
    v_o[0] = v
    a_o[0] = -kk
    b_o[0] = kk * a
    g_o[0] = g


def _rwkv_proj(h, vfirst, p):
    B, Lp, D = h.shape
    TM = PROJ_TM
    has_vres = vfirst is not None
    tile = pl.BlockSpec((1, TM, D), lambda b, i: (b, i, 0))
    prev8 = pl.BlockSpec((1, SUBLANES, D), lambda b, i: (b, jnp.maximum(i * (TM // SUBLANES) - 1, 0), 0))
    names = ["mu", "wr", "wk", "wv", "w0", "wl1", "wl2", "a0", "al1", "al2", "gl1", "gl2", "kk", "ka"]
    args = [h, h] + [p[n] for n in names]
    specs = [tile, prev8] + [_full(p[n].shape) for n in names]
    if has_vres:
        args += [vfirst, p["v0"], p["vl1"], p["vl2"]]
        specs += [tile, _full(p["v0"].shape), _full(p["vl1"].shape), _full(p["vl2"].shape)]
    out = jax.ShapeDtypeStruct((B, Lp, D), F32)
    return pl.pallas_call(
        functools.partial(_rwkv_proj_kernel, has_vres),
        out_shape=[out] * 7,
        grid=(B, Lp // TM),
        in_specs=specs,
        out_specs=[tile] * 7,
        compiler_params=_cparams(("parallel", "arbitrary")),
        name="rwkv_proj",
    )(*args)


def _wkv_kernel(r_ref, w_ref, k_ref, v_ref, a_ref, b_ref, rk_ref, lg_ref, lb_ref, o_ref,
                s_ref, ma_ref, mwr_ref, mw_ref, mb_ref, mk_ref, zv_ref, br_ref, kr_ref, rkk_ref, yc_ref):
    NV = RWKV_HEAD
    NCOL = 2 * D_MODEL // LANES
    NRB = NV // SUBLANES

    @pl.when(pl.program_id(0) == 0)
    def _():
        s_ref[...] = jnp.zeros_like(s_ref)

    lane = lax.broadcasted_iota(I32, (SUBLANES, LANES), 1)
    even = ((lane // 16) % 2) == 0
    grp = lane // 32

    def cs(j):
        return slice(j * LANES, (j + 1) * LANES)

    def merge(ref, t0):
        x0 = ref[0, pl.ds(t0, SUBLANES), :]
        x1 = ref[1, pl.ds(t0, SUBLANES), :]
        cols = []
        for j in range(D_MODEL // LANES):
            a0 = x0[:, cs(j)]
            a1 = x1[:, cs(j)]
            cols.append(jnp.where(even, a0, pltpu.roll(a1, 16, 1)))
            cols.append(jnp.where(even, pltpu.roll(a0, LANES - 16, 1), a1))
        return cols

    def fold(x):
        return (x + pltpu.roll(x, 32, 1)) + (pltpu.roll(x, 64, 1) + pltpu.roll(x, 96, 1))

    def colsum(xs):
        acc = xs[0]
        for x in xs[1:]:
            acc = acc + x
        return acc

    def sub_chunk(c, carry):
        t0 = pl.multiple_of(c * SCAN_SUB, SCAN_SUB)
        R = merge(r_ref, t0)
        W = merge(w_ref, t0)
        K = merge(k_ref, t0)
        V = merge(v_ref, t0)
        A = merge(a_ref, t0)
        Bm = merge(b_ref, t0)
        trow_id = lax.broadcasted_iota(I32, (SUBLANES, LANES), 0)
        for j in range(NCOL):
            p = W[j]
            for sh in (1, 2, 4):
                p = p * jnp.where(trow_id >= sh, pltpu.roll(p, sh, 0), 1.0)
            p_prev = jnp.where(trow_id >= 1, pltpu.roll(p, 1, 0), 1.0)
            p_inv = 1.0 / p
            ma_ref[j] = A[j] * p_prev
            mw_ref[j] = p
            mwr_ref[j] = R[j] * p
            mb_ref[j] = Bm[j] * p_inv
            mk_ref[j] = K[j] * p_inv
        br_ref[...] = fold(colsum([Bm[j] * R[j] for j in range(NCOL)]))
        kr_ref[...] = fold(colsum([K[j] * R[j] for j in range(NCOL)]))
        rkk_ref[...] = fold(colsum([K[j] * R[j] * rk_ref[:, cs(j)] for j in range(NCOL)]))
        for j in range(NCOL):
            rolled = [V[j]] + [pltpu.roll(V[j], 32 * m, 1) for m in range(1, 4)]
            for q in range(4):
                z = rolled[(3 - q) % 4]
                for pos in range(2, -1, -1):
                    z = jnp.where(grp == pos, rolled[(pos - q) % 4], z)
                zv_ref[pl.ds(4 * j + q, SUBLANES, stride=NV), :] = z

        def pair_sum(xs):
            return colsum(xs[0::2]) + colsum(xs[1::2])

        def step(t):
            trow = slice(t, t + 1)
            base = t * NV
            brt = br_ref[trow, :]
            krt = kr_ref[trow, :]
            rkt = rkk_ref[trow, :]
            vcol, y = [], []
            for i in range(NRB):
                rows = slice(i * SUBLANES, (i + 1) * SUBLANES)
                vc = zv_ref[pl.ds(base + i * SUBLANES, SUBLANES), :]
                pa, py = [], []
                for j in range(NCOL):
                    sj = s_ref[rows, cs(j)]
                    pa.append(sj * ma_ref[j, trow, :])
                    py.append(sj * mwr_ref[j, trow, :])
                    s_ref[rows, cs(j)] = sj + vc * mk_ref[j, trow, :]
                sa = fold(pair_sum(pa))
                yp = fold(pair_sum(py))
                for j in range(NCOL):
                    s_ref[rows, cs(j)] = s_ref[rows, cs(j)] + sa * mb_ref[j, trow, :]
                vcol.append(vc)
                y.append(yp + sa * brt + vc * krt)
            mean = jnp.sum(colsum(y), axis=0, keepdims=True) * (1.0 / NV)
            d = [yi - mean for yi in y]
            var = jnp.sum(colsum([di * di for di in d]), axis=0, keepdims=True) * (1.0 / NV)
            inv = lax.rsqrt(var + RWKV_GN_EPS)
            for i in range(NRB):
                rows = slice(i * SUBLANES, (i + 1) * SUBLANES)
                out = d[i] * inv * lg_ref[rows, :] + lb_ref[rows, :] + vcol[i] * rkt
                yc_ref[pl.ds(base + i * SUBLANES, SUBLANES), :] = out

        for t in range(SCAN_SUB):
            step(t)
        last = slice(SCAN_SUB - 1, SCAN_SUB)
        for j in range(NCOL):
            s_ref[:, cs(j)] = s_ref[:, cs(j)] * mw_ref[j, last, :]

        cols = []
        for j in range(NCOL):
            acc = None
            for q in range(4):
                z = yc_ref[pl.ds(4 * j + q, SUBLANES, stride=NV), :]
                acc = z if q == 0 else jnp.where(grp == q, z, acc)
            cols.append(acc)
        for j in range(D_MODEL // LANES):
            c0, c1 = cols[2 * j], cols[2 * j + 1]
            o_ref[0, pl.ds(t0, SUBLANES), cs(j)] = jnp.where(even, c0, pltpu.roll(c1, 16, 1))
            o_ref[1, pl.ds(t0, SUBLANES), cs(j)] = jnp.where(even, pltpu.roll(c0, LANES - 16, 1), c1)
        return carry

    lax.fori_loop(0, SCAN_TC // SCAN_SUB, sub_chunk, 0)


def _wkv_scan(r, w, k, v, a, b, rk_m, lg_t, lb_t):
    B, Lp, D = r.shape
    assert B == 2, "the scan packs exactly two batch rows into the lane dimension"
    blk = pl.BlockSpec((B, SCAN_TC, D), lambda i: (0, i, 0))
    vm = lambda *s: pltpu.VMEM(s, F32)
    return pl.pallas_call(
        _wkv_kernel,
        out_shape=jax.ShapeDtypeStruct((B, Lp, D), F32),
        grid=(Lp // SCAN_TC,),
        in_specs=[blk] * 6 + [_full(rk_m.shape), _full(lg_t.shape), _full(lb_t.shape)],
        out_specs=blk,
        scratch_shapes=[vm(RWKV_HEAD, 2 * D)] + [vm(2 * D // LANES, SCAN_SUB, LANES)] * 5
        + [vm(SCAN_SUB * RWKV_HEAD, LANES)] + [vm(SCAN_SUB, LANES)] * 3
        + [vm(SCAN_SUB * RWKV_HEAD, LANES)],
        compiler_params=_cparams(("arbitrary",)),
        name="wkv_scan",
    )(r, w, k, v, a, b, rk_m, lg_t, lb_t)


def _mm_res_ln_kernel(has_gate, *refs):
    if has_gate:
        z_ref, g_ref, w_ref, h_ref, lg_ref, lb_ref, o_ref = refs
        z = (z_ref[...] * g_ref[...]).astype(BF16)
    else:
        z_ref, w_ref, h_ref, lg_ref, lb_ref, o_ref = refs
        z = z_ref[...].astype(BF16)
    y = DN_ALPHA * h_ref[...] + _dot(z, w_ref[...])
    o_ref[...] = _layer_norm(y, lg_ref[...], lb_ref[...])


def _mm_res_ln(z, gate, w, h, lg, lb):
    Np, D = h.shape
    TM = ROW_TM
    tile = pl.BlockSpec((TM, D), lambda i: (i, 0))
    has_gate = gate is not None
    args = [z] + ([gate] if has_gate else []) + [w, h, lg, lb]
    specs = [tile] + ([tile] if has_gate else []) + [_full(w.shape), tile, _full(lg.shape), _full(lb.shape)]
    return pl.pallas_call(
        functools.partial(_mm_res_ln_kernel, has_gate),
        out_shape=jax.ShapeDtypeStruct((Np, D), F32),
        grid=(Np // TM,),
        in_specs=specs,
        out_specs=tile,
        compiler_params=_cparams(("parallel",)),
        name="mm_res_ln",
    )(*args)


def _proj_kernel(x_ref, w_ref, o_ref):
    o_ref[...] = _dot(x_ref[...].astype(BF16), w_ref[...]).astype(o_ref.dtype)


def _proj(x, w):
    Np, D = x.shape
    Nout = w.shape[1]
    TM = ROW_TM
    return pl.pallas_call(
        _proj_kernel,
        out_shape=jax.ShapeDtypeStruct((Np, Nout), BF16),
        grid=(Np // TM,),
        in_specs=[pl.BlockSpec((TM, D), lambda i: (i, 0)), _full(w.shape)],
        out_specs=pl.BlockSpec((TM, Nout), lambda i: (i, 0)),
        compiler_params=_cparams(("parallel",)),
        name="proj",
    )(x, w)


def _attn_kernel(lam_init, lam_ref, sg_ref, q_ref, k_ref, v_ref, o_ref,
                 sa_ref, sb_ref, m1_ref, l1_ref, acc1_ref, m2_ref, l2_ref, acc2_ref):
    T = ATT_T
    qi = pl.program_id(2)
    last = pl.num_programs(2) - 1
    q = q_ref[0]
    lane = lax.broadcasted_iota(I32, q.shape, 1)
    zero = jnp.zeros_like(q)
    q12 = jnp.concatenate([jnp.where(lane < DIFF_HD, q, zero), jnp.where(lane >= DIFF_HD, q, zero)], axis=0)
    m1_ref[...] = jnp.full_like(m1_ref, -1e30)
    m2_ref[...] = jnp.full_like(m2_ref, -1e30)
    l1_ref[...] = jnp.zeros_like(l1_ref)
    l2_ref[...] = jnp.zeros_like(l2_ref)
    acc1_ref[...] = jnp.zeros_like(acc1_ref)
    acc2_ref[...] = jnp.zeros_like(acc2_ref)
    nt = (((1,), (1,)), ((), ()))
    ones = jnp.ones((T, LANES), BF16)
    rel = lax.broadcasted_iota(I32, (T, T), 1) - lax.broadcasted_iota(I32, (T, T), 0)

    def chunk_start(kc):
        return pl.multiple_of(jnp.minimum(kc, last) * T, T)

    def halves(x):
        return [x[:, c * LANES:(c + 1) * LANES] for c in range(x.shape[1] // LANES)]

    def scores(g, buf):
        for half in range(2):
            kb = k_ref[0, pl.ds(chunk_start(2 * g + half), T), :]
            s12 = lax.dot_general(q12, kb, nt, preferred_element_type=F32)
            buf[0, :, half * T:(half + 1) * T] = s12[:T]
            buf[1, :, half * T:(half + 1) * T] = s12[T:]

    def consume(g, buf, masked):
        vext = jnp.concatenate(
            [jnp.concatenate([v_ref[0, pl.ds(chunk_start(2 * g + half), T), :], ones], axis=1) for half in range(2)],
            axis=0)
        for si, m_ref, l_ref, acc_ref in ((0, m1_ref, l1_ref, acc1_ref), (1, m2_ref, l2_ref, acc2_ref)):
            s = buf[si]
            if masked:
                s = jnp.concatenate(
                    [jnp.where(rel <= (qi - (2 * g + half)) * T, s[:, half * T:(half + 1) * T], -jnp.inf)
                     for half in range(2)], axis=1)
            parts = halves(s)
            smax = jnp.max(functools.reduce(jnp.maximum, parts), axis=1, keepdims=True)
            m_old = m_ref[...]
            m_new = jnp.maximum(m_old, smax)
            alpha = jnp.exp2(m_old - m_new)
            p = jnp.exp2(s - jnp.concatenate([m_new] * len(parts), axis=1)).astype(BF16)
            d = _dot(p, vext)
            acc_ref[...] = alpha * acc_ref[...] + d[:, :LANES]
            l_ref[...] = alpha * l_ref[...] + d[:, LANES:]
            m_ref[...] = m_new

    n_free = qi // 2
    scores(0, sa_ref)

    def pair(i, carry):
        g = 2 * i
        scores(g + 1, sb_ref)
        consume(g, sa_ref, False)
        scores(g + 2, sa_ref)
        consume(g + 1, sb_ref, False)
        return carry

    n_pairs = n_free // 2
    lax.fori_loop(0, n_pairs, pair, 0)
    g_a = 2 * n_pairs
    odd = n_free % 2 == 1

    @pl.when(odd)
    def _():
        scores(g_a + 1, sb_ref)

    consume(g_a, sa_ref, True)

    @pl.when(odd)
    def _():
        consume(g_a + 1, sb_ref, True)


    lam_v = lam_ref[...]
    lam = (jnp.exp(jnp.sum(lam_v[0:1] * lam_v[1:2], axis=1, keepdims=True))
           - jnp.exp(jnp.sum(lam_v[2:3] * lam_v[3:4], axis=1, keepdims=True)) + lam_init)
    o = acc1_ref[...] / l1_ref[...] - lam * (acc2_ref[...] / l2_ref[...])
    o = o * lax.rsqrt(jnp.mean(o * o, axis=1, keepdims=True) + 1e-5) * sg_ref[...] * (1.0 - lam_init)
    o_ref[0] = o.astype(o_ref.dtype)


def _diff_attn(q, kv, lam4, subln_g, lam_init):
    B, Lp, D = q.shape
    T = ATT_T
    H = DIFF_HEADS
    HW = 2 * DIFF_HD
    vm = lambda *s: pltpu.VMEM(s, F32)
    return pl.pallas_call(
        functools.partial(_attn_kernel, lam_init),
        out_shape=jax.ShapeDtypeStruct((B, Lp, D), BF16),
        grid=(B, H, Lp // T),
        in_specs=[_full(lam4.shape), _full(subln_g.shape),
                  pl.BlockSpec((1, T, HW), lambda b, h, i: (b, i, h)),
                  pl.BlockSpec((1, Lp, HW), lambda b, h, i: (b, 0, h)),
                  pl.BlockSpec((1, Lp, HW), lambda b, h, i: (b, 0, H + h))],
        out_specs=pl.BlockSpec((1, T, HW), lambda b, h, i: (b, i, h)),
        scratch_shapes=[vm(2, T, 2 * T), vm(2, T, 2 * T),
                        vm(T, LANES), vm(T, LANES), vm(T, HW), vm(T, LANES), vm(T, LANES), vm(T, HW)],
        compiler_params=_cparams(("parallel", "parallel", "arbitrary")),
        name="diff_attn",
    )(lam4, subln_g, q, kv, kv)


def _route_kernel(x_ref, rw_ref, rb_ref, sg_ref, su_ref, sd_ref, tri_ref,
                  sh_o, idx_o, gate_o, rank_o, cnt_o, carry_ref):
    TM = ROUTE_TM
    G, EG = N_GROUPS, N_EXPERTS // N_GROUPS

    @pl.when(pl.program_id(0) == 0)
    def _():
        carry_ref[...] = jnp.zeros_like(carry_ref)

    x = x_ref[...]
    xb = x.astype(BF16)
    hmid = _dot(xb, sg_ref[...])
    hmid = hmid * jax.nn.sigmoid(hmid) * _dot(xb, su_ref[...])
    sh_o[...] = _dot(hmid.astype(BF16), sd_ref[...])

    logit = lax.dot_general(rw_ref[...], x, (((1,), (1,)), ((), ())),
                            preferred_element_type=F32, precision=lax.Precision.HIGHEST)
    s = jax.nn.sigmoid(logit)
    s3 = s.reshape(G, EG, TM)
    sel3 = (s + rb_ref[...]).reshape(G, EG, TM)
    io_j = lax.broadcasted_iota(I32, (G, EG, TM), 1)
    io_g = lax.broadcasted_iota(I32, (G, 1, TM), 0)
    neg = -jnp.inf
    m1 = jnp.max(sel3, axis=1, keepdims=True)
    i1 = jnp.min(jnp.where(sel3 == m1, io_j, EG), axis=1, keepdims=True)
    m2 = jnp.max(jnp.where(io_j == i1, neg, sel3), axis=1, keepdims=True)
    gsc = m1 + m2
    gkeep = jnp.zeros((G, 1, TM), F32)
    for _ in range(TOPK_GROUPS):
        m = jnp.max(gsc, axis=0, keepdims=True)
        gi = jnp.min(jnp.where(gsc == m, io_g, G), axis=0, keepdims=True)
        hit = io_g == gi
        gkeep = jnp.where(hit, 1.0, gkeep)
        gsc = jnp.where(hit, neg, gsc)
    cur = jnp.where(gkeep > 0.0, sel3, neg)
    io_e = io_g * EG + io_j
    hits, idxs, ws = [], [], []
    for _ in range(TOP_K):
        m = jnp.max(jnp.max(cur, axis=1, keepdims=True), axis=0, keepdims=True)
        ei = jnp.min(jnp.min(jnp.where(cur == m, io_e, N_EXPERTS), axis=1, keepdims=True), axis=0, keepdims=True)
        hit = io_e == ei
        ws.append(jnp.sum(jnp.sum(jnp.where(hit, s3, 0.0), axis=1, keepdims=True), axis=0, keepdims=True))
        cur = jnp.where(hit, neg, cur)
        hits.append(hit)
        idxs.append(ei)
    wsum = ws[0]
    for wv in ws[1:]:
        wsum = wsum + wv
    scale = ROUTED_SCALE / wsum
    onehot = jnp.zeros((G, EG, TM), F32)
    for hit in hits:
        onehot = jnp.where(hit, 1.0, onehot)
    oh2 = onehot.reshape(N_EXPERTS, TM)
    rank_full = (_dot(oh2.astype(BF16), tri_ref[...]) + carry_ref[:, :1]).reshape(G, EG, TM)
    for kk in range(TOP_K):
        rk = jnp.sum(jnp.sum(jnp.where(hits[kk], rank_full, 0.0), axis=1, keepdims=True), axis=0, keepdims=True)
        idx_o[kk:kk + 1, :] = idxs[kk].reshape(1, TM)
        gate_o[kk:kk + 1, :] = (ws[kk] * scale).reshape(1, TM)
        rank_o[kk:kk + 1, :] = rk.reshape(1, TM).astype(I32)
    carry_ref[...] = carry_ref[...] + jnp.sum(oh2, axis=1, keepdims=True)
    cnt_o[...] = carry_ref[...]


def _route_shared(x, rw_t, rb, sg, su, sd, tri):
    Np, D = x.shape
    TM = ROUTE_TM
    tok = pl.BlockSpec((TOP_K, TM), lambda i: (0, i))
    return pl.pallas_call(
        _route_kernel,
        out_shape=[jax.ShapeDtypeStruct((Np, D), F32),
                   jax.ShapeDtypeStruct((TOP_K, Np), I32),
                   jax.ShapeDtypeStruct((TOP_K, Np), F32),
                   jax.ShapeDtypeStruct((TOP_K, Np), I32),
                   jax.ShapeDtypeStruct((N_EXPERTS, LANES), F32)],
        grid=(Np // TM,),
        in_specs=[pl.BlockSpec((TM, D), lambda i: (i, 0)), _full(rw_t.shape), _full(rb.shape),
                  _full(sg.shape), _full(su.shape), _full(sd.shape), _full(tri.shape)],
        out_specs=[pl.BlockSpec((TM, D), lambda i: (i, 0)), tok, tok, tok, _full((N_EXPERTS, LANES))],
        scratch_shapes=[pltpu.VMEM((N_EXPERTS, LANES), F32)],
        compiler_params=_cparams(("arbitrary",)),
        name="moe_route",
    )(x, rw_t, rb, sg, su, sd, tri)


def _tile_copy(src_ref, src_row, dst_ref, dst_row, sem):
    src = src_ref.at[pl.ds(pl.multiple_of(src_row * ROW_TILES, ROW_TILES), ROW_TILES)]
    dst = dst_ref.at[pl.ds(pl.multiple_of(dst_row * ROW_TILES, ROW_TILES), ROW_TILES)]
    return pltpu.make_async_copy(src, dst, sem)


def _rows_to_tiles(x, tiles_ref, n_rows):
    for s in range(ROW_TILES):
        tiles_ref[pl.ds(s, n_rows, stride=ROW_TILES), :] = x[:, s * LANES:(s + 1) * LANES]


def _tiles_to_rows(tiles_ref, n_rows):
    return [tiles_ref[pl.ds(s, n_rows, stride=ROW_TILES), :] for s in range(ROW_TILES)]


def _dispatch_kernel(ps_ref, idx_ref, rank_ref, x_ref, xs_hbm, stage_ref, sem):
    _rows_to_tiles(x_ref[...], stage_ref, DISP_TM)

    def issue(n, c):
        for kk in range(TOP_K):
            dst = ps_ref[idx_ref[kk, n]] + rank_ref[kk, n]
            _tile_copy(stage_ref, n, xs_hbm, dst, sem).start(priority=kk % DMA_QUEUES)
        return c

    lax.fori_loop(0, DISP_TM, issue, 0)

    def drain(n, c):
        for kk in range(TOP_K):
            _tile_copy(stage_ref, n, xs_hbm, 0, sem).wait()
        return c

    lax.fori_loop(0, DISP_TM, drain, 0)


def _dispatch(x, idx_t, rank_t, pad_start, n_rows):
    Np, D = x.shape
    tok = pl.BlockSpec((TOP_K, DISP_TM), lambda i, ps: (0, i), memory_space=pltpu.SMEM)
    return pl.pallas_call(
        _dispatch_kernel,
        out_shape=jax.ShapeDtypeStruct((n_rows * ROW_TILES, LANES), F32),
        grid_spec=pltpu.PrefetchScalarGridSpec(
            num_scalar_prefetch=1,
            grid=(Np // DISP_TM,),
            in_specs=[tok, tok, pl.BlockSpec((DISP_TM, D), lambda i, ps: (i, 0))],
            out_specs=pl.BlockSpec(memory_space=pl.ANY),
            scratch_shapes=[pltpu.VMEM((DISP_TM * ROW_TILES, LANES), F32), pltpu.SemaphoreType.DMA(())],
        ),
        compiler_params=_cparams(("arbitrary",)),
        name="moe_dispatch",
    )(pad_start, idx_t, rank_t, x)


def _expert_kernel(be_ref, nu_ref, xs_ref, wg_ref, wu_ref, wd_ref, ys_ref):
    @pl.when(pl.program_id(0) < nu_ref[0])
    def _():
        x = jnp.concatenate(_tiles_to_rows(xs_ref, MOE_BLK), axis=1).astype(BF16)
        g = _dot(x, wg_ref[0])
        hmid = g * jax.nn.sigmoid(g) * _dot(x, wu_ref[0])
        _rows_to_tiles(_dot(hmid.astype(BF16), wd_ref[0]), ys_ref, MOE_BLK)


def _experts(xs, block_e, n_used, wg, wu, wd):
    D, F = wg.shape[1], wg.shape[2]
    nb = xs.shape[0] // (MOE_BLK * ROW_TILES)
    tiles = pl.BlockSpec((MOE_BLK * ROW_TILES, LANES), lambda i, be, nu: (jnp.minimum(i, nu[0] - 1), 0))
    wspec = lambda s: pl.BlockSpec((1,) + s, lambda i, be, nu: (be[jnp.minimum(i, nu[0] - 1)], 0, 0))
    return pl.pallas_call(
        _expert_kernel,
        out_shape=jax.ShapeDtypeStruct(xs.shape, F32),
        grid_spec=pltpu.PrefetchScalarGridSpec(
            num_scalar_prefetch=2,
            grid=(nb,),
            in_specs=[tiles, wspec((D, F)), wspec((D, F)), wspec((F, D))],
            out_specs=tiles,
        ),
        compiler_params=_cparams(("arbitrary",)),
        name="moe_experts",
    )(block_e, n_used, xs, wg, wu, wd)


def _combine_kernel(ps_ref, idx_ref, rank_ref, gate_ref, sh_ref, h_ref, lg_ref, lb_ref, ys_hbm, o_ref,
                    buf, sem):
    def issue(n, c):
        for kk in range(TOP_K):
            src = ps_ref[idx_ref[kk, n]] + rank_ref[kk, n]
            _tile_copy(ys_hbm, src, buf.at[kk], n, sem).start(priority=kk % DMA_QUEUES)
        return c

    lax.fori_loop(0, COMB_TM, issue, 0)

    def drain(n, c):
        for kk in range(TOP_K):
            _tile_copy(ys_hbm, 0, buf.at[kk], n, sem).wait()
        return c

    lax.fori_loop(0, COMB_TM, drain, 0)
    groups = []
    for s in range(ROW_TILES):
        acc = None
        for kk in range(TOP_K):
            part = gate_ref[kk] * buf[kk, pl.ds(s, COMB_TM, stride=ROW_TILES), :]
            acc = part if kk == 0 else acc + part
        groups.append(acc)
    ffn = sh_ref[...] + jnp.concatenate(groups, axis=1)
    o_ref[...] = _layer_norm(DN_ALPHA * h_ref[...] + ffn, lg_ref[...], lb_ref[...])


def _combine(ys, idx_t, rank_t, pad_start, gate, shared, h, lg, lb):
    Np, D = h.shape
    TM = COMB_TM
    tok = pl.BlockSpec((TOP_K, TM), lambda i, ps: (0, i), memory_space=pltpu.SMEM)
    tile = pl.BlockSpec((TM, D), lambda i, ps: (i, 0))
    vec = pl.BlockSpec((1, D), lambda i, ps: (0, 0))
    return pl.pallas_call(
        _combine_kernel,
        out_shape=jax.ShapeDtypeStruct((Np, D), F32),
        grid_spec=pltpu.PrefetchScalarGridSpec(
            num_scalar_prefetch=1,
            grid=(Np // TM,),
            in_specs=[tok, tok, pl.BlockSpec((TOP_K, TM, LANES), lambda i, ps: (0, i, 0)), tile, tile, vec, vec,
                      pl.BlockSpec(memory_space=pl.ANY)],
            out_specs=tile,
            scratch_shapes=[pltpu.VMEM((TOP_K, TM * ROW_TILES, LANES), F32), pltpu.SemaphoreType.DMA(())],
        ),
        compiler_params=_cparams(("arbitrary",)),
        name="moe_combine",
    )(pad_start, idx_t, rank_t, gate, shared, h, lg, lb, ys)


def _moe_layer(h2, l, lg, lb, router_w, router_b, w_gate, w_up, w_down, sh_gate, sh_up, sh_down, tri):
    Np, D = h2.shape
    shared, idx_t, gate_t, rank_t, cnt = _route_shared(
        h2, router_w[l].T, router_b[l].reshape(N_EXPERTS, 1),
        sh_gate[l].astype(BF16), sh_up[l].astype(BF16), sh_down[l].astype(BF16), tri)
    counts = cnt[:, 0].astype(I32)
    padded = (counts + MOE_BLK - 1) // MOE_BLK * MOE_BLK
    pad_end = jnp.cumsum(padded)
    pad_start = (pad_end - padded).astype(I32)
    n_blocks = Np * TOP_K // MOE_BLK + N_EXPERTS
    n_used = (pad_end[-1:] // MOE_BLK).astype(I32)
    blk_row0 = jnp.arange(n_blocks, dtype=I32) * MOE_BLK
    block_e = jnp.minimum(jnp.sum((pad_end[None, :] <= blk_row0[:, None]).astype(I32), axis=1), N_EXPERTS - 1)
    xs = _dispatch(h2, idx_t, rank_t, pad_start, n_blocks * MOE_BLK)
    ys = _experts(xs, block_e, n_used, w_gate[l].astype(BF16), w_up[l].astype(BF16), w_down[l].astype(BF16))
    gate_splat = jnp.broadcast_to(gate_t[:, :, None], (TOP_K, Np, LANES))
    return _combine(ys, idx_t, rank_t, pad_start, gate_splat, shared, h2, lg, lb)


def _pad_cols(w, n):
    return jnp.pad(w, ((0, 0), (0, n - w.shape[1])))


def _pad_rows(w, n):
    return jnp.pad(w, ((0, n - w.shape[0]), (0, 0)))


def _trunk(x, meta_tokens, ln_mix_g, ln_mix_b, ln_ffn_g, ln_ffn_b,
           rw_mu, rw_w_rkv, rw_w0, rw_w_l1, rw_w_l2, rw_a0, rw_a_l1, rw_a_l2,
           rw_g_l1, rw_g_l2, rw_k_k, rw_k_a, rw_r_k, rw_lnx_g, rw_lnx_b, rw_w_out,
           rw_v0, rw_v_l1, rw_v_l2, kv_w,
           da_w_q, da_lam_q1, da_lam_k1, da_lam_q2, da_lam_k2, da_subln_g, da_w_out,
           moe_router_w, moe_router_b, moe_w_gate, moe_w_up, moe_w_down,
           moe_sh_gate, moe_sh_up, moe_sh_down):
    B, S, D = x.shape
    L = S + N_META
    Lp = -(-L // SEQ_ALIGN) * SEQ_ALIGN
    Np = B * Lp
    assert D == D_MODEL and Np % ROW_TM == 0 and Lp % PROJ_TM == 0 and Lp % SCAN_TC == 0 and Lp % ATT_T == 0
    meta = jnp.broadcast_to(meta_tokens[None].astype(x.dtype), (B, N_META, D))
    h = jnp.concatenate([meta, x, jnp.zeros((B, Lp - L, D), x.dtype)], axis=1)

    pc = _perm_cols()
    tri = jnp.asarray(np.triu(np.ones((ROUTE_TM, ROUTE_TM), np.float32), 1)).astype(BF16)
    row = lambda vec: vec.reshape(1, -1)
    lane = np.arange(2 * D_MODEL)
    m_key, m_head = lane // 32, lane % 16
    c128 = np.arange(LANES) % 16
    vrow = np.arange(RWKV_HEAD)

    v_first = None
    kv = None
    for l in range(DEPTH):
        if l < N_A_LAYERS:
            p = {
                "mu": _pad_rows(rw_mu[l], SUBLANES),
                "wr": rw_w_rkv[l, 0][:, pc].astype(BF16),
                "wk": rw_w_rkv[l, 1][:, pc].astype(BF16),
                "wv": rw_w_rkv[l, 2][:, pc].astype(BF16),
                "w0": row(rw_w0[l][pc]),
                "wl1": _pad_cols(rw_w_l1[l], LANES).astype(BF16),
                "wl2": _pad_rows(rw_w_l2[l][:, pc], LANES).astype(BF16),
                "a0": row(rw_a0[l][pc]),
                "al1": _pad_cols(rw_a_l1[l], LANES).astype(BF16),
                "al2": _pad_rows(rw_a_l2[l][:, pc], LANES).astype(BF16),
                "gl1": _pad_cols(rw_g_l1[l], 2 * LANES).astype(BF16),
                "gl2": _pad_rows(rw_g_l2[l][:, pc], 2 * LANES).astype(BF16),
                "kk": row(rw_k_k[l][pc]),
                "ka": row(rw_k_a[l][pc]),
            }
            if l > 0:
                p["v0"] = row(rw_v0[l - 1][pc])
                p["vl1"] = _pad_cols(rw_v_l1[l - 1], LANES).astype(BF16)
                p["vl2"] = _pad_rows(rw_v_l2[l - 1][:, pc], LANES).astype(BF16)
            r, w, k, v, a, b, g = _rwkv_proj(h, v_first if l > 0 else None, p)
            if l == 0:
                v_first = v
            rk_m = rw_r_k[l][m_head, m_key].reshape(1, 2 * D_MODEL)
            lg_t = rw_lnx_g[l].reshape(RWKV_HEADS, RWKV_HEAD)[c128[None, :], vrow[:, None]]
            lb_t = rw_lnx_b[l].reshape(RWKV_HEADS, RWKV_HEAD)[c128[None, :], vrow[:, None]]
            z = _wkv_scan(r, w, k, v, a, b, rk_m, lg_t, lb_t)
            h2 = _mm_res_ln(z.reshape(Np, D), g.reshape(Np, D), rw_w_out[l][pc, :].astype(BF16),
                            h.reshape(Np, D), row(ln_mix_g[l]), row(ln_mix_b[l]))
        else:
            j = l - N_A_LAYERS
            h2 = h.reshape(Np, D)
            if kv is None:
                kv = _proj(h2, kv_w.astype(BF16)).reshape(B, Lp, 2 * D)
            q_scale = DIFF_HD ** -0.5 * math.log2(math.e)
            q = _proj(h2, (da_w_q[j] * q_scale).astype(BF16)).reshape(B, Lp, D)
            lam_init = 0.8 - 0.6 * math.exp(-0.3 * l)
            lam4 = jnp.stack([da_lam_q1[j], da_lam_k1[j], da_lam_q2[j], da_lam_k2[j]])
            o = _diff_attn(q, kv, _pad_rows(lam4, SUBLANES), row(da_subln_g[j]), lam_init)
            h2 = _mm_res_ln(o.reshape(Np, D), None, da_w_out[j].astype(BF16), h2,
                            row(ln_mix_g[l]), row(ln_mix_b[l]))
        h2 = _moe_layer(h2, l, row(ln_ffn_g[l]), row(ln_ffn_b[l]), moe_router_w, moe_router_b,
                        moe_w_gate, moe_w_up, moe_w_down, moe_sh_gate, moe_sh_up, moe_sh_down, tri)
        h = h2.reshape(B, Lp, D)
    return h[:, N_META:L]


_trunk_jit = jax.jit(_trunk)


def kernel(x, meta_tokens, ln_mix_g, ln_mix_b, ln_ffn_g, ln_ffn_b, rw_mu, rw_w_rkv, rw_w0, rw_w_l1, rw_w_l2, rw_a0, rw_a_l1, rw_a_l2, rw_g_l1, rw_g_l2, rw_k_k, rw_k_a, rw_r_k, rw_lnx_g, rw_lnx_b, rw_w_out, rw_v0, rw_v_l1, rw_v_l2, kv_w, da_w_q, da_lam_q1, da_lam_k1, da_lam_q2, da_lam_k2, da_subln_g, da_w_out, moe_router_w, moe_router_b, moe_w_gate, moe_w_up, moe_w_down, moe_sh_gate, moe_sh_up, moe_sh_down):
    return _trunk_jit(x, meta_tokens, ln_mix_g, ln_mix_b, ln_ffn_g, ln_ffn_b, rw_mu, rw_w_rkv, rw_w0, rw_w_l1,
                      rw_w_l2, rw_a0, rw_a_l1, rw_a_l2, rw_g_l1, rw_g_l2, rw_k_k, rw_k_a, rw_r_k, rw_lnx_g,
                      rw_lnx_b, rw_w_out, rw_v0, rw_v_l1, rw_v_l2, kv_w, da_w_q, da_lam_q1, da_lam_k1,
                      da_lam_q2, da_lam_k2, da_subln_g, da_w_out, moe_router_w, moe_router_b, moe_w_gate,
                      moe_w_up, moe_w_down, moe_sh_gate, moe_sh_up, moe_sh_down)
```

```python
import functools
import math

import numpy as np
import jax
import jax.numpy as jnp
from jax import lax
from jax.experimental import pallas as pl
from jax.experimental.pallas import tpu as pltpu

F32 = jnp.float32
BF16 = jnp.bfloat16
I32 = jnp.int32

D_MODEL = 1024
DEPTH = 4
N_META = 16
N_A_LAYERS = DEPTH // 2
RWKV_HEAD = 64
RWKV_HEADS = D_MODEL // RWKV_HEAD
RWKV_GN_EPS = 64e-5
DIFF_HEADS = 8
DIFF_HD = D_MODEL // (2 * DIFF_HEADS)
N_EXPERTS = 64
TOP_K = 8
N_GROUPS = 8
TOPK_GROUPS = 4
EXPERT_FF = D_MODEL // 4
ROUTED_SCALE = 2.5
DN_ALPHA = (2 * DEPTH) ** 0.25
LN_EPS = 1e-5

LANES = 128
SUBLANES = 8
VMEM_LIMIT_BYTES = 56 * 1024 * 1024
ROW_TILES = D_MODEL // LANES

SEQ_ALIGN = 256
PROJ_TM = 256
ROW_TM = 512
SCAN_TC = 128
SCAN_SUB = SUBLANES
ATT_TQ = 512
ATT_TK = 256
ROUTE_TM = 256
MOE_BLK = 512
DISP_TM = 512
COMB_TM = 512

_SIGMA = (0, 4, 1, 5, 2, 6, 3, 7)


def _cparams(sem):
    return pltpu.CompilerParams(dimension_semantics=sem, vmem_limit_bytes=VMEM_LIMIT_BYTES)


def _dot(a, b):
    return jnp.dot(a, b, preferred_element_type=F32)


def _full(shape):
    nd = len(shape)
    return pl.BlockSpec(shape, lambda *_: (0,) * nd)


def _layer_norm(y, g, b):
    mu = jnp.mean(y, axis=-1, keepdims=True)
    d = y - mu
    var = jnp.mean(d * d, axis=-1, keepdims=True)
    return d * lax.rsqrt(var + LN_EPS) * g + b


def _perm_cols():
    L = np.arange(D_MODEL)
    j, p, hh = L // LANES, (L % LANES) // 16, L % 16
    i = 8 * j + np.asarray(_SIGMA)[p]
    return (hh * RWKV_HEAD + i).astype(np.int32)


def _rwkv_proj_kernel(has_vres, *refs):
    if has_vres:
        (x_ref, xp_ref, mu_ref, wr_ref, wk_ref, wv_ref, w0_ref, wl1_ref, wl2_ref, a0_ref, al1_ref,
         al2_ref, gl1_ref, gl2_ref, kk_ref, ka_ref, vf_ref, v0_ref, vl1_ref, vl2_ref,
         r_o, w_o, k_o, v_o, a_o, b_o, g_o) = refs
    else:
        (x_ref, xp_ref, mu_ref, wr_ref, wk_ref, wv_ref, w0_ref, wl1_ref, wl2_ref, a0_ref, al1_ref,
         al2_ref, gl1_ref, gl2_ref, kk_ref, ka_ref,
         r_o, w_o, k_o, v_o, a_o, b_o, g_o) = refs
    i = pl.program_id(1)
    x = x_ref[0]
    prev = xp_ref[0][SUBLANES - 1:SUBLANES, :]
    prev = jnp.where(i == 0, 0.0, prev)
    row = lax.broadcasted_iota(I32, x.shape, 0)
    xprev = jnp.where(row == 0, prev, pltpu.roll(x, 1, 0))
    dx = xprev - x

    def mix(s):
        return (x + dx * mu_ref[s:s + 1, :]).astype(BF16)

    xv = mix(2)
    r = _dot(mix(0), wr_ref[...])
    k = _dot(mix(1), wk_ref[...])
    v = _dot(xv, wv_ref[...])
    zw = w0_ref[...] + _dot(jnp.tanh(_dot(mix(3), wl1_ref[...])).astype(BF16), wl2_ref[...])
    decay = jnp.exp(-math.exp(-0.5) * jax.nn.sigmoid(zw))
    a = jax.nn.sigmoid(a0_ref[...] + _dot(_dot(mix(4), al1_ref[...]).astype(BF16), al2_ref[...]))
    g = _dot(jax.nn.sigmoid(_dot(mix(5), gl1_ref[...])).astype(BF16), gl2_ref[...])
    if has_vres:
        gate_v = jax.nn.sigmoid(v0_ref[...] + _dot(_dot(xv, vl1_ref[...]).astype(BF16), vl2_ref[...]))
        v = v + (vf_ref[0] - v) * gate_v
    kk = k * kk_ref[...]
    sq = kk * kk
    ss = functools.reduce(jnp.add, [sq[:, j * LANES:(j + 1) * LANES] for j in range(D_MODEL // LANES)])
    for shift in (16, 32, 64):
        ss = ss + pltpu.roll(ss, shift, 1)
    inv = 1.0 / jnp.maximum(jnp.sqrt(ss), 1e-12)
    kk = kk * jnp.concatenate([inv] * (D_MODEL // LANES), axis=1)
    k = k * (1.0 + (a - 1.0) * ka_ref[...])
    r_o[0] = r
    w_o[0] = decay
    k_o[0] = k
    v_o[0] = v
    a_o[0] = -kk
    b_o[0] = kk * a
    g_o[0] = g


def _rwkv_proj(h, vfirst, p):
    B, Lp, D = h.shape
    TM = PROJ_TM
    has_vres = vfirst is not None
    tile = pl.BlockSpec((1, TM, D), lambda b, i: (b, i, 0))
    prev8 = pl.BlockSpec((1, SUBLANES, D), lambda b, i: (b, jnp.maximum(i * (TM // SUBLANES) - 1, 0), 0))
    names = ["mu", "wr", "wk", "wv", "w0", "wl1", "wl2", "a0", "al1", "al2", "gl1", "gl2", "kk", "ka"]
    args = [h, h] + [p[n] for n in names]
    specs = [tile, prev8] + [_full(p[n].shape) for n in names]
    if has_vres:
        args += [vfirst, p["v0"], p["vl1"], p["vl2"]]
        specs += [tile, _full(p["v0"].shape), _full(p["vl1"].shape), _full(p["vl2"].shape)]
    out = jax.ShapeDtypeStruct((B, Lp, D), F32)
    return pl.pallas_call(
        functools.partial(_rwkv_proj_kernel, has_vres),
        out_shape=[out] * 7,
        grid=(B, Lp // TM),
        in_specs=specs,
        out_specs=[tile] * 7,
        compiler_params=_cparams(("parallel", "arbitrary")),
        name="rwkv_proj",
    )(*args)


def _wkv_kernel(r_ref, w_ref, k_ref, v_ref, a_ref, b_ref, rk_ref, lg_ref, lb_ref, o_ref,
                s_ref, ma_ref, mwr_ref, mw_ref, mb_ref, mk_ref, zv_ref, br_ref, kr_ref, rkk_ref, yc_ref):
    NV = RWKV_HEAD
    NCOL = 2 * D_MODEL // LANES
    NRB = NV // SUBLANES

    @pl.when(pl.program_id(0) == 0)
    def _():
        s_ref[...] = jnp.zeros_like(s_ref)

    lane = lax.broadcasted_iota(I32, (SUBLANES, LANES), 1)
    even = ((lane // 16) % 2) == 0
    grp = lane // 32

    def cs(j):
        return slice(j * LANES, (j + 1) * LANES)

    def merge(ref, t0):
        x0 = ref[0, pl.ds(t0, SUBLANES), :]
        x1 = ref[1, pl.ds(t0, SUBLANES), :]
        cols = []
        for j in range(D_MODEL // LANES):
            a0 = x0[:, cs(j)]
            a1 = x1[:, cs(j)]
            cols.append(jnp.where(even, a0, pltpu.roll(a1, 16, 1)))
            cols.append(jnp.where(even, pltpu.roll(a0, LANES - 16, 1), a1))
        return cols

    def fold(x):
        return (x + pltpu.roll(x, 32, 1)) + (pltpu.roll(x, 64, 1) + pltpu.roll(x, 96, 1))

    def colsum(xs):
        acc = xs[0]
        for x in xs[1:]:
            acc = acc + x
        return acc

    def sub_chunk(c, carry):
        t0 = pl.multiple_of(c * SCAN_SUB, SCAN_SUB)
        R = merge(r_ref, t0)
        W = merge(w_ref, t0)
        K = merge(k_ref, t0)
        V = merge(v_ref, t0)
        A = merge(a_ref, t0)
        Bm = merge(b_ref, t0)
        trow_id = lax.broadcasted_iota(I32, (SUBLANES, LANES), 0)
        for j in range(NCOL):
            p = W[j]
            for sh in (1, 2, 4):
                p = p * jnp.where(trow_id >= sh, pltpu.roll(p, sh, 0), 1.0)
            p_prev = jnp.where(trow_id >= 1, pltpu.roll(p, 1, 0), 1.0)
            p_inv = 1.0 / p
            ma_ref[j] = A[j] * p_prev
            mw_ref[j] = p
            mwr_ref[j] = R[j] * p
            mb_ref[j] = Bm[j] * p_inv
            mk_ref[j] = K[j] * p_inv
        br_ref[...] = fold(colsum([Bm[j] * R[j] for j in range(NCOL)]))
        kr_ref[...] = fold(colsum([K[j] * R[j] for j in range(NCOL)]))
        rkk_ref[...] = fold(colsum([K[j] * R[j] * rk_ref[:, cs(j)] for j in range(NCOL)]))
        for j in range(NCOL):
            rolled = [V[j]] + [pltpu.roll(V[j], 32 * m, 1) for m in range(1, 4)]
            for q in range(4):
                z = rolled[(3 - q) % 4]
                for pos in range(2, -1, -1):
                    z = jnp.where(grp == pos, rolled[(pos - q) % 4], z)
                zv_ref[pl.ds(4 * j + q, SUBLANES, stride=NV), :] = z

        def pair_sum(xs):
            return colsum(xs[0::2]) + colsum(xs[1::2])

        def step(t):
            trow = slice(t, t + 1)
            base = t * NV
            brt = br_ref[trow, :]
            krt = kr_ref[trow, :]
            rkt = rkk_ref[trow, :]
            vcol, y = [], []
            for i in range(NRB):
                rows = slice(i * SUBLANES, (i + 1) * SUBLANES)
                vc = zv_ref[pl.ds(base + i * SUBLANES, SUBLANES), :]
                pa, py = [], []
                for j in range(NCOL):
                    sj = s_ref[rows, cs(j)]
                    pa.append(sj * ma_ref[j, trow, :])
                    py.append(sj * mwr_ref[j, trow, :])
                    s_ref[rows, cs(j)] = sj + vc * mk_ref[j, trow, :]
                sa = fold(pair_sum(pa))
                yp = fold(pair_sum(py))
                for j in range(NCOL):
                    s_ref[rows, cs(j)] = s_ref[rows, cs(j)] + sa * mb_ref[j, trow, :]
                vcol.append(vc)
                y.append(yp + sa * brt + vc * krt)
            mean = jnp.sum(colsum(y), axis=0, keepdims=True) * (1.0 / NV)
            d = [yi - mean for yi in y]
            var = jnp.sum(colsum([di * di for di in d]), axis=0, keepdims=True) * (1.0 / NV)
            inv = lax.rsqrt(var + RWKV_GN_EPS)
            for i in range(NRB):
                rows = slice(i * SUBLANES, (i + 1) * SUBLANES)
                out = d[i] * inv * lg_ref[rows, :] + lb_ref[rows, :] + vcol[i] * rkt
                yc_ref[pl.ds(base + i * SUBLANES, SUBLANES), :] = out

        for t in range(SCAN_SUB):
            step(t)
        last = slice(SCAN_SUB - 1, SCAN_SUB)
        for j in range(NCOL):
            s_ref[:, cs(j)] = s_ref[:, cs(j)] * mw_ref[j, last, :]

        cols = []
        for j in range(NCOL):
            acc = None
            for q in range(4):
                z = yc_ref[pl.ds(4 * j + q, SUBLANES, stride=NV), :]
                acc = z if q == 0 else jnp.where(grp == q, z, acc)
            cols.append(acc)
        for j in range(D_MODEL // LANES):
            c0, c1 = cols[2 * j], cols[2 * j + 1]
            o_ref[0, pl.ds(t0, SUBLANES), cs(j)] = jnp.where(even, c0, pltpu.roll(c1, 16, 1))
            o_ref[1, pl.ds(t0, SUBLANES), cs(j)] = jnp.where(even, pltpu.roll(c0, LANES - 16, 1), c1)
        return carry

    lax.fori_loop(0, SCAN_TC // SCAN_SUB, sub_chunk, 0)


def _wkv_scan(r, w, k, v, a, b, rk_m, lg_t, lb_t):
    B, Lp, D = r.shape
    assert B == 2, "the scan packs exactly two batch rows into the lane dimension"
    blk = pl.BlockSpec((B, SCAN_TC, D), lambda i: (0, i, 0))
    vm = lambda *s: pltpu.VMEM(s, F32)
    return pl.pallas_call(
        _wkv_kernel,
        out_shape=jax.ShapeDtypeStruct((B, Lp, D), F32),
        grid=(Lp // SCAN_TC,),
        in_specs=[blk] * 6 + [_full(rk_m.shape), _full(lg_t.shape), _full(lb_t.shape)],
        out_specs=blk,
        scratch_shapes=[vm(RWKV_HEAD, 2 * D)] + [vm(2 * D // LANES, SCAN_SUB, LANES)] * 5
        + [vm(SCAN_SUB * RWKV_HEAD, LANES)] + [vm(SCAN_SUB, LANES)] * 3
        + [vm(SCAN_SUB * RWKV_HEAD, LANES)],
        compiler_params=_cparams(("arbitrary",)),
        name="wkv_scan",
    )(r, w, k, v, a, b, rk_m, lg_t, lb_t)


def _mm_res_ln_kernel(has_gate, *refs):
    if has_gate:
        z_ref, g_ref, w_ref, h_ref, lg_ref, lb_ref, o_ref = refs
        z = (z_ref[...] * g_ref[...]).astype(BF16)
    else:
        z_ref, w_ref, h_ref, lg_ref, lb_ref, o_ref = refs
        z = z_ref[...].astype(BF16)
    y = DN_ALPHA * h_ref[...] + _dot(z, w_ref[...])
    o_ref[...] = _layer_norm(y, lg_ref[...], lb_ref[...])


def _mm_res_ln(z, gate, w, h, lg, lb):
    Np, D = h.shape
    TM = ROW_TM
    tile = pl.BlockSpec((TM, D), lambda i: (i, 0))
    has_gate = gate is not None
    args = [z] + ([gate] if has_gate else []) + [w, h, lg, lb]
    specs = [tile] + ([tile] if has_gate else []) + [_full(w.shape), tile, _full(lg.shape), _full(lb.shape)]
    return pl.pallas_call(
        functools.partial(_mm_res_ln_kernel, has_gate),
        out_shape=jax.ShapeDtypeStruct((Np, D), F32),
        grid=(Np // TM,),
        in_specs=specs,
        out_specs=tile,
        compiler_params=_cparams(("parallel",)),
        name="mm_res_ln",
    )(*args)


def _proj_kernel(x_ref, w_ref, o_ref):
    o_ref[...] = _dot(x_ref[...].astype(BF16), w_ref[...]).astype(o_ref.dtype)


def _proj(x, w):
    Np, D = x.shape
    Nout = w.shape[1]
    TM = ROW_TM
    return pl.pallas_call(
        _proj_kernel,
        out_shape=jax.ShapeDtypeStruct((Np, Nout), BF16),
        grid=(Np // TM,),
        in_specs=[pl.BlockSpec((TM, D), lambda i: (i, 0)), _full(w.shape)],
        out_specs=pl.BlockSpec((TM, Nout), lambda i: (i, 0)),
        compiler_params=_cparams(("parallel",)),
        name="proj",
    )(x, w)


def _attn_kernel(lam_init, lam_ref, sg_ref, q_ref, k_ref, v_ref, o_ref,
                 sa_ref, sb_ref, m1_ref, l1_ref, acc1_ref, m2_ref, l2_ref, acc2_ref):
    TQ, TK = ATT_TQ, ATT_TK
    GK = 2 * TK
    qi = pl.program_id(2)
    last_chunk = k_ref.shape[1] // TK - 1
    q = q_ref[0]
    lane = lax.broadcasted_iota(I32, q.shape, 1)
    zero = jnp.zeros_like(q)
    q12 = jnp.concatenate([jnp.where(lane < DIFF_HD, q, zero), jnp.where(lane >= DIFF_HD, q, zero)], axis=0)
    m1_ref[...] = jnp.full_like(m1_ref, -1e30)
    m2_ref[...] = jnp.full_like(m2_ref, -1e30)
    l1_ref[...] = jnp.zeros_like(l1_ref)
    l2_ref[...] = jnp.zeros_like(l2_ref)
    acc1_ref[...] = jnp.zeros_like(acc1_ref)
    acc2_ref[...] = jnp.zeros_like(acc2_ref)
    nt = (((1,), (1,)), ((), ()))
    ones = jnp.ones((TK, LANES), BF16)
    rel = lax.broadcasted_iota(I32, (TQ, TK), 1) - lax.broadcasted_iota(I32, (TQ, TK), 0)

    def chunk_start(kc):
        return pl.multiple_of(jnp.minimum(kc, last_chunk) * TK, TK)

    def halves(x):
        return [x[:, c * LANES:(c + 1) * LANES] for c in range(x.shape[1] // LANES)]

    def scores(g, buf):
        for half in range(2):
            kb = k_ref[0, pl.ds(chunk_start(2 * g + half), TK), :]
            s12 = lax.dot_general(q12, kb, nt, preferred_element_type=F32)
            buf[0, :, half * TK:(half + 1) * TK] = s12[:TQ]
            buf[1, :, half * TK:(half + 1) * TK] = s12[TQ:]

    def consume(g, buf, masked):
        vext = jnp.concatenate(
            [jnp.concatenate([v_ref[0, pl.ds(chunk_start(2 * g + half), TK), :], ones], axis=1) for half in range(2)],
            axis=0)
        for si, m_ref, l_ref, acc_ref in ((0, m1_ref, l1_ref, acc1_ref), (1, m2_ref, l2_ref, acc2_ref)):
            s = buf[si]
            if masked:
                s = jnp.concatenate(
                    [jnp.where(rel <= qi * TQ - (2 * g + half) * TK, s[:, half * TK:(half + 1) * TK], -jnp.inf)
                     for half in range(2)], axis=1)
            parts = halves(s)
            smax = jnp.max(functools.reduce(jnp.maximum, parts), axis=1, keepdims=True)
            m_old = m_ref[...]
            m_new = jnp.maximum(m_old, smax)
            alpha = jnp.exp2(m_old - m_new)
            p = jnp.exp2(s - jnp.concatenate([m_new] * len(parts), axis=1)).astype(BF16)
            d = _dot(p, vext)
            acc_ref[...] = alpha * acc_ref[...] + d[:, :LANES]
            l_ref[...] = alpha * l_ref[...] + d[:, LANES:]
            m_ref[...] = m_new

    n_free = (qi * TQ) // GK
    n_groups = (qi * TQ + TQ - 1) // GK + 1
    scores(0, sa_ref)

    def pair(i, carry):
        g = 2 * i
        scores(g + 1, sb_ref)
        consume(g, sa_ref, False)
        scores(g + 2, sa_ref)
        consume(g + 1, sb_ref, False)
        return carry

    n_pairs = n_free // 2
    lax.fori_loop(0, n_pairs, pair, 0)
    g_a = 2 * n_pairs
    two_left = n_groups - g_a == 2

    @pl.when(two_left)
    def _():
        scores(g_a + 1, sb_ref)

    consume(g_a, sa_ref, True)

    @pl.when(two_left)
    def _():
        consume(g_a + 1, sb_ref, True)

    lam_v = lam_ref[...]
    lam = (jnp.exp(jnp.sum(lam_v[0:1] * lam_v[1:2], axis=1, keepdims=True))
           - jnp.exp(jnp.sum(lam_v[2:3] * lam_v[3:4], axis=1, keepdims=True)) + lam_init)
    o = acc1_ref[...] / l1_ref[...] - lam * (acc2_ref[...] / l2_ref[...])
    o = o * lax.rsqrt(jnp.mean(o * o, axis=1, keepdims=True) + 1e-5) * sg_ref[...] * (1.0 - lam_init)
    o_ref[0] = o.astype(o_ref.dtype)


def _diff_attn(q, kv, lam4, subln_g, lam_init):
    B, La, D = q.shape
    TQ, TK = ATT_TQ, ATT_TK
    assert La % TQ == 0 and TQ % (2 * TK) == 0
    H = DIFF_HEADS
    HW = 2 * DIFF_HD
    vm = lambda *s: pltpu.VMEM(s, F32)
    return pl.pallas_call(
        functools.partial(_attn_kernel, lam_init),
        out_shape=jax.ShapeDtypeStruct((B, La, D), BF16),
        grid=(B, H, La // TQ),
        in_specs=[_full(lam4.shape), _full(subln_g.shape),
                  pl.BlockSpec((1, TQ, HW), lambda b, h, i: (b, i, h)),
                  pl.BlockSpec((1, La, HW), lambda b, h, i: (b, 0, h)),
                  pl.BlockSpec((1, La, HW), lambda b, h, i: (b, 0, H + h))],
        out_specs=pl.BlockSpec((1, TQ, HW), lambda b, h, i: (b, i, h)),
        scratch_shapes=[vm(2, TQ, 2 * TK), vm(2, TQ, 2 * TK),
                        vm(TQ, LANES), vm(TQ, LANES), vm(TQ, HW), vm(TQ, LANES), vm(TQ, LANES), vm(TQ, HW)],
        compiler_params=_cparams(("parallel", "parallel", "arbitrary")),
        name="diff_attn",
    )(lam4, subln_g, q, kv, kv)


def _route_kernel(x_ref, rw_ref, rb_ref, sg_ref, su_ref, sd_ref, tri_ref,
                  sh_o, idx_o, gate_o, rank_o, cnt_o, carry_ref):
    TM = ROUTE_TM
    G, EG = N_GROUPS, N_EXPERTS // N_GROUPS

    @pl.when(pl.program_id(0) == 0)
    def _():
        carry_ref[...] = jnp.zeros_like(carry_ref)

    x = x_ref[...]
    xb = x.astype(BF16)
    hmid = _dot(xb, sg_ref[...])
    hmid = hmid * jax.nn.sigmoid(hmid) * _dot(xb, su_ref[...])
    sh_o[...] = _dot(hmid.astype(BF16), sd_ref[...])

    logit = lax.dot_general(rw_ref[...], x, (((1,), (1,)), ((), ())),
                            preferred_element_type=F32, precision=lax.Precision.HIGHEST)
    s = jax.nn.sigmoid(logit)
    s3 = s.reshape(G, EG, TM)
    sel3 = (s + rb_ref[...]).reshape(G, EG, TM)
    io_j = lax.broadcasted_iota(I32, (G, EG, TM), 1)
    io_g = lax.broadcasted_iota(I32, (G, 1, TM), 0)
    neg = -jnp.inf
    m1 = jnp.max(sel3, axis=1, keepdims=True)
    i1 = jnp.min(jnp.where(sel3 == m1, io_j, EG), axis=1, keepdims=True)
    m2 = jnp.max(jnp.where(io_j == i1, neg, sel3), axis=1, keepdims=True)
    gsc = m1 + m2
    gkeep = jnp.zeros((G, 1, TM), F32)
    for _ in range(TOPK_GROUPS):
        m = jnp.max(gsc, axis=0, keepdims=True)
        gi = jnp.min(jnp.where(gsc == m, io_g, G), axis=0, keepdims=True)
        hit = io_g == gi
        gkeep = jnp.where(hit, 1.0, gkeep)
        gsc = jnp.where(hit, neg, gsc)
    cur = jnp.where(gkeep > 0.0, sel3, neg)
    io_e = io_g * EG + io_j
    hits, idxs, ws = [], [], []
    for _ in range(TOP_K):
        m = jnp.max(jnp.max(cur, axis=1, keepdims=True), axis=0, keepdims=True)
        ei = jnp.min(jnp.min(jnp.where(cur == m, io_e, N_EXPERTS), axis=1, keepdims=True), axis=0, keepdims=True)
        hit = io_e == ei
        ws.append(jnp.sum(jnp.sum(jnp.where(hit, s3, 0.0), axis=1, keepdims=True), axis=0, keepdims=True))
        cur = jnp.where(hit, neg, cur)
        hits.append(hit)
        idxs.append(ei)
    wsum = ws[0]
    for wv in ws[1:]:
        wsum = wsum + wv
    scale = ROUTED_SCALE / wsum
    onehot = jnp.zeros((G, EG, TM), F32)
    for hit in hits:
        onehot = jnp.where(hit, 1.0, onehot)
    oh2 = onehot.reshape(N_EXPERTS, TM)
    rank_full = (_dot(oh2.astype(BF16), tri_ref[...]) + carry_ref[:, :1]).reshape(G, EG, TM)
    for kk in range(TOP_K):
        rk = jnp.sum(jnp.sum(jnp.where(hits[kk], rank_full, 0.0), axis=1, keepdims=True), axis=0, keepdims=True)
        idx_o[kk:kk + 1, :] = idxs[kk].reshape(1, TM)
        gate_o[kk:kk + 1, :] = (ws[kk] * scale).reshape(1, TM)
        rank_o[kk:kk + 1, :] = rk.reshape(1, TM).astype(I32)
    carry_ref[...] = carry_ref[...] + jnp.sum(oh2, axis=1, keepdims=True)
    cnt_o[...] = carry_ref[...]


def _route_shared(x, rw_t, rb, sg, su, sd, tri):
    Np, D = x.shape
    TM = ROUTE_TM
    tok = pl.BlockSpec((TOP_K, TM), lambda i: (0, i))
    return pl.pallas_call(
        _route_kernel,
        out_shape=[jax.ShapeDtypeStruct((Np, D), F32),
                   jax.ShapeDtypeStruct((TOP_K, Np), I32),
                   jax.ShapeDtypeStruct((TOP_K, Np), F32),
                   jax.ShapeDtypeStruct((TOP_K, Np), I32),
                   jax.ShapeDtypeStruct((N_EXPERTS, LANES), F32)],
        grid=(Np // TM,),
        in_specs=[pl.BlockSpec((TM, D), lambda i: (i, 0)), _full(rw_t.shape), _full(rb.shape),
                  _full(sg.shape), _full(su.shape), _full(sd.shape), _full(tri.shape)],
        out_specs=[pl.BlockSpec((TM, D), lambda i: (i, 0)), tok, tok, tok, _full((N_EXPERTS, LANES))],
        scratch_shapes=[pltpu.VMEM((N_EXPERTS, LANES), F32)],
        compiler_params=_cparams(("arbitrary",)),
        name="moe_route",
    )(x, rw_t, rb, sg, su, sd, tri)


def _tile_copy(src_ref, src_row, dst_ref, dst_row, sem):
    src = src_ref.at[pl.ds(pl.multiple_of(src_row * ROW_TILES, ROW_TILES), ROW_TILES)]
    dst = dst_ref.at[pl.ds(pl.multiple_of(dst_row * ROW_TILES, ROW_TILES), ROW_TILES)]
    return pltpu.make_async_copy(src, dst, sem)


def _rows_to_tiles(x, tiles_ref, n_rows):
    for s in range(ROW_TILES):
        tiles_ref[pl.ds(s, n_rows, stride=ROW_TILES), :] = x[:, s * LANES:(s + 1) * LANES]


def _tiles_to_rows(tiles_ref, n_rows):
    return [tiles_ref[pl.ds(s, n_rows, stride=ROW_TILES), :] for s in range(ROW_TILES)]


def _dispatch_kernel(ps_ref, idx_ref, rank_ref, x_ref, xs_hbm, stage_ref, sem):
    _rows_to_tiles(x_ref[...], stage_ref, DISP_TM)

    def issue(n, c):
        for kk in range(TOP_K):
            dst = ps_ref[idx_ref[kk, n]] + rank_ref[kk, n]
            _tile_copy(stage_ref, n, xs_hbm, dst, sem).start()
        return c

    lax.fori_loop(0, DISP_TM, issue, 0)

    def drain(n, c):
        for kk in range(TOP_K):
            _tile_copy(stage_ref, n, xs_hbm, 0, sem).wait()
        return c

    lax.fori_loop(0, DISP_TM, drain, 0)


def _dispatch(x, idx_t, rank_t, pad_start, n_rows):
    Np, D = x.shape
    tok = pl.BlockSpec((TOP_K, DISP_TM), lambda i, ps: (0, i), memory_space=pltpu.SMEM)
    return pl.pallas_call(
        _dispatch_kernel,
        out_shape=jax.ShapeDtypeStruct((n_rows * ROW_TILES, LANES), F32),
        grid_spec=pltpu.PrefetchScalarGridSpec(
            num_scalar_prefetch=1,
            grid=(Np // DISP_TM,),
            in_specs=[tok, tok, pl.BlockSpec((DISP_TM, D), lambda i, ps: (i, 0))],
            out_specs=pl.BlockSpec(memory_space=pl.ANY),
            scratch_shapes=[pltpu.VMEM((DISP_TM * ROW_TILES, LANES), F32), pltpu.SemaphoreType.DMA(())],
        ),
        compiler_params=_cparams(("arbitrary",)),
        name="moe_dispatch",
    )(pad_start, idx_t, rank_t, x)


def _expert_kernel(be_ref, nu_ref, xs_ref, wg_ref, wu_ref, wd_ref, ys_ref):
    @pl.when(pl.program_id(0) < nu_ref[0])
    def _():
        x = jnp.concatenate(_tiles_to_rows(xs_ref, MOE_BLK), axis=1).astype(BF16)
        g = _dot(x, wg_ref[0])
        hmid = g * jax.nn.sigmoid(g) * _dot(x, wu_ref[0])
        _rows_to_tiles(_dot(hmid.astype(BF16), wd_ref[0]), ys_ref, MOE_BLK)


def _experts(xs, block_e, n_used, wg, wu, wd):
    D, F = wg.shape[1], wg.shape[2]
    nb = xs.shape[0] // (MOE_BLK * ROW_TILES)
    tiles = pl.BlockSpec((MOE_BLK * ROW_TILES, LANES), lambda i, be, nu: (jnp.minimum(i, nu[0] - 1), 0))
    wspec = lambda s: pl.BlockSpec((1,) + s, lambda i, be, nu: (be[jnp.minimum(i, nu[0] - 1)], 0, 0))
    return pl.pallas_call(
        _expert_kernel,
        out_shape=jax.ShapeDtypeStruct(xs.shape, F32),
        grid_spec=pltpu.PrefetchScalarGridSpec(
            num_scalar_prefetch=2,
            grid=(nb,),
            in_specs=[tiles, wspec((D, F)), wspec((D, F)), wspec((F, D))],
            out_specs=tiles,
        ),
        compiler_params=_cparams(("arbitrary",)),
        name="moe_experts",
    )(block_e, n_used, xs, wg, wu, wd)


def _combine_kernel(ps_ref, idx_ref, rank_ref, gate_ref, sh_ref, h_ref, lg_ref, lb_ref, ys_hbm, o_ref,
                    buf, sem):
    def issue(n, c):
        for kk in range(TOP_K):
            src = ps_ref[idx_ref[kk, n]] + rank_ref[kk, n]
            _tile_copy(ys_hbm, src, buf.at[kk], n, sem).start()
        return c

    lax.fori_loop(0, COMB_TM, issue, 0)

    def drain(n, c):
        for kk in range(TOP_K):
            _tile_copy(ys_hbm, 0, buf.at[kk], n, sem).wait()
        return c

    lax.fori_loop(0, COMB_TM, drain, 0)
    groups = []
    for s in range(ROW_TILES):
        acc = None
        for kk in range(TOP_K):
            part = gate_ref[kk] * buf[kk, pl.ds(s, COMB_TM, stride=ROW_TILES), :]
            acc = part if kk == 0 else acc + part
        groups.append(acc)
    ffn = sh_ref[...] + jnp.concatenate(groups, axis=1)
    o_ref[...] = _layer_norm(DN_ALPHA * h_ref[...] + ffn, lg_ref[...], lb_ref[...])


def _combine(ys, idx_t, rank_t, pad_start, gate, shared, h, lg, lb):
    Np, D = h.shape
    TM = COMB_TM
    tok = pl.BlockSpec((TOP_K, TM), lambda i, ps: (0, i), memory_space=pltpu.SMEM)
    tile = pl.BlockSpec((TM, D), lambda i, ps: (i, 0))
    vec = pl.BlockSpec((1, D), lambda i, ps: (0, 0))
    return pl.pallas_call(
        _combine_kernel,
        out_shape=jax.ShapeDtypeStruct((Np, D), F32),
        grid_spec=pltpu.PrefetchScalarGridSpec(
            num_scalar_prefetch=1,
            grid=(Np // TM,),
            in_specs=[tok, tok, pl.BlockSpec((TOP_K, TM, LANES), lambda i, ps: (0, i, 0)), tile, tile, vec, vec,
                      pl.BlockSpec(memory_space=pl.ANY)],
            out_specs=tile,
            scratch_shapes=[pltpu.VMEM((TOP_K, TM * ROW_TILES, LANES), F32), pltpu.SemaphoreType.DMA(())],
        ),
        compiler_params=_cparams(("arbitrary",)),
        name="moe_combine",
    )(pad_start, idx_t, rank_t, gate, shared, h, lg, lb, ys)


def _moe_layer(h2, l, lg, lb, router_w, router_b, w_gate, w_up, w_down, sh_gate, sh_up, sh_down, tri):
    Np, D = h2.shape
    shared, idx_t, gate_t, rank_t, cnt = _route_shared(
        h2, router_w[l].T, router_b[l].reshape(N_EXPERTS, 1),
        sh_gate[l].astype(BF16), sh_up[l].astype(BF16), sh_down[l].astype(BF16), tri)
    counts = cnt[:, 0].astype(I32)
    padded = (counts + MOE_BLK - 1) // MOE_BLK * MOE_BLK
    pad_end = jnp.cumsum(padded)
    pad_start = (pad_end - padded).astype(I32)
    n_blocks = Np * TOP_K // MOE_BLK + N_EXPERTS
    n_used = (pad_end[-1:] // MOE_BLK).astype(I32)
    blk_row0 = jnp.arange(n_blocks, dtype=I32) * MOE_BLK
    block_e = jnp.minimum(jnp.sum((pad_end[None, :] <= blk_row0[:, None]).astype(I32), axis=1), N_EXPERTS - 1)
    xs = _dispatch(h2, idx_t, rank_t, pad_start, n_blocks * MOE_BLK)
    ys = _experts(xs, block_e, n_used, w_gate[l].astype(BF16), w_up[l].astype(BF16), w_down[l].astype(BF16))
    gate_splat = jnp.broadcast_to(gate_t[:, :, None], (TOP_K, Np, LANES))
    return _combine(ys, idx_t, rank_t, pad_start, gate_splat, shared, h2, lg, lb)


def _pad_cols(w, n):
    return jnp.pad(w, ((0, 0), (0, n - w.shape[1])))


def _pad_rows(w, n):
    return jnp.pad(w, ((0, n - w.shape[0]), (0, 0)))


def _trunk(x, meta_tokens, ln_mix_g, ln_mix_b, ln_ffn_g, ln_ffn_b,
           rw_mu, rw_w_rkv, rw_w0, rw_w_l1, rw_w_l2, rw_a0, rw_a_l1, rw_a_l2,
           rw_g_l1, rw_g_l2, rw_k_k, rw_k_a, rw_r_k, rw_lnx_g, rw_lnx_b, rw_w_out,
           rw_v0, rw_v_l1, rw_v_l2, kv_w,
           da_w_q, da_lam_q1, da_lam_k1, da_lam_q2, da_lam_k2, da_subln_g, da_w_out,
           moe_router_w, moe_router_b, moe_w_gate, moe_w_up, moe_w_down,
           moe_sh_gate, moe_sh_up, moe_sh_down):
    B, S, D = x.shape
    L = S + N_META
    Lp = -(-L // SEQ_ALIGN) * SEQ_ALIGN
    Np = B * Lp
    La = -(-Lp // ATT_TQ) * ATT_TQ
    assert D == D_MODEL and Np % ROW_TM == 0 and Lp % PROJ_TM == 0 and Lp % SCAN_TC == 0
    assert Np % ROUTE_TM == 0 and Np % DISP_TM == 0 and Np % COMB_TM == 0 and (Np * TOP_K) % MOE_BLK == 0
    meta = jnp.broadcast_to(meta_tokens[None].astype(x.dtype), (B, N_META, D))
    h = jnp.concatenate([meta, x, jnp.zeros((B, Lp - L, D), x.dtype)], axis=1)

    pc = _perm_cols()
    tri = jnp.asarray(np.triu(np.ones((ROUTE_TM, ROUTE_TM), np.float32), 1)).astype(BF16)
    row = lambda vec: vec.reshape(1, -1)
    lane = np.arange(2 * D_MODEL)
    m_key, m_head = lane // 32, lane % 16
    c128 = np.arange(LANES) % 16
    vrow = np.arange(RWKV_HEAD)

    v_first = None
    kv = None
    for l in range(DEPTH):
        if l < N_A_LAYERS:
            p = {
                "mu": _pad_rows(rw_mu[l], SUBLANES),
                "wr": rw_w_rkv[l, 0][:, pc].astype(BF16),
                "wk": rw_w_rkv[l, 1][:, pc].astype(BF16),
                "wv": rw_w_rkv[l, 2][:, pc].astype(BF16),
                "w0": row(rw_w0[l][pc]),
                "wl1": _pad_cols(rw_w_l1[l], LANES).astype(BF16),
                "wl2": _pad_rows(rw_w_l2[l][:, pc], LANES).astype(BF16),
                "a0": row(rw_a0[l][pc]),
                "al1": _pad_cols(rw_a_l1[l], LANES).astype(BF16),
                "al2": _pad_rows(rw_a_l2[l][:, pc], LANES).astype(BF16),
                "gl1": _pad_cols(rw_g_l1[l], 2 * LANES).astype(BF16),
                "gl2": _pad_rows(rw_g_l2[l][:, pc], 2 * LANES).astype(BF16),
                "kk": row(rw_k_k[l][pc]),
                "ka": row(rw_k_a[l][pc]),
            }
            if l > 0:
                p["v0"] = row(rw_v0[l - 1][pc])
                p["vl1"] = _pad_cols(rw_v_l1[l - 1], LANES).astype(BF16)
                p["vl2"] = _pad_rows(rw_v_l2[l - 1][:, pc], LANES).astype(BF16)
            r, w, k, v, a, b, g = _rwkv_proj(h, v_first if l > 0 else None, p)
            if l == 0:
                v_first = v
            rk_m = rw_r_k[l][m_head, m_key].reshape(1, 2 * D_MODEL)
            lg_t = rw_lnx_g[l].reshape(RWKV_HEADS, RWKV_HEAD)[c128[None, :], vrow[:, None]]
            lb_t = rw_lnx_b[l].reshape(RWKV_HEADS, RWKV_HEAD)[c128[None, :], vrow[:, None]]
            z = _wkv_scan(r, w, k, v, a, b, rk_m, lg_t, lb_t)
            h2 = _mm_res_ln(z.reshape(Np, D), g.reshape(Np, D), rw_w_out[l][pc, :].astype(BF16),
                            h.reshape(Np, D), row(ln_mix_g[l]), row(ln_mix_b[l]))
        else:
            j = l - N_A_LAYERS
            h2 = h.reshape(Np, D)
            pad_t = lambda z: jnp.pad(z, ((0, 0), (0, La - Lp), (0, 0)))
            if kv is None:
                kv = pad_t(_proj(h2, kv_w.astype(BF16)).reshape(B, Lp, 2 * D))
            q_scale = DIFF_HD ** -0.5 * math.log2(math.e)
            q = _proj(h2, (da_w_q[j] * q_scale).astype(BF16)).reshape(B, Lp, D)
            lam_init = 0.8 - 0.6 * math.exp(-0.3 * l)
            lam4 = jnp.stack([da_lam_q1[j], da_lam_k1[j], da_lam_q2[j], da_lam_k2[j]])
            o = _diff_attn(pad_t(q), kv, _pad_rows(lam4, SUBLANES), row(da_subln_g[j]), lam_init)[:, :Lp]
            h2 = _mm_res_ln(o.reshape(Np, D), None, da_w_out[j].astype(BF16), h2,
                            row(ln_mix_g[l]), row(ln_mix_b[l]))
        h2 = _moe_layer(h2, l, row(ln_ffn_g[l]), row(ln_ffn_b[l]), moe_router_w, moe_router_b,
                        moe_w_gate, moe_w_up, moe_w_down, moe_sh_gate, moe_sh_up, moe_sh_down, tri)
        h = h2.reshape(B, Lp, D)
    return h[:, N_META:L]


_trunk_jit = jax.jit(_trunk)


def kernel(x, meta_tokens, ln_mix_g, ln_mix_b, ln_ffn_g, ln_ffn_b, rw_mu, rw_w_rkv, rw_w0, rw_w_l1, rw_w_l2, rw_a0, rw_a_l1, rw_a_l2, rw_g_l1, rw_g_l2, rw_k_k, rw_k_a, rw_r_k, rw_lnx_g, rw_lnx_b, rw_w_out, rw_v0, rw_v_l1, rw_v_l2, kv_w, da_w_q, da_lam_q1, da_lam_k1, da_lam_q2, da_lam_k2, da_subln_g, da_w_out, moe_router_w, moe_router_b, moe_w_gate, moe_w_up, moe_w_down, moe_sh_gate, moe_sh_up, moe_sh_down):
    return _trunk_jit(x, meta_tokens, ln_mix_g, ln_mix_b, ln_ffn_g, ln_ffn_b, rw_mu, rw_w_rkv, rw_w0, rw_w_l1,
                      rw_w_l2, rw_a0, rw_a_l1, rw_a_l2, rw_g_l1, rw_g_l2, rw_k_k, rw_k_a, rw_r_k, rw_lnx_g,
                      rw_lnx_b, rw_w_out, rw_v0, rw_v_l1, rw_v_l2, kv_w, da_w_q, da_lam_q1, da_lam_k1,
                      da_lam_q2, da_lam_k2, da_subln_g, da_w_out, moe_router_w, moe_router_b, moe_w_gate,
                      moe_w_up, moe_w_down, moe_sh_gate, moe_sh_up, moe_sh_down)
```
